```python
import jax, jax.numpy as jnp
from jax import lax
import numpy as np

D_MODEL = 2048
BATCH = 16
SEQ = 256
DEPTH = 2
DEC_BATCH = 4
DEC_SEQ = 2048
PAST_LEN = 512

GRID_W = 64
HEAD_DIM = 64
GROUP_W = D_MODEL // 4
NA_HEADS = GROUP_W // HEAD_DIM
NA_KH = 8
NA_KW = 16
NA_QCOLS = 16
NA_KCOLS = NA_QCOLS + NA_KW
MLA_HEADS = 4
MLA_NOPE = 128
MLA_ROPE = 64
MLA_V = GROUP_W // MLA_HEADS
MLA_Q_LORA = 384
MLA_KV_LORA = 128
WIN_HEADS = GROUP_W // HEAD_DIM
WIN_KV_HEADS = 2
WIN_GROUPS = WIN_HEADS // WIN_KV_HEADS
WINDOW = 128
FN_GROUPS = 4
FN_CH = GROUP_W // FN_GROUPS
D_FF = 5632
MOD_CHUNKS = 6
Q_BLOCK = 128
ROPE_BASE = 10000.0
EPS = 1e-6
IN_COLS = 3 * GROUP_W + MLA_Q_LORA + MLA_KV_LORA + MLA_ROPE + WIN_HEADS * HEAD_DIM + 2 * WIN_KV_HEADS * HEAD_DIM + GROUP_W

kernel_name = 'hybrid_diffusion_prefix_trunk_step'

F32 = jnp.float32


def rms_norm(x, g):
    xf = x.astype(F32)
    y = xf * lax.rsqrt(jnp.mean(xf * xf, axis=-1, keepdims=True) + EPS)
    return (y * g.astype(F32)).astype(x.dtype)


def axial_rope(x, row, col):
    d = x.shape[-1]
    half = d // 2
    quarter = half // 2
    inv = ROPE_BASE ** (-jnp.arange(quarter, dtype=F32) / quarter)
    lead = (1, x.shape[1]) + (1,) * (x.ndim - 3)

    def rot(xh, pos):
        ang = (pos.astype(F32)[:, None] * inv[None, :]).reshape(lead + (quarter,))
        c, s = jnp.cos(ang), jnp.sin(ang)
        x1, x2 = xh[..., :quarter], xh[..., quarter:]
        return jnp.concatenate([x1 * c - x2 * s, x1 * s + x2 * c], -1)

    xf = x.astype(F32)
    return jnp.concatenate([rot(xf[..., :half], row), rot(xf[..., half:], col)], -1).astype(x.dtype)


def rope_tail(t, row, col):
    return jnp.concatenate([t[..., :MLA_NOPE], axial_rope(t[..., MLA_NOPE:], row, col)], -1)


def block_attention(q, k, v, sink=None):
    B, Sq, Hk, G, dk = q.shape
    nb = Sq // Q_BLOCK
    scale = dk ** -0.5
    qb = jnp.moveaxis(q.reshape(B, nb, Q_BLOCK, Hk, G, dk), 1, 0)

    def one(qi):
        s = jnp.einsum('bqhgd,bkhd->bhgqk', qi, k).astype(F32) * scale
        if sink is not None:
            s = jnp.concatenate([s, jnp.broadcast_to(sink.astype(F32)[None, :, :, None, None], s.shape[:-1] + (1,))], -1)
        p = jax.nn.softmax(s, axis=-1)
        if sink is not None:
            p = p[..., :-1]
        return jnp.einsum('bhgqk,bkhd->bqhgd', p.astype(v.dtype), v)

    o = lax.map(one, qb)
    return jnp.moveaxis(o, 0, 1).reshape(B, Sq, Hk, G, v.shape[-1])


def neighborhood_attention(q, k, v, k_ctx, v_ctx, rpb):
    B, S, H, d = q.shape
    rows = S // GRID_W
    kh = min(NA_KH, rows)
    ncb = GRID_W // NA_QCOLS
    r = jnp.arange(rows)
    ridx = jnp.clip(r - kh // 2, 0, rows - kh)[:, None] + jnp.arange(kh)[None, :]
    qcol = jnp.arange(ncb)[:, None] * NA_QCOLS + jnp.arange(NA_QCOLS)[None, :]
    cidx = jnp.clip(jnp.arange(ncb) * NA_QCOLS - NA_KW // 2, 0, GRID_W - NA_KCOLS)[:, None] + jnp.arange(NA_KCOLS)[None, :]
    cs = jnp.clip(qcol - NA_KW // 2, 0, GRID_W - NA_KW)
    kcol = cidx[:, None, :]
    col_ok = (kcol >= cs[..., None]) & (kcol < cs[..., None] + NA_KW)
    dc = jnp.clip(kcol - qcol[..., None], 1 - NA_KW, NA_KW - 1) + NA_KW - 1
    dr = ridx - r[:, None] + NA_KH - 1
    bias = rpb[:, dr[:, None, None, :, None], dc[None, :, :, None, :]].astype(F32)

    def gather(t):
        return t.reshape(B, rows, GRID_W, H, d)[:, ridx[:, None, :, None], cidx[None, :, None, :]]

    kg, vg = gather(k), gather(v)
    qb = q.reshape(B, rows, ncb, NA_QCOLS, H, d)
    scale = d ** -0.5
    s_loc = jnp.einsum('brjqhd,brjakhd->bhrjqak', qb, kg).astype(F32) * scale + bias[None]
    n_loc = kh * NA_KCOLS
    s_loc = jnp.where(col_ok[:, :, None, :], s_loc, -jnp.inf).reshape(B, H, rows, ncb, NA_QCOLS, n_loc)
    s_ctx = jnp.einsum('brjqhd,blhd->bhrjql', qb, k_ctx).astype(F32) * scale
    p = jax.nn.softmax(jnp.concatenate([s_loc, s_ctx], -1), axis=-1).astype(v.dtype)
    o = (jnp.einsum('bhrjqn,brjnhd->brjqhd', p[..., :n_loc], vg.reshape(B, rows, ncb, n_loc, H, d))
         + jnp.einsum('bhrjql,blhd->brjqhd', p[..., n_loc:], v_ctx))
    return o.reshape(B, S, H, d)


def window_attention(q, k, v, k_ctx, v_ctx, sink):
    B, S, Hk, G, d = q.shape
    nb = S // WINDOW

    def band(t):
        tb = jnp.pad(t, ((0, 0), (WINDOW, WINDOW), (0, 0), (0, 0))).reshape(B, nb + 2, WINDOW, Hk, d)
        return jnp.concatenate([tb[:, :-2], tb[:, 1:-1], tb[:, 2:]], axis=2)

    kb, vb = band(k), band(v)
    qb = q.reshape(B, nb, WINDOW, Hk, G, d)
    scale = d ** -0.5
    qpos = jnp.arange(nb)[:, None, None] * WINDOW + jnp.arange(WINDOW)[None, :, None]
    kpos = jnp.arange(nb)[:, None, None] * WINDOW - WINDOW + jnp.arange(3 * WINDOW)[None, None, :]
    ok = (jnp.abs(kpos - qpos) <= WINDOW) & (kpos >= 0) & (kpos < S)
    s_loc = jnp.where(ok, jnp.einsum('bnqhgd,bnkhd->bhgnqk', qb, kb).astype(F32) * scale, -jnp.inf)
    s_ctx = jnp.einsum('bnqhgd,blhd->bhgnql', qb, k_ctx).astype(F32) * scale
    s_sink = jnp.broadcast_to(sink.astype(F32)[None, :, :, None, None, None], s_ctx.shape[:-1] + (1,))
    p = jax.nn.softmax(jnp.concatenate([s_loc, s_ctx, s_sink], -1), axis=-1).astype(v.dtype)
    nl = 3 * WINDOW
    lc = k_ctx.shape[1]
    o = (jnp.einsum('bhgnqk,bnkhd->bnqhgd', p[..., :nl], vb)
         + jnp.einsum('bhgnql,blhd->bnqhgd', p[..., nl:nl + lc], v_ctx))
    return o.reshape(B, S, Hk, G, d)


def fourier_mix(t):
    B, S, _ = t.shape
    tf = t.astype(F32).reshape(B, S, FN_GROUPS, FN_CH)
    y = jnp.fft.fft2(tf, axes=(1, 3), norm='ortho').real
    return y.reshape(B, S, GROUP_W).astype(t.dtype)


def conv_ffn(h, p):
    g = h @ p['w_gate']
    gp = jnp.pad(g, ((0, 0), (1, 1), (0, 0)))
    wc = p['w_conv']
    g = gp[:, :-2] * wc[0] + gp[:, 1:-1] * wc[1] + gp[:, 2:] * wc[2]
    return (jax.nn.silu(g) * (h @ p['w_up'])) @ p['w_down']


def mixer_heads(h, p):
    B, S, _ = h.shape
    sizes = (GROUP_W, GROUP_W, GROUP_W, MLA_Q_LORA, MLA_KV_LORA + MLA_ROPE,
             WIN_HEADS * HEAD_DIM, WIN_KV_HEADS * HEAD_DIM, WIN_KV_HEADS * HEAD_DIM, GROUP_W)
    idx = np.cumsum(sizes)[:-1].tolist()
    a_q, a_k, a_v, m_q, m_kv, c_q, c_k, c_v, f_v = jnp.split(h @ p['w_in'], idx, axis=-1)
    cq = rms_norm(m_q, p['g_q_lora'])
    return dict(
        q_na=rms_norm(a_q.reshape(B, S, NA_HEADS, HEAD_DIM), p['g_qn_na']),
        k_na=rms_norm(a_k.reshape(B, S, NA_HEADS, HEAD_DIM), p['g_kn_na']),
        v_na=a_v.reshape(B, S, NA_HEADS, HEAD_DIM),
        q_mla=rms_norm((cq @ p['w_q_up']).reshape(B, S, MLA_HEADS, MLA_NOPE + MLA_ROPE), p['g_qn_mla']),
        ckv=rms_norm(m_kv[..., :MLA_KV_LORA], p['g_kv_lora']),
        krope=m_kv[..., MLA_KV_LORA:],
        q_win=rms_norm(c_q.reshape(B, S, WIN_KV_HEADS, WIN_GROUPS, HEAD_DIM), p['g_qn_win']),
        k_win=rms_norm(c_k.reshape(B, S, WIN_KV_HEADS, HEAD_DIM), p['g_kn_win']),
        v_win=c_v.reshape(B, S, WIN_KV_HEADS, HEAD_DIM),
        v_fn=f_v,
    )


def mla_kv(ckv, krope, p):
    B, L, _ = ckv.shape
    kv = (ckv @ p['w_kv_up']).reshape(B, L, MLA_HEADS, MLA_NOPE + MLA_V)
    k = jnp.concatenate([kv[..., :MLA_NOPE], jnp.broadcast_to(krope[:, :, None, :], (B, L, MLA_HEADS, MLA_ROPE))], -1)
    return rms_norm(k, p['g_kn_mla']), kv[..., MLA_NOPE:]


def modulate(x, g, shift, scale):
    return rms_norm(x, g) * (1 + scale) + shift


def context_layer(x, c_ctx, p):
    B, L, _ = x.shape
    mod = jax.nn.silu(c_ctx) @ p['w_mod'] + p['b_mod']
    sh1, sc1, gt1, sh2, sc2, gt2 = jnp.split(mod, MOD_CHUNKS, axis=-1)
    hd = mixer_heads(modulate(x, p['g_mix'], sh1, sc1), p)
    k_m, v_m = mla_kv(hd['ckv'], hd['krope'], p)
    o_a = block_attention(hd['q_na'][:, :, :, None], hd['k_na'], hd['v_na'])
    o_b = block_attention(hd['q_mla'][:, :, :, None], k_m, v_m)
    o_c = block_attention(hd['q_win'], hd['k_win'], hd['v_win'], p['sink_win'].reshape(WIN_KV_HEADS, WIN_GROUPS))
    o_d = fourier_mix(hd['v_fn'])
    o = jnp.concatenate([o_a.reshape(B, L, GROUP_W), o_b.reshape(B, L, GROUP_W), o_c.reshape(B, L, GROUP_W), o_d], -1)
    x = x + gt1 * (o @ p['w_out'])
    x = x + gt2 * conv_ffn(modulate(x, p['g_ffn'], sh2, sc2), p)
    return x, (hd['k_na'], hd['v_na'], hd['ckv'], hd['krope'], hd['k_win'], hd['v_win'])


def latent_layer(x, c, ctx, p):
    B, S, _ = x.shape
    na_k, na_v, m_ckv, m_krope, w_k, w_v = ctx
    t = jnp.arange(S)
    row, col = t // GRID_W, t % GRID_W
    mod = jax.nn.silu(c) @ p['w_mod'] + p['b_mod']
    sh1, sc1, gt1, sh2, sc2, gt2 = [m[:, None, :] for m in jnp.split(mod, MOD_CHUNKS, axis=-1)]
    hd = mixer_heads(modulate(x, p['g_mix'], sh1, sc1), p)
    o_a = neighborhood_attention(hd['q_na'], hd['k_na'], hd['v_na'], na_k, na_v, p['rpb_na'])
    q_m = rope_tail(hd['q_mla'], row, col)
    k_l, v_l = mla_kv(hd['ckv'], hd['krope'], p)
    k_l = rope_tail(k_l, row, col)
    k_c, v_c = mla_kv(m_ckv, m_krope, p)
    o_b = block_attention(q_m[:, :, :, None], jnp.concatenate([k_l, k_c], 1), jnp.concatenate([v_l, v_c], 1))
    o_c = window_attention(axial_rope(hd['q_win'], row, col), axial_rope(hd['k_win'], row, col), hd['v_win'],
                           w_k, w_v, p['sink_win'].reshape(WIN_KV_HEADS, WIN_GROUPS))
    o_d = fourier_mix(hd['v_fn'])
    o = jnp.concatenate([o_a.reshape(B, S, GROUP_W), o_b.reshape(B, S, GROUP_W), o_c.reshape(B, S, GROUP_W), o_d], -1)
    x = x + gt1 * (o @ p['w_out'])
    x = x + gt2 * conv_ffn(modulate(x, p['g_ffn'], sh2, sc2), p)
    return x


def setup_inputs(seed: int = 0) -> dict:
    key = jax.random.key(seed)
    ks = iter(jax.random.split(key, 40))

    def nrm(shape, s=1.0):
        return jax.random.normal(next(ks), shape, F32) * s

    def gain(shape):
        return 1.0 + 0.05 * nrm(shape)

    D = D_MODEL
    return {
        'x_prompt': nrm((BATCH, SEQ, D)),
        'x_sample': nrm((DEC_BATCH, DEC_SEQ, D)),
        'cache_na_k': nrm((DEC_BATCH, DEPTH, PAST_LEN, NA_HEADS, HEAD_DIM)),
        'cache_na_v': nrm((DEC_BATCH, DEPTH, PAST_LEN, NA_HEADS, HEAD_DIM)),
        'cache_mla_ckv': nrm((DEC_BATCH, DEPTH, PAST_LEN, MLA_KV_LORA)),
        'cache_mla_krope': nrm((DEC_BATCH, DEPTH, PAST_LEN, MLA_ROPE)),
        'cache_win_k': nrm((DEC_BATCH, DEPTH, PAST_LEN, WIN_KV_HEADS, HEAD_DIM)),
        'cache_win_v': nrm((DEC_BATCH, DEPTH, PAST_LEN, WIN_KV_HEADS, HEAD_DIM)),
        'c': nrm((DEC_BATCH, D)),
        'c_ctx': nrm((D,)),
        'w_mod': nrm((DEPTH, D, MOD_CHUNKS * D), 0.5 * D ** -0.5),
        'b_mod': nrm((DEPTH, MOD_CHUNKS * D), 0.02),
        'g_mix': gain((DEPTH, D)),
        'g_ffn': gain((DEPTH, D)),
        'w_in': nrm((DEPTH, D, IN_COLS), D ** -0.5),
        'g_qn_na': gain((DEPTH, HEAD_DIM)),
        'g_kn_na': gain((DEPTH, HEAD_DIM)),
        'rpb_na': nrm((DEPTH, NA_HEADS, 2 * NA_KH - 1, 2 * NA_KW - 1), 0.1),
        'g_q_lora': gain((DEPTH, MLA_Q_LORA)),
        'w_q_up': nrm((DEPTH, MLA_Q_LORA, MLA_HEADS * (MLA_NOPE + MLA_ROPE)), MLA_Q_LORA ** -0.5),
        'g_kv_lora': gain((DEPTH, MLA_KV_LORA)),
        'w_kv_up': nrm((DEPTH, MLA_KV_LORA, MLA_HEADS * (MLA_NOPE + MLA_V)), MLA_KV_LORA ** -0.5),
        'g_qn_mla': gain((DEPTH, MLA_NOPE + MLA_ROPE)),
        'g_kn_mla': gain((DEPTH, MLA_NOPE + MLA_ROPE)),
        'g_qn_win': gain((DEPTH, HEAD_DIM)),
        'g_kn_win': gain((DEPTH, HEAD_DIM)),
        'sink_win': nrm((DEPTH, WIN_HEADS), 0.5),
        'w_out': nrm((DEPTH, D, D), D ** -0.5),
        'w_gate': nrm((DEPTH, D, D_FF), D ** -0.5),
        'w_up': nrm((DEPTH, D, D_FF), D ** -0.5),
        'w_conv': nrm((DEPTH, 3, D_FF), 3 ** -0.5),
        'w_down': nrm((DEPTH, D_FF, D), D_FF ** -0.5),
    }


def reference(x_prompt, x_sample, cache_na_k, cache_na_v, cache_mla_ckv, cache_mla_krope, cache_win_k, cache_win_v,
              c, c_ctx, w_mod, b_mod, g_mix, g_ffn, w_in, g_qn_na, g_kn_na, rpb_na, g_q_lora, w_q_up, g_kv_lora,
              w_kv_up, g_qn_mla, g_kn_mla, g_qn_win, g_kn_win, sink_win, w_out, w_gate, w_up, w_conv, w_down):
    params = [dict(w_mod=w_mod[l], b_mod=b_mod[l], g_mix=g_mix[l], g_ffn=g_ffn[l], w_in=w_in[l],
                   g_qn_na=g_qn_na[l], g_kn_na=g_kn_na[l], rpb_na=rpb_na[l], g_q_lora=g_q_lora[l],
                   w_q_up=w_q_up[l], g_kv_lora=g_kv_lora[l], w_kv_up=w_kv_up[l], g_qn_mla=g_qn_mla[l],
                   g_kn_mla=g_kn_mla[l], g_qn_win=g_qn_win[l], g_kn_win=g_kn_win[l], sink_win=sink_win[l],
                   w_out=w_out[l], w_gate=w_gate[l], w_up=w_up[l], w_conv=w_conv[l], w_down=w_down[l])
              for l in range(DEPTH)]

    xp = x_prompt
    per_layer = []
    for l in range(DEPTH):
        xp, st = context_layer(xp, c_ctx, params[l])
        per_layer.append(st)
    new_na_k = jnp.stack([s[0] for s in per_layer], axis=1)
    new_na_v = jnp.stack([s[1] for s in per_layer], axis=1)
    new_mla_ckv = jnp.stack([s[2] for s in per_layer], axis=1)
    new_mla_krope = jnp.stack([s[3] for s in per_layer], axis=1)
    new_win_k = jnp.stack([s[4] for s in per_layer], axis=1)
    new_win_v = jnp.stack([s[5] for s in per_layer], axis=1)

    xs = x_sample
    for l in range(DEPTH):
        ctx = (cache_na_k[:, l], cache_na_v[:, l], cache_mla_ckv[:, l], cache_mla_krope[:, l],
               cache_win_k[:, l], cache_win_v[:, l])
        xs = latent_layer(xs, c, ctx, params[l])

    return (xp, xs, new_na_k, new_na_v, new_mla_ckv, new_mla_krope, new_win_k, new_win_v)
```

```python
import functools
import math

import numpy as np
import jax
import jax.numpy as jnp
from jax import lax
from jax.experimental import pallas as pl
from jax.experimental.pallas import tpu as pltpu

F32 = jnp.float32
BF16 = jnp.bfloat16

D_MODEL = 2048
DEPTH = 2
SEQ = 256
DEC_SEQ = 2048
PAST_LEN = 512
GRID_W = 64
HEAD_DIM = 64
GROUP_W = 512
MLA_HEADS = 4
MLA_NOPE = 128
MLA_ROPE = 64
MLA_QK = MLA_NOPE + MLA_ROPE
MLA_Q_LORA = 384
MLA_KV_LORA = 128
WINDOW = 128
FN_GROUPS = 4
FN_CH = 128
D_FF = 5632
MOD_CHUNKS = 6
ROPE_BASE = 10000.0
EPS = 1e-6
NEG = -1e30

LANES = 128
MOD_ROWS = 8
VMEM_LIMIT = 56 * 1024 * 1024

C_AQ, C_AK, C_AV = 0, 512, 1024
C_MQ = 1536
C_CKV = 1920
C_KR = 2048
C_CQ = 2176
C_CK = 2688
C_CV = 2944
C_FV = 3200
IN_COLS_R = 3712

TM_PROJ = 256
TM_OUT = 256
TM_FFN = 512
TF_FFN = 512
HALO = 16
TQ = 256
NA_KROWS = 12
NA_TILES = 5


def _cp(sem):
    return pltpu.CompilerParams(dimension_semantics=sem, vmem_limit_bytes=VMEM_LIMIT)


def _const_spec(shape):
    n = len(shape)
    return pl.BlockSpec(shape, lambda *_: (0,) * n)


def _dot(a, b):
    return jnp.dot(a, b, preferred_element_type=F32)


def _dot_nt(a, b):
    return lax.dot_general(a, b, (((1,), (1,)), ((), ())), preferred_element_type=F32)


def _rms(t, n):
    return t * lax.rsqrt(jnp.sum(t * t, axis=-1, keepdims=True) * (1.0 / n) + EPS)


def _seg_rms64(t):
    rows, width = t.shape
    lo = lax.broadcasted_iota(jnp.int32, (rows, LANES), 1) < HEAD_DIM
    outs = []
    for c in range(width // LANES):
        tc = t[:, c * LANES:(c + 1) * LANES]
        sq = tc * tc
        s_lo = jnp.sum(jnp.where(lo, sq, 0.0), axis=-1, keepdims=True)
        s_hi = jnp.sum(jnp.where(lo, 0.0, sq), axis=-1, keepdims=True)
        ms = jnp.where(lo, s_lo, s_hi) * (1.0 / HEAD_DIM)
        outs.append(tc * lax.rsqrt(ms + EPS))
    return jnp.concatenate(outs, axis=-1)


def _rope128(t, cos, sin_signed):
    first = (lax.broadcasted_iota(jnp.int32, t.shape, 1) % 32) < 16
    partner = jnp.where(first, pltpu.roll(t, LANES - 16, 1), pltpu.roll(t, 16, 1))
    return t * cos + partner * sin_signed


def _rope_wide(t, cos, sin_signed):
    return jnp.concatenate(
        [_rope128(t[:, c * LANES:(c + 1) * LANES], cos, sin_signed) for c in range(t.shape[1] // LANES)], axis=-1)


MOD_TN = 512


def _mod_kernel(ct_ref, w_ref, b_ref, o_ref, *, nvec):
    tn = w_ref.shape[1]

    def body(kc, accs):
        k0 = pl.multiple_of(kc * 8, 8)
        w = w_ref[pl.ds(k0, 8), :]
        cv = ct_ref[pl.ds(k0, 8), :]
        sx = cv / (1.0 + jnp.exp(-cv))
        return tuple(accs[v] + sx[:, v:v + 1] * w for v in range(nvec))

    zero = jnp.zeros((8, tn), F32)
    accs = lax.fori_loop(0, w_ref.shape[0] // 8, body, (zero,) * nvec)
    rows = [jnp.sum(a, axis=0, keepdims=True) for a in accs]
    rows.append(jnp.zeros((MOD_ROWS - nvec, tn), F32))
    o_ref[...] = jnp.concatenate(rows, axis=0) + b_ref[...]


def _modulation(cvecs, w_mod, b_mod):
    nvec = cvecs.shape[0]
    ct = jnp.zeros((D_MODEL, MOD_ROWS), F32).at[:, :nvec].set(cvecs.T)
    ncol = w_mod.shape[2]
    out = pl.pallas_call(
        functools.partial(_mod_kernel, nvec=nvec),
        grid=(DEPTH, ncol // MOD_TN),
        in_specs=[
            _const_spec((D_MODEL, MOD_ROWS)),
            pl.BlockSpec((None, D_MODEL, MOD_TN), lambda l, j: (l, 0, j)),
            pl.BlockSpec((None, 1, MOD_TN), lambda l, j: (l, 0, j)),
        ],
        out_specs=pl.BlockSpec((None, MOD_ROWS, MOD_TN), lambda l, j: (l, 0, j)),
        out_shape=jax.ShapeDtypeStruct((DEPTH, MOD_ROWS, ncol), F32),
        compiler_params=_cp(("arbitrary", "arbitrary")),
        name="modulation",
    )(ct, w_mod, b_mod.reshape(DEPTH, 1, ncol))
    return out.reshape(DEPTH * MOD_ROWS, MOD_CHUNKS, D_MODEL)


def _mla_kv_heads(ckv_n, kr, wkv_ref, gk_ref, rope):
    kv = _dot(ckv_n.astype(BF16), wkv_ref[...])
    ss_r = jnp.sum(kr * kr, axis=-1, keepdims=True)
    ks = []
    for h in range(MLA_HEADS):
        kn = kv[:, h * MLA_NOPE:(h + 1) * MLA_NOPE]
        ms = (jnp.sum(kn * kn, axis=-1, keepdims=True) + ss_r) * (1.0 / MLA_QK)
        r = lax.rsqrt(ms + EPS)
        tail = kr * r * gk_ref[:, h * 256 + 128:(h + 1) * 256]
        if rope is not None:
            tail = _rope128(tail, *rope)
        ks.append(kn * r * gk_ref[:, h * 256:h * 256 + 128])
        ks.append(tail)
    return jnp.concatenate(ks, axis=-1), kv[:, MLA_HEADS * MLA_NOPE:]


def _proj_kernel(*refs, latent):
    (x_ref, mod_ref, gmix_ref, win_ref, wq_ref, wkv_ref, gqna_ref, gkna_ref, gql_ref, gkvl_ref,
     gqm_ref, gkm_ref, gqw_ref, gkw_ref) = refs[:14]
    pos = 14
    rope = None
    if latent:
        rope = (refs[14][...], refs[15][...])
        pos = 16
    (qna_o, kna_o, vna_o, qm_o, km_o, vm_o, qw_o, kw_o, vw_o, fv_o) = refs[pos:pos + 10]
    cache_o = refs[pos + 10:]

    x = x_ref[...]
    rinv = lax.rsqrt(jnp.mean(x * x, axis=-1, keepdims=True) + EPS)
    h = (x * rinv) * (gmix_ref[...] * (1.0 + mod_ref[0, 1:2, :])) + mod_ref[0, 0:1, :]
    hb = h.astype(BF16)

    def proj(c0, c1):
        return _dot(hb, win_ref[:, c0:c1])

    qna_o[...] = (_seg_rms64(proj(C_AQ, C_AK)) * gqna_ref[...]).astype(BF16)
    k_na = _seg_rms64(proj(C_AK, C_AV)) * gkna_ref[...]
    kna_o[...] = k_na.astype(BF16)
    v_na = proj(C_AV, C_MQ)
    vna_o[...] = v_na.astype(BF16)

    cq = (_rms(proj(C_MQ, C_CKV), MLA_Q_LORA) * gql_ref[...]).astype(BF16)
    qu = _dot(cq, wq_ref[...])
    qs = []
    for hd in range(MLA_HEADS):
        blk = qu[:, hd * 256:(hd + 1) * 256]
        blk = blk * lax.rsqrt(jnp.sum(blk * blk, axis=-1, keepdims=True) * (1.0 / MLA_QK) + EPS)
        blk = blk * gqm_ref[:, hd * 256:(hd + 1) * 256]
        if latent:
            blk = jnp.concatenate([blk[:, :LANES], _rope128(blk[:, LANES:], *rope)], axis=-1)
        qs.append(blk)
    qm_o[...] = jnp.concatenate(qs, axis=-1).astype(BF16)
    ckv_n = _rms(proj(C_CKV, C_KR), MLA_KV_LORA) * gkvl_ref[...]
    kr = proj(C_KR, C_CQ)
    k_m, v_m = _mla_kv_heads(ckv_n, kr, wkv_ref, gkm_ref, rope)
    km_o[...] = k_m.astype(BF16)
    vm_o[...] = v_m.astype(BF16)

    q_w = _seg_rms64(proj(C_CQ, C_CK)) * gqw_ref[...]
    k_w = _seg_rms64(proj(C_CK, C_CV)) * gkw_ref[...]
    if latent:
        q_w = _rope_wide(q_w, *rope)
        k_w = _rope_wide(k_w, *rope)
    qw_o[...] = q_w.astype(BF16)
    kw_o[...] = k_w.astype(BF16)
    v_w = proj(C_CV, C_FV)
    vw_o[...] = v_w.astype(BF16)

    fv_o[...] = proj(C_FV, IN_COLS_R).astype(BF16)

    if not latent:
        kna32_o, vna32_o, ckv32_o, kr32_o, kw32_o, vw32_o = cache_o
        lo = lax.broadcasted_iota(jnp.int32, (x.shape[0], LANES), 1) < HEAD_DIM
        kna32_o[...] = k_na
        vna32_o[...] = v_na
        ckv32_o[...] = ckv_n
        kr32_o[...] = kr[:, :MLA_ROPE]
        kw32_o[...] = jnp.where(lo, k_w[:, :LANES], k_w[:, LANES:])
        vw32_o[...] = jnp.where(lo, v_w[:, :LANES], v_w[:, LANES:])


def _project(x, mod, layer, p, latent, rope_tabs):
    n = x.shape[0]
    tm = TM_PROJ
    row = lambda i: (i, 0)
    if latent:
        mod_map = lambda i: (layer * MOD_ROWS + 1 + (i * tm) // DEC_SEQ, 0, 0)
    else:
        mod_map = lambda i: (layer * MOD_ROWS, 0, 0)
    in_specs = [
        pl.BlockSpec((tm, D_MODEL), row),
        pl.BlockSpec((1, MOD_CHUNKS, D_MODEL), mod_map),
        _const_spec((1, D_MODEL)),
        _const_spec((D_MODEL, IN_COLS_R)),
        _const_spec((MLA_Q_LORA, 1024)),
        _const_spec((MLA_KV_LORA, 1024)),
        _const_spec((1, 512)), _const_spec((1, 512)), _const_spec((1, MLA_Q_LORA)), _const_spec((1, MLA_KV_LORA)),
        _const_spec((1, 1024)), _const_spec((1, 1024)), _const_spec((1, 512)), _const_spec((1, 256)),
    ]
    args = [x, mod, p['g_mix'], p['w_in'], p['w_q_up'], p['w_kv_up'], p['gq_na'], p['gk_na'], p['g_q_lora'],
            p['g_kv_lora'], p['gq_mla'], p['gk_mla'], p['gq_win'], p['gk_win']]
    if latent:
        nt = DEC_SEQ // tm
        in_specs += [pl.BlockSpec((tm, LANES), lambda i: (i % nt, 0))] * 2
        args += list(rope_tabs)
    widths = [512, 512, 512, 1024, 1024, 512, 512, 256, 256, 512]
    out_specs = [pl.BlockSpec((tm, w), row) for w in widths]
    out_shape = [jax.ShapeDtypeStruct((n, w), BF16) for w in widths]
    if not latent:
        cache_w = [512, 512, MLA_KV_LORA, MLA_ROPE, 128, 128]
        out_specs += [pl.BlockSpec((tm, w), row) for w in cache_w]
        out_shape += [jax.ShapeDtypeStruct((n, w), F32) for w in cache_w]
    return pl.pallas_call(
        functools.partial(_proj_kernel, latent=latent),
        grid=(n // tm,),
        in_specs=in_specs,
        out_specs=out_specs,
        out_shape=out_shape,
        compiler_params=_cp(("arbitrary",)),
        name="proj_latent" if latent else "proj_context",
    )(*args)


def _mla_ctx_kernel(ckv_ref, kr_ref, wkv_ref, gk_ref, k_o, v_o):
    k_m, v_m = _mla_kv_heads(ckv_ref[...], kr_ref[...], wkv_ref, gk_ref, None)
    k_o[...] = k_m.astype(BF16)
    v_o[...] = v_m.astype(BF16)


def _mla_ctx(ckv, kr128, p):
    n = ckv.shape[0]
    tm = 512
    row = lambda i: (i, 0)
    return pl.pallas_call(
        _mla_ctx_kernel,
        grid=(n // tm,),
        in_specs=[pl.BlockSpec((tm, MLA_KV_LORA), row), pl.BlockSpec((tm, LANES), row),
                  _const_spec((MLA_KV_LORA, 1024)), _const_spec((1, 1024))],
        out_specs=[pl.BlockSpec((tm, 1024), row), pl.BlockSpec((tm, 512), row)],
        out_shape=[jax.ShapeDtypeStruct((n, 1024), BF16), jax.ShapeDtypeStruct((n, 512), BF16)],
        compiler_params=_cp(("arbitrary",)),
        name="mla_cached_kv",
    )(ckv, kr128, p['w_kv_up'], p['gk_mla'])


def _softmax_pv(scores, values, sink=None):
    m = functools.reduce(jnp.maximum, [jnp.max(s, axis=-1, keepdims=True) for s in scores])
    if sink is not None:
        m = jnp.maximum(m, sink)
    ps = [jnp.exp(s - m) for s in scores]
    l = functools.reduce(lambda a, b: a + b, [jnp.sum(p, axis=-1, keepdims=True) for p in ps])
    if sink is not None:
        l = l + jnp.exp(sink - m)
    o = functools.reduce(lambda a, b: a + b, [_dot(p.astype(BF16), v) for p, v in zip(ps, values)])
    return o / l


def _half_masks():
    lane = lax.broadcasted_iota(jnp.int32, (1, LANES), 1)
    lo = lane < HEAD_DIM
    return jnp.where(lo, 1.0, 0.0).astype(BF16), jnp.where(lo, 0.0, 1.0).astype(BF16)


def _pair_attention(q2, keys, values, extra=None, sinks=None):
    rows = q2.shape[0]
    lo = lax.broadcasted_iota(jnp.int32, (rows, LANES), 1) < HEAD_DIM
    outs = []
    for e, msk in enumerate(_half_masks()):
        qm = q2 * msk
        scores = []
        for idx, k2 in enumerate(keys):
            s = _dot_nt(qm, k2)
            if extra is not None:
                s = extra(e, idx, s)
            scores.append(s)
        sink = None
        if sinks is not None:
            sink = jnp.full((rows, 1), sinks[e], F32)
        outs.append(_softmax_pv(scores, values, sink))
    return jnp.where(lo, outs[0], outs[1])


def _dft_real(x, cc_ref, sc_ref, cs, ss, scale):
    outs = []
    for g in range(FN_GROUPS):
        xg = x[:, g * FN_CH:(g + 1) * FN_CH]
        xc = _dot(xg, cc_ref[...]).astype(BF16)
        xs = _dot(xg, sc_ref[...]).astype(BF16)
        outs.append(_dot(cs, xc) - _dot(ss, xs))
    return jnp.concatenate(outs, axis=-1) * scale


def _ctx_attn_kernel(sink_ref, qna, kna, vna, qm, km, vm, qw, kw, vw, fv, cs_ref, ss_ref, cc_ref, sc_ref,
                     oa, ob, oc, od):
    for hp in range(4):
        cs = slice(hp * LANES, (hp + 1) * LANES)
        oa[:, cs] = _pair_attention(qna[:, cs], [kna[:, cs]], [vna[:, cs]]).astype(BF16)
    for hd in range(MLA_HEADS):
        s = _dot_nt(qm[:, hd * 256:(hd + 1) * 256], km[:, hd * 256:(hd + 1) * 256])
        ob[:, hd * 128:(hd + 1) * 128] = _softmax_pv([s], [vm[:, hd * 128:(hd + 1) * 128]]).astype(BF16)
    for hp in range(4):
        cs = slice(hp * LANES, (hp + 1) * LANES)
        kcs = slice((hp // 2) * LANES, (hp // 2 + 1) * LANES)
        oc[:, cs] = _pair_attention(qw[:, cs], [kw[:, kcs]], [vw[:, kcs]],
                                    sinks=(sink_ref[2 * hp], sink_ref[2 * hp + 1])).astype(BF16)
    od[...] = _dft_real(fv[...], cc_ref, sc_ref, cs_ref[...], ss_ref[...],
                        1.0 / math.sqrt(SEQ * FN_CH)).astype(BF16)


def _ctx_attention(pr, sink, dft):
    qna, kna, vna, qm, km, vm, qw, kw, vw, fv = pr[:10]
    n = qna.shape[0]
    row = lambda b: (b, 0)
    widths = [512, 512, 512, 1024, 1024, 512, 512, 256, 256, 512]
    in_specs = [pl.BlockSpec(memory_space=pltpu.SMEM)]
    in_specs += [pl.BlockSpec((SEQ, w), row) for w in widths]
    in_specs += [_const_spec((SEQ, SEQ)), _const_spec((SEQ, SEQ)), _const_spec((FN_CH, FN_CH)),
                 _const_spec((FN_CH, FN_CH))]
    return pl.pallas_call(
        _ctx_attn_kernel,
        grid=(n // SEQ,),
        in_specs=in_specs,
        out_specs=[pl.BlockSpec((SEQ, 512), row)] * 4,
        out_shape=[jax.ShapeDtypeStruct((n, 512), BF16)] * 4,
        compiler_params=_cp(("arbitrary",)),
        name="context_attention",
    )(sink, qna, kna, vna, qm, km, vm, qw, kw, vw, fv, dft['cs_ctx'], dft['ss_ctx'], dft['cc'], dft['sc'])


def _na_kernel(q_ref, k_ref, v_ref, kc_ref, vc_ref, bias_ref, o_ref):
    qt = pl.program_id(1)
    ks = pl.multiple_of(jnp.clip(qt * 4 - 4, 0, DEC_SEQ // GRID_W - NA_KROWS) * GRID_W, GRID_W)
    nk = NA_KROWS * GRID_W
    for hp in range(4):
        cs = slice(hp * LANES, (hp + 1) * LANES)

        def extra(e, idx, s, hp=hp):
            return s + bias_ref[0, 2 * hp + e] if idx == 0 else s

        o = _pair_attention(q_ref[:, cs], [k_ref[pl.ds(ks, nk), cs], kc_ref[:, cs]],
                            [v_ref[pl.ds(ks, nk), cs], vc_ref[:, cs]], extra=extra)
        o_ref[:, cs] = o.astype(BF16)


def _lat_specs(width_q, width_k, width_v):
    nq = DEC_SEQ // TQ
    return [
        pl.BlockSpec((TQ, width_q), lambda b, t: (b * nq + t, 0)),
        pl.BlockSpec((DEC_SEQ, width_k), lambda b, t: (b, 0)),
        pl.BlockSpec((DEC_SEQ, width_v), lambda b, t: (b, 0)),
        pl.BlockSpec((PAST_LEN, width_k), lambda b, t: (b, 0)),
        pl.BlockSpec((PAST_LEN, width_v), lambda b, t: (b, 0)),
    ]


def _lat_call(kernel, name, args, in_specs, nb):
    nq = DEC_SEQ // TQ
    return pl.pallas_call(
        kernel,
        grid=(nb, nq),
        in_specs=in_specs,
        out_specs=pl.BlockSpec((TQ, 512), lambda b, t: (b * nq + t, 0)),
        out_shape=jax.ShapeDtypeStruct((nb * DEC_SEQ, 512), BF16),
        compiler_params=_cp(("arbitrary", "arbitrary")),
        name=name,
    )(*args)


def _na_attention(q, k, v, kc, vc, bias, nb):
    nk = NA_KROWS * GRID_W
    specs = _lat_specs(512, 512, 512)
    specs.append(pl.BlockSpec((1, 8, TQ, nk),
                              lambda b, t: (jnp.minimum(t, 2) + jnp.maximum(t - 5, 0), 0, 0, 0)))
    return _lat_call(_na_kernel, "neighbourhood_attention", (q, k, v, kc, vc, bias), specs, nb)


def _mla_kernel(q_ref, k_ref, v_ref, kc_ref, vc_ref, o_ref):
    for hd in range(MLA_HEADS):
        ks = slice(hd * 256, (hd + 1) * 256)
        vs = slice(hd * 128, (hd + 1) * 128)
        q = q_ref[:, ks]
        scores = [_dot_nt(q, k_ref[:, ks]), _dot_nt(q, kc_ref[:, ks])]
        o_ref[:, vs] = _softmax_pv(scores, [v_ref[:, vs], vc_ref[:, vs]]).astype(BF16)


def _mla_attention(q, k, v, kc, vc, nb):
    return _lat_call(_mla_kernel, "latent_attention", (q, k, v, kc, vc), _lat_specs(1024, 1024, 512), nb)


def _win_kernel(sink_ref, q_ref, k_ref, v_ref, kc_ref, vc_ref, o_ref):
    qt = pl.program_id(1)
    nk = 4 * WINDOW
    ks = pl.multiple_of(jnp.clip(qt * TQ - WINDOW, 0, DEC_SEQ - nk), WINDOW)
    qpos = qt * TQ + lax.broadcasted_iota(jnp.int32, (TQ, nk), 0)
    kpos = ks + lax.broadcasted_iota(jnp.int32, (TQ, nk), 1)
    ok = jnp.abs(kpos - qpos) <= WINDOW

    def extra(e, idx, s):
        return jnp.where(ok, s, NEG) if idx == 0 else s

    for hp in range(4):
        cs = slice(hp * LANES, (hp + 1) * LANES)
        kcs = slice((hp // 2) * LANES, (hp // 2 + 1) * LANES)
        o = _pair_attention(q_ref[:, cs], [k_ref[pl.ds(ks, nk), kcs], kc_ref[:, kcs]],
                            [v_ref[pl.ds(ks, nk), kcs], vc_ref[:, kcs]], extra=extra,
                            sinks=(sink_ref[2 * hp], sink_ref[2 * hp + 1]))
        o_ref[:, cs] = o.astype(BF16)


def _win_attention(sink, q, k, v, kc, vc, nb):
    specs = [pl.BlockSpec(memory_space=pltpu.SMEM)] + _lat_specs(512, 256, 256)
    return _lat_call(_win_kernel, "window_attention", (sink, q, k, v, kc, vc), specs, nb)


def _fourier_kernel(x_ref, cc_ref, sc_ref, cs_ref, ss_ref, o_ref, xc_ref, xs_ref):
    @pl.when(pl.program_id(1) == 0)
    def _():
        for g in range(FN_GROUPS):
            gs = slice(g * FN_CH, (g + 1) * FN_CH)
            xg = x_ref[:, gs]
            xc_ref[:, gs] = _dot(xg, cc_ref[...]).astype(BF16)
            xs_ref[:, gs] = _dot(xg, sc_ref[...]).astype(BF16)

    y = _dot(cs_ref[...], xc_ref[...]) - _dot(ss_ref[...], xs_ref[...])
    o_ref[...] = (y * (1.0 / math.sqrt(DEC_SEQ * FN_CH))).astype(BF16)


def _fourier(x, dft, nb):
    nq = DEC_SEQ // TQ
    return pl.pallas_call(
        _fourier_kernel,
        grid=(nb, nq),
        in_specs=[
            pl.BlockSpec((DEC_SEQ, 512), lambda b, t: (b, 0)),
            _const_spec((FN_CH, FN_CH)), _const_spec((FN_CH, FN_CH)),
            pl.BlockSpec((TQ, DEC_SEQ), lambda b, t: (t, 0)),
            pl.BlockSpec((TQ, DEC_SEQ), lambda b, t: (t, 0)),
        ],
        out_specs=pl.BlockSpec((TQ, 512), lambda b, t: (b * nq + t, 0)),
        out_shape=jax.ShapeDtypeStruct((nb * DEC_SEQ, 512), BF16),
        scratch_shapes=[pltpu.VMEM((DEC_SEQ, 512), BF16), pltpu.VMEM((DEC_SEQ, 512), BF16)],
        compiler_params=_cp(("arbitrary", "arbitrary")),
        name="fourier_mix",
    )(x, dft['cc'], dft['sc'], dft['cs_lat'], dft['ss_lat'])


def _out_kernel(oa, ob, oc, od, x_ref, mod_ref, gffn_ref, w_ref, x1_o, h2_o):
    acc = _dot(oa[...], w_ref[0:512, :])
    acc += _dot(ob[...], w_ref[512:1024, :])
    acc += _dot(oc[...], w_ref[1024:1536, :])
    acc += _dot(od[...], w_ref[1536:2048, :])
    x1 = x_ref[...] + mod_ref[0, 2:3, :] * acc
    x1_o[...] = x1
    rinv = lax.rsqrt(jnp.mean(x1 * x1, axis=-1, keepdims=True) + EPS)
    h2 = (x1 * rinv) * (gffn_ref[...] * (1.0 + mod_ref[0, 4:5, :])) + mod_ref[0, 3:4, :]
    h2_o[...] = h2.astype(BF16)


def _mod_map(layer, tm, latent):
    if latent:
        return lambda i, *_: (layer * MOD_ROWS + 1 + (i * tm) // DEC_SEQ, 0, 0)
    return lambda i, *_: (layer * MOD_ROWS, 0, 0)


def _out_project(o4, x, mod, layer, p, latent):
    n = x.shape[0]
    tm = TM_OUT
    row = lambda i: (i, 0)
    return pl.pallas_call(
        _out_kernel,
        grid=(n // tm,),
        in_specs=[pl.BlockSpec((tm, 512), row)] * 4 + [
            pl.BlockSpec((tm, D_MODEL), row),
            pl.BlockSpec((1, MOD_CHUNKS, D_MODEL), _mod_map(layer, tm, latent)),
            _const_spec((1, D_MODEL)),
            _const_spec((D_MODEL, D_MODEL)),
        ],
        out_specs=[pl.BlockSpec((tm, D_MODEL), row), pl.BlockSpec((tm, D_MODEL), row)],
        out_shape=[jax.ShapeDtypeStruct((n, D_MODEL), F32), jax.ShapeDtypeStruct((n, D_MODEL), BF16)],
        compiler_params=_cp(("arbitrary",)),
        name="out_proj_latent" if latent else "out_proj_context",
    )(*o4, x, mod, p['g_ffn'], p['w_out'])


def _ffn_kernel(h_ref, hp_ref, hn_ref, x1_ref, mod_ref, wg_ref, wu_ref, wc_ref, wd_ref, o_ref, hext_ref, acc_ref,
                *, seq_len):
    i = pl.program_id(0)
    j = pl.program_id(1)
    tm = h_ref.shape[0]

    @pl.when(j == 0)
    def _():
        hext_ref[0:HALO, :] = hp_ref[...]
        hext_ref[HALO:HALO + tm, :] = h_ref[...]
        hext_ref[HALO + tm:, :] = hn_ref[...]
        acc_ref[...] = jnp.zeros_like(acc_ref)

    g_ext = _dot(hext_ref[...], wg_ref[...])
    rows_ext = g_ext.shape[0]
    pos = (i * tm + lax.broadcasted_iota(jnp.int32, (tm, 1), 0)) % seq_len
    g_mid = g_ext[HALO:HALO + tm]
    g_prev = pltpu.roll(g_ext, 1, 0)[HALO:HALO + tm]
    g_next = pltpu.roll(g_ext, rows_ext - 1, 0)[HALO:HALO + tm]
    g_prev = jnp.where(pos == 0, 0.0, g_prev)
    g_next = jnp.where(pos == seq_len - 1, 0.0, g_next)
    g = g_prev * wc_ref[0:1, :] + g_mid * wc_ref[1:2, :] + g_next * wc_ref[2:3, :]
    u = _dot(h_ref[...], wu_ref[...])
    act = (g / (1.0 + jnp.exp(-g))) * u
    acc_ref[...] += _dot(act.astype(BF16), wd_ref[...])

    @pl.when(j == pl.num_programs(1) - 1)
    def _():
        o_ref[...] = x1_ref[...] + mod_ref[0, 5:6, :] * acc_ref[...]


def _ffn(h2, x1, mod, layer, p, latent):
    n = x1.shape[0]
    tm, tf = TM_FFN, TF_FFN
    seq_len = DEC_SEQ if latent else SEQ
    assert n % tm == 0 and tm % HALO == 0 and D_FF % tf == 0
    hb = tm // HALO
    nhalo = n // HALO
    return pl.pallas_call(
        functools.partial(_ffn_kernel, seq_len=seq_len),
        grid=(n // tm, D_FF // tf),
        in_specs=[
            pl.BlockSpec((tm, D_MODEL), lambda i, j: (i, 0)),
            pl.BlockSpec((HALO, D_MODEL), lambda i, j: (jnp.maximum(i * hb - 1, 0), 0)),
            pl.BlockSpec((HALO, D_MODEL), lambda i, j: (jnp.minimum((i + 1) * hb, nhalo - 1), 0)),
            pl.BlockSpec((tm, D_MODEL), lambda i, j: (i, 0)),
            pl.BlockSpec((1, MOD_CHUNKS, D_MODEL), _mod_map(layer, tm, latent)),
            pl.BlockSpec((D_MODEL, tf), lambda i, j: (0, j)),
            pl.BlockSpec((D_MODEL, tf), lambda i, j: (0, j)),
            pl.BlockSpec((8, tf), lambda i, j: (0, j)),
            pl.BlockSpec((tf, D_MODEL), lambda i, j: (j, 0)),
        ],
        out_specs=pl.BlockSpec((tm, D_MODEL), lambda i, j: (i, 0)),
        out_shape=jax.ShapeDtypeStruct((n, D_MODEL), F32),
        scratch_shapes=[pltpu.VMEM((tm + 2 * HALO, D_MODEL), BF16), pltpu.VMEM((tm, D_MODEL), F32)],
        compiler_params=_cp(("arbitrary", "arbitrary")),
        name="ffn_latent" if latent else "ffn_context",
    )(h2, h2, h2, x1, mod, p['w_gate'], p['w_up'], p['w_conv'], p['w_down'])


def _tile_row(g, reps, scale=1.0):
    return (jnp.tile(g, reps) * scale).reshape(1, -1)


def _prep_layer(w, l):
    win = w['w_in'][l]
    zeros64 = jnp.zeros((D_MODEL, 64), F32)
    ck = win[:, 2624:2752]
    cv = win[:, 2752:2880]
    dup = lambda t: jnp.concatenate([t[:, :64], t[:, :64], t[:, 64:], t[:, 64:]], axis=1)
    win_r = jnp.concatenate([
        win[:, 0:1536], win[:, 1536:1920], win[:, 1920:2048], win[:, 2048:2112], zeros64,
        win[:, 2112:2624], dup(ck), dup(cv), win[:, 2880:3392]], axis=1).astype(BF16)

    wq = w['w_q_up'][l].reshape(MLA_Q_LORA, MLA_HEADS, MLA_QK)
    wq_r = jnp.concatenate([wq, jnp.zeros((MLA_Q_LORA, MLA_HEADS, 256 - MLA_QK), F32)], axis=2)
    wq_r = wq_r.reshape(MLA_Q_LORA, MLA_HEADS * 256).astype(BF16)
    wkv = w['w_kv_up'][l].reshape(MLA_KV_LORA, MLA_HEADS, 2 * MLA_NOPE)
    wkv_r = jnp.concatenate([wkv[:, :, :MLA_NOPE].reshape(MLA_KV_LORA, -1),
                             wkv[:, :, MLA_NOPE:].reshape(MLA_KV_LORA, -1)], axis=1).astype(BF16)

    pad_head = lambda g, s: jnp.tile(jnp.concatenate([g, jnp.zeros((256 - MLA_QK,), F32)]) * s, MLA_HEADS)
    wc = jnp.concatenate([w['w_conv'][l], jnp.zeros((5, D_FF), F32)], axis=0)
    return dict(
        g_mix=w['g_mix'][l].reshape(1, -1), g_ffn=w['g_ffn'][l].reshape(1, -1),
        w_in=win_r, w_q_up=wq_r, w_kv_up=wkv_r,
        gq_na=_tile_row(w['g_qn_na'][l], 8, HEAD_DIM ** -0.5), gk_na=_tile_row(w['g_kn_na'][l], 8),
        g_q_lora=w['g_q_lora'][l].reshape(1, -1), g_kv_lora=w['g_kv_lora'][l].reshape(1, -1),
        gq_mla=pad_head(w['g_qn_mla'][l], MLA_QK ** -0.5).reshape(1, -1),
        gk_mla=pad_head(w['g_kn_mla'][l], 1.0).reshape(1, -1),
        gq_win=_tile_row(w['g_qn_win'][l], 8, HEAD_DIM ** -0.5), gk_win=_tile_row(w['g_kn_win'][l], 4),
        sink=w['sink_win'][l],
        w_out=w['w_out'][l].astype(BF16), w_gate=w['w_gate'][l].astype(BF16), w_up=w['w_up'][l].astype(BF16),
        w_conv=wc, w_down=w['w_down'][l].astype(BF16),
    )


def _na_bias_tables(rpb):
    rows = DEC_SEQ // GRID_W
    tabs = []
    for r0 in (0, 4, 8, 24, 28):
        kstart = int(np.clip(r0 - 4, 0, rows - NA_KROWS))
        qi = np.arange(TQ)
        kj = np.arange(NA_KROWS * GRID_W)
        r, c = r0 + qi // GRID_W, qi % GRID_W
        kr, kc = kstart + kj // GRID_W, kj % GRID_W
        rs = np.clip(r - 4, 0, rows - 8)
        row_ok = (kr[None, :] >= rs[:, None]) & (kr[None, :] < rs[:, None] + 8)
        cs = np.clip(c - 8, 0, GRID_W - 16)
        col_ok = (kc[None, :] >= cs[:, None]) & (kc[None, :] < cs[:, None] + 16)
        dr = np.clip(kr[None, :] - r[:, None] + 7, 0, 14)
        dc = np.clip(kc[None, :] - c[:, None], -15, 15) + 15
        b = rpb[:, dr, dc]
        tabs.append(jnp.where(jnp.asarray(row_ok & col_ok)[None], b, NEG))
    return jnp.stack(tabs, axis=0)


@functools.lru_cache(maxsize=None)
def _dft_tables():
    def cs(n):
        k = (np.arange(n)[:, None] * np.arange(n)[None, :]) % n
        ang = 2.0 * np.pi * k.astype(np.float64) / n
        return np.cos(ang), np.sin(ang)

    out = {}
    for name, n in (('ctx', SEQ), ('lat', DEC_SEQ)):
        c, s = cs(n)
        out['cs_' + name], out['ss_' + name] = c, s
    out['cc'], out['sc'] = cs(FN_CH)
    return out


@functools.lru_cache(maxsize=None)
def _rope_tables():
    t = np.arange(DEC_SEQ)
    quarter = MLA_ROPE // 4
    inv = ROPE_BASE ** (-np.arange(quarter, dtype=np.float64) / quarter)
    j = np.arange(MLA_ROPE)
    pos = np.where(j[None, :] < MLA_ROPE // 2, (t // GRID_W)[:, None], (t % GRID_W)[:, None]).astype(np.float64)
    ang = pos * inv[j % quarter][None, :]
    sign = np.where((j % 32) < 16, -1.0, 1.0)
    cos = np.cos(ang)
    sin = np.sin(ang) * sign[None, :]
    return np.tile(cos, (1, 2)).astype(np.float32), np.tile(sin, (1, 2)).astype(np.float32)


def kernel(x_prompt, x_sample, cache_na_k, cache_na_v, cache_mla_ckv, cache_mla_krope, cache_win_k, cache_win_v,
           c, c_ctx, w_mod, b_mod, g_mix, g_ffn, w_in, g_qn_na, g_kn_na, rpb_na, g_q_lora, w_q_up, g_kv_lora,
           w_kv_up, g_qn_mla, g_kn_mla, g_qn_win, g_kn_win, sink_win, w_out, w_gate, w_up, w_conv, w_down):
    w = dict(g_mix=g_mix, g_ffn=g_ffn, w_in=w_in, g_qn_na=g_qn_na, g_kn_na=g_kn_na, g_q_lora=g_q_lora,
             w_q_up=w_q_up, g_kv_lora=g_kv_lora, w_kv_up=w_kv_up, g_qn_mla=g_qn_mla, g_kn_mla=g_kn_mla,
             g_qn_win=g_qn_win, g_kn_win=g_kn_win, sink_win=sink_win, w_out=w_out, w_gate=w_gate, w_up=w_up,
             w_conv=w_conv, w_down=w_down)
    nb_ctx, nb_lat = x_prompt.shape[0], x_sample.shape[0]
    dft = {k: jnp.asarray(v, F32).astype(BF16) for k, v in _dft_tables().items()}
    rope_tabs = tuple(jnp.asarray(t) for t in _rope_tables())

    mod = _modulation(jnp.concatenate([c_ctx[None, :], c], axis=0), w_mod, b_mod)

    xp = x_prompt.reshape(nb_ctx * SEQ, D_MODEL)
    xs = x_sample.reshape(nb_lat * DEC_SEQ, D_MODEL)
    caches = []
    for l in range(DEPTH):
        p = _prep_layer(w, l)

        pr = _project(xp, mod, l, p, False, None)
        caches.append(pr[10:])
        o4 = _ctx_attention(pr, p['sink'], dft)
        x1, h2 = _out_project(o4, xp, mod, l, p, False)
        xp = _ffn(h2, x1, mod, l, p, False)

        qna, kna, vna, qm, km, vm, qw, kw, vw, fv = _project(xs, mod, l, p, True, rope_tabs)
        nctx = nb_lat * PAST_LEN
        kna_c = cache_na_k[:, l].reshape(nctx, 512).astype(BF16)
        vna_c = cache_na_v[:, l].reshape(nctx, 512).astype(BF16)
        kr_c = jnp.pad(cache_mla_krope[:, l].reshape(nctx, MLA_ROPE), ((0, 0), (0, LANES - MLA_ROPE)))
        km_c, vm_c = _mla_ctx(cache_mla_ckv[:, l].reshape(nctx, MLA_KV_LORA), kr_c, p)
        dup = lambda t: jnp.concatenate([t[:, :64], t[:, :64], t[:, 64:], t[:, 64:]], axis=1).astype(BF16)
        kw_c = dup(cache_win_k[:, l].reshape(nctx, 128))
        vw_c = dup(cache_win_v[:, l].reshape(nctx, 128))
        o_a = _na_attention(qna, kna, vna, kna_c, vna_c, _na_bias_tables(rpb_na[l]), nb_lat)
        o_b = _mla_attention(qm, km, vm, km_c, vm_c, nb_lat)
        o_c = _win_attention(p['sink'], qw, kw, vw, kw_c, vw_c, nb_lat)
        o_d = _fourier(fv, dft, nb_lat)
        x1, h2 = _out_project((o_a, o_b, o_c, o_d), xs, mod, l, p, True)
        xs = _ffn(h2, x1, mod, l, p, True)

    def stack(idx, shape):
        return jnp.stack([caches[l][idx].reshape((nb_ctx, SEQ) + shape) for l in range(DEPTH)], axis=1)

    return (xp.reshape(nb_ctx, SEQ, D_MODEL), xs.reshape(nb_lat, DEC_SEQ, D_MODEL),
            stack(0, (8, HEAD_DIM)), stack(1, (8, HEAD_DIM)), stack(2, (MLA_KV_LORA,)), stack(3, (MLA_ROPE,)),
            stack(4, (2, HEAD_DIM)), stack(5, (2, HEAD_DIM)))
```

```python
import functools
import math

import numpy as np
import jax
import jax.numpy as jnp
from jax import lax
from jax.experimental import pallas as pl
from jax.experimental.pallas import tpu as pltpu

F32 = jnp.float32
BF16 = jnp.bfloat16

D_MODEL = 2048
DEPTH = 2
SEQ = 256
DEC_SEQ = 2048
PAST_LEN = 512
GRID_W = 64
HEAD_DIM = 64
GROUP_W = 512
MLA_HEADS = 4
MLA_NOPE = 128
MLA_ROPE = 64
MLA_QK = MLA_NOPE + MLA_ROPE
MLA_Q_LORA = 384
MLA_KV_LORA = 128
WINDOW = 128
FN_GROUPS = 4
FN_CH = 128
D_FF = 5632
MOD_CHUNKS = 6
ROPE_BASE = 10000.0
EPS = 1e-6
NEG = -1e30

LANES = 128
MOD_ROWS = 8
VMEM_LIMIT = 56 * 1024 * 1024

C_AQ, C_AK, C_AV = 0, 512, 1024
C_MQ = 1536
C_CKV = 1920
C_KR = 2048
C_CQ = 2176
C_CK = 2688
C_CV = 2944
C_FV = 3200
IN_COLS_R = 3712

TM_PROJ = 256
TM_OUT = 256
TM_FFN = 512
TF_FFN = 512
HALO = 16
TQ = 256
NA_KROWS = 12
NA_TILES = 5


def _cp(sem):
    return pltpu.CompilerParams(dimension_semantics=sem, vmem_limit_bytes=VMEM_LIMIT)


def _const_spec(shape):
    n = len(shape)
    return pl.BlockSpec(shape, lambda *_: (0,) * n)


def _dot(a, b):
    return jnp.dot(a, b, preferred_element_type=F32)


def _dot_nt(a, b):
    return lax.dot_general(a, b, (((1,), (1,)), ((), ())), preferred_element_type=F32)


def _rms(t, n):
    return t * lax.rsqrt(jnp.sum(t * t, axis=-1, keepdims=True) * (1.0 / n) + EPS)


def _seg_rms64(t):
    rows, width = t.shape
    lo = lax.broadcasted_iota(jnp.int32, (rows, LANES), 1) < HEAD_DIM
    outs = []
    for c in range(width // LANES):
        tc = t[:, c * LANES:(c + 1) * LANES]
        sq = tc * tc
        s_lo = jnp.sum(jnp.where(lo, sq, 0.0), axis=-1, keepdims=True)
        s_hi = jnp.sum(jnp.where(lo, 0.0, sq), axis=-1, keepdims=True)
        ms = jnp.where(lo, s_lo, s_hi) * (1.0 / HEAD_DIM)
        outs.append(tc * lax.rsqrt(ms + EPS))
    return jnp.concatenate(outs, axis=-1)


def _rope128(t, cos, sin_signed):
    first = (lax.broadcasted_iota(jnp.int32, t.shape, 1) % 32) < 16
    partner = jnp.where(first, pltpu.roll(t, LANES - 16, 1), pltpu.roll(t, 16, 1))
    return t * cos + partner * sin_signed


def _rope_wide(t, cos, sin_signed):
    return jnp.concatenate(
        [_rope128(t[:, c * LANES:(c + 1) * LANES], cos, sin_signed) for c in range(t.shape[1] // LANES)], axis=-1)


MOD_TN = 512


def _mod_kernel(ct_ref, w_ref, b_ref, o_ref, xb_ref, *, nvec):
    tn = w_ref.shape[1]
    nct = tn // LANES

    @pl.when((pl.program_id(0) == 0) & (pl.program_id(1) == 0))
    def _():
        cv = ct_ref[...]
        sx = cv / (1.0 + jnp.exp(-cv))
        for v in range(nvec):
            xb_ref[v] = jnp.broadcast_to(sx[:, v:v + 1], (D_MODEL, LANES))

    def body(kc, accs):
        k0 = pl.multiple_of(kc * 8, 8)
        ws = [w_ref[pl.ds(k0, 8), t * LANES:(t + 1) * LANES] for t in range(nct)]
        new = []
        for v in range(nvec):
            xv = xb_ref[v, pl.ds(k0, 8), :]
            new.extend(accs[v * nct + t] + xv * ws[t] for t in range(nct))
        return tuple(new)

    zero = jnp.zeros((8, LANES), F32)
    accs = lax.fori_loop(0, w_ref.shape[0] // 8, body, (zero,) * (nvec * nct), unroll=2)
    rows = [jnp.concatenate([jnp.sum(accs[v * nct + t], axis=0, keepdims=True) for t in range(nct)], axis=1)
            for v in range(nvec)]
    rows.append(jnp.zeros((MOD_ROWS - nvec, tn), F32))
    o_ref[...] = jnp.concatenate(rows, axis=0) + b_ref[...]


def _modulation(cvecs, w_mod, b_mod):
    nvec = cvecs.shape[0]
    ct = jnp.zeros((D_MODEL, MOD_ROWS), F32).at[:, :nvec].set(cvecs.T)
    ncol = w_mod.shape[2]
    out = pl.pallas_call(
        functools.partial(_mod_kernel, nvec=nvec),
        grid=(DEPTH, ncol // MOD_TN),
        in_specs=[
            _const_spec((D_MODEL, MOD_ROWS)),
            pl.BlockSpec((None, D_MODEL, MOD_TN), lambda l, j: (l, 0, j)),
            pl.BlockSpec((None, 1, MOD_TN), lambda l, j: (l, 0, j)),
        ],
        out_specs=pl.BlockSpec((None, MOD_ROWS, MOD_TN), lambda l, j: (l, 0, j)),
        out_shape=jax.ShapeDtypeStruct((DEPTH, MOD_ROWS, ncol), F32),
        scratch_shapes=[pltpu.VMEM((nvec, D_MODEL, LANES), F32)],
        compiler_params=_cp(("arbitrary", "arbitrary")),
        name="modulation",
    )(ct, w_mod, b_mod.reshape(DEPTH, 1, ncol))
    return out.reshape(DEPTH * MOD_ROWS, MOD_CHUNKS, D_MODEL)


def _mla_kv_heads(ckv_n, kr, wkv_ref, gk_ref, rope):
    kv = _dot(ckv_n.astype(BF16), wkv_ref[...])
    ss_r = jnp.sum(kr * kr, axis=-1, keepdims=True)
    ks = []
    for h in range(MLA_HEADS):
        kn = kv[:, h * MLA_NOPE:(h + 1) * MLA_NOPE]
        ms = (jnp.sum(kn * kn, axis=-1, keepdims=True) + ss_r) * (1.0 / MLA_QK)
        r = lax.rsqrt(ms + EPS)
        tail = kr * r * gk_ref[:, h * 256 + 128:(h + 1) * 256]
        if rope is not None:
            tail = _rope128(tail, *rope)
        ks.append(kn * r * gk_ref[:, h * 256:h * 256 + 128])
        ks.append(tail)
    return jnp.concatenate(ks, axis=-1), kv[:, MLA_HEADS * MLA_NOPE:]


def _proj_kernel(*refs, latent):
    (x_ref, mod_ref, gmix_ref, win_ref, wq_ref, wkv_ref, gqna_ref, gkna_ref, gql_ref, gkvl_ref,
     gqm_ref, gkm_ref, gqw_ref, gkw_ref) = refs[:14]
    pos = 14
    rope = None
    if latent:
        rope = (refs[14][...], refs[15][...])
        pos = 16
    (qna_o, kna_o, vna_o, qm_o, km_o, vm_o, qw_o, kw_o, vw_o, fv_o) = refs[pos:pos + 10]
    cache_o = refs[pos + 10:]

    x = x_ref[...]
    rinv = lax.rsqrt(jnp.mean(x * x, axis=-1, keepdims=True) + EPS)
    h = (x * rinv) * (gmix_ref[...] * (1.0 + mod_ref[0, 1:2, :])) + mod_ref[0, 0:1, :]
    hb = h.astype(BF16)

    def proj(c0, c1):
        return _dot(hb, win_ref[:, c0:c1])

    qna_o[...] = (_seg_rms64(proj(C_AQ, C_AK)) * gqna_ref[...]).astype(BF16)
    k_na = _seg_rms64(proj(C_AK, C_AV)) * gkna_ref[...]
    kna_o[...] = k_na.astype(BF16)
    v_na = proj(C_AV, C_MQ)
    vna_o[...] = v_na.astype(BF16)

    cq = (_rms(proj(C_MQ, C_CKV), MLA_Q_LORA) * gql_ref[...]).astype(BF16)
    qu = _dot(cq, wq_ref[...])
    qs = []
    for hd in range(MLA_HEADS):
        blk = qu[:, hd * 256:(hd + 1) * 256]
        blk = blk * lax.rsqrt(jnp.sum(blk * blk, axis=-1, keepdims=True) * (1.0 / MLA_QK) + EPS)
        blk = blk * gqm_ref[:, hd * 256:(hd + 1) * 256]
        if latent:
            blk = jnp.concatenate([blk[:, :LANES], _rope128(blk[:, LANES:], *rope)], axis=-1)
        qs.append(blk)
    qm_o[...] = jnp.concatenate(qs, axis=-1).astype(BF16)
    ckv_n = _rms(proj(C_CKV, C_KR), MLA_KV_LORA) * gkvl_ref[...]
    kr = proj(C_KR, C_CQ)
    k_m, v_m = _mla_kv_heads(ckv_n, kr, wkv_ref, gkm_ref, rope)
    km_o[...] = k_m.astype(BF16)
    vm_o[...] = v_m.astype(BF16)

    q_w = _seg_rms64(proj(C_CQ, C_CK)) * gqw_ref[...]
    k_w = _seg_rms64(proj(C_CK, C_CV)) * gkw_ref[...]
    if latent:
        q_w = _rope_wide(q_w, *rope)
        k_w = _rope_wide(k_w, *rope)
    qw_o[...] = q_w.astype(BF16)
    kw_o[...] = k_w.astype(BF16)
    v_w = proj(C_CV, C_FV)
    vw_o[...] = v_w.astype(BF16)

    fv_o[...] = proj(C_FV, IN_COLS_R).astype(BF16)

    if not latent:
        kna32_o, vna32_o, ckv32_o, kr32_o, kw32_o, vw32_o = cache_o
        lo = lax.broadcasted_iota(jnp.int32, (x.shape[0], LANES), 1) < HEAD_DIM
        kna32_o[...] = k_na
        vna32_o[...] = v_na
        ckv32_o[...] = ckv_n
        kr32_o[...] = kr[:, :MLA_ROPE]
        kw32_o[...] = jnp.where(lo, k_w[:, :LANES], k_w[:, LANES:])
        vw32_o[...] = jnp.where(lo, v_w[:, :LANES], v_w[:, LANES:])


def _project(x, mod, layer, p, latent, rope_tabs):
    n = x.shape[0]
    tm = TM_PROJ
    row = lambda i: (i, 0)
    if latent:
        mod_map = lambda i: (layer * MOD_ROWS + 1 + (i * tm) // DEC_SEQ, 0, 0)
    else:
        mod_map = lambda i: (layer * MOD_ROWS, 0, 0)
    in_specs = [
        pl.BlockSpec((tm, D_MODEL), row),
        pl.BlockSpec((1, MOD_CHUNKS, D_MODEL), mod_map),
        _const_spec((1, D_MODEL)),
        _const_spec((D_MODEL, IN_COLS_R)),
        _const_spec((MLA_Q_LORA, 1024)),
        _const_spec((MLA_KV_LORA, 1024)),
        _const_spec((1, 512)), _const_spec((1, 512)), _const_spec((1, MLA_Q_LORA)), _const_spec((1, MLA_KV_LORA)),
        _const_spec((1, 1024)), _const_spec((1, 1024)), _const_spec((1, 512)), _const_spec((1, 256)),
    ]
    args = [x, mod, p['g_mix'], p['w_in'], p['w_q_up'], p['w_kv_up'], p['gq_na'], p['gk_na'], p['g_q_lora'],
            p['g_kv_lora'], p['gq_mla'], p['gk_mla'], p['gq_win'], p['gk_win']]
    if latent:
        nt = DEC_SEQ // tm
        in_specs += [pl.BlockSpec((tm, LANES), lambda i: (i % nt, 0))] * 2
        args += list(rope_tabs)
    widths = [512, 512, 512, 1024, 1024, 512, 512, 256, 256, 512]
    out_specs = [pl.BlockSpec((tm, w), row) for w in widths]
    out_shape = [jax.ShapeDtypeStruct((n, w), BF16) for w in widths]
    if not latent:
        cache_w = [512, 512, MLA_KV_LORA, MLA_ROPE, 128, 128]
        out_specs += [pl.BlockSpec((tm, w), row) for w in cache_w]
        out_shape += [jax.ShapeDtypeStruct((n, w), F32) for w in cache_w]
    return pl.pallas_call(
        functools.partial(_proj_kernel, latent=latent),
        grid=(n // tm,),
        in_specs=in_specs,
        out_specs=out_specs,
        out_shape=out_shape,
        compiler_params=_cp(("arbitrary",)),
        name="proj_latent" if latent else "proj_context",
    )(*args)


def _mla_ctx_kernel(ckv_ref, kr_ref, wkv_ref, gk_ref, k_o, v_o):
    k_m, v_m = _mla_kv_heads(ckv_ref[...], kr_ref[...], wkv_ref, gk_ref, None)
    k_o[...] = k_m.astype(BF16)
    v_o[...] = v_m.astype(BF16)


def _mla_ctx(ckv, kr128, p):
    n = ckv.shape[0]
    tm = 512
    row = lambda i: (i, 0)
    return pl.pallas_call(
        _mla_ctx_kernel,
        grid=(n // tm,),
        in_specs=[pl.BlockSpec((tm, MLA_KV_LORA), row), pl.BlockSpec((tm, LANES), row),
                  _const_spec((MLA_KV_LORA, 1024)), _const_spec((1, 1024))],
        out_specs=[pl.BlockSpec((tm, 1024), row), pl.BlockSpec((tm, 512), row)],
        out_shape=[jax.ShapeDtypeStruct((n, 1024), BF16), jax.ShapeDtypeStruct((n, 512), BF16)],
        compiler_params=_cp(("arbitrary",)),
        name="mla_cached_kv",
    )(ckv, kr128, p['w_kv_up'], p['gk_mla'])


def _softmax_pv(scores, values, sink=None):
    m = functools.reduce(jnp.maximum, [jnp.max(s, axis=-1, keepdims=True) for s in scores])
    if sink is not None:
        m = jnp.maximum(m, sink)
    ps = [jnp.exp(s - m) for s in scores]
    l = functools.reduce(lambda a, b: a + b, [jnp.sum(p, axis=-1, keepdims=True) for p in ps])
    if sink is not None:
        l = l + jnp.exp(sink - m)
    o = functools.reduce(lambda a, b: a + b, [_dot(p.astype(BF16), v) for p, v in zip(ps, values)])
    return o / l


def _half_masks():
    lane = lax.broadcasted_iota(jnp.int32, (1, LANES), 1)
    lo = lane < HEAD_DIM
    return jnp.where(lo, 1.0, 0.0).astype(BF16), jnp.where(lo, 0.0, 1.0).astype(BF16)


def _pair_attention(q2, keys, values, extra=None, sinks=None):
    rows = q2.shape[0]
    lo = lax.broadcasted_iota(jnp.int32, (rows, LANES), 1) < HEAD_DIM
    outs = []
    for e, msk in enumerate(_half_masks()):
        qm = q2 * msk
        scores = []
        for idx, k2 in enumerate(keys):
            s = _dot_nt(qm, k2)
            if extra is not None:
                s = extra(e, idx, s)
            scores.append(s)
        sink = None
        if sinks is not None:
            sink = jnp.full((rows, 1), sinks[e], F32)
        outs.append(_softmax_pv(scores, values, sink))
    return jnp.where(lo, outs[0], outs[1])


def _dft_real(x, cc_ref, sc_ref, cs, ss, scale):
    outs = []
    for g in range(FN_GROUPS):
        xg = x[:, g * FN_CH:(g + 1) * FN_CH]
        xc = _dot(xg, cc_ref[...]).astype(BF16)
        xs = _dot(xg, sc_ref[...]).astype(BF16)
        outs.append(_dot(cs, xc) - _dot(ss, xs))
    return jnp.concatenate(outs, axis=-1) * scale


def _ctx_attn_kernel(sink_ref, qna, kna, vna, qm, km, vm, qw, kw, vw, fv, cs_ref, ss_ref, cc_ref, sc_ref,
                     oa, ob, oc, od):
    for hp in range(4):
        cs = slice(hp * LANES, (hp + 1) * LANES)
        oa[:, cs] = _pair_attention(qna[:, cs], [kna[:, cs]], [vna[:, cs]]).astype(BF16)
    for hd in range(MLA_HEADS):
        s = _dot_nt(qm[:, hd * 256:(hd + 1) * 256], km[:, hd * 256:(hd + 1) * 256])
        ob[:, hd * 128:(hd + 1) * 128] = _softmax_pv([s], [vm[:, hd * 128:(hd + 1) * 128]]).astype(BF16)
    for hp in range(4):
        cs = slice(hp * LANES, (hp + 1) * LANES)
        kcs = slice((hp // 2) * LANES, (hp // 2 + 1) * LANES)
        oc[:, cs] = _pair_attention(qw[:, cs], [kw[:, kcs]], [vw[:, kcs]],
                                    sinks=(sink_ref[2 * hp], sink_ref[2 * hp + 1])).astype(BF16)
    od[...] = _dft_real(fv[...], cc_ref, sc_ref, cs_ref[...], ss_ref[...],
                        1.0 / math.sqrt(SEQ * FN_CH)).astype(BF16)


def _ctx_attention(pr, sink, dft):
    qna, kna, vna, qm, km, vm, qw, kw, vw, fv = pr[:10]
    n = qna.shape[0]
    row = lambda b: (b, 0)
    widths = [512, 512, 512, 1024, 1024, 512, 512, 256, 256, 512]
    in_specs = [pl.BlockSpec(memory_space=pltpu.SMEM)]
    in_specs += [pl.BlockSpec((SEQ, w), row) for w in widths]
    in_specs += [_const_spec((SEQ, SEQ)), _const_spec((SEQ, SEQ)), _const_spec((FN_CH, FN_CH)),
                 _const_spec((FN_CH, FN_CH))]
    return pl.pallas_call(
        _ctx_attn_kernel,
        grid=(n // SEQ,),
        in_specs=in_specs,
        out_specs=[pl.BlockSpec((SEQ, 512), row)] * 4,
        out_shape=[jax.ShapeDtypeStruct((n, 512), BF16)] * 4,
        compiler_params=_cp(("arbitrary",)),
        name="context_attention",
    )(sink, qna, kna, vna, qm, km, vm, qw, kw, vw, fv, dft['cs_ctx'], dft['ss_ctx'], dft['cc'], dft['sc'])


def _na_kernel(q_ref, k_ref, v_ref, kc_ref, vc_ref, bias_ref, o_ref):
    qt = pl.program_id(1)
    ks = pl.multiple_of(jnp.clip(qt * 4 - 4, 0, DEC_SEQ // GRID_W - NA_KROWS) * GRID_W, GRID_W)
    nk = NA_KROWS * GRID_W
    for hp in range(4):
        cs = slice(hp * LANES, (hp + 1) * LANES)

        def extra(e, idx, s, hp=hp):
            return s + bias_ref[0, 2 * hp + e] if idx == 0 else s

        o = _pair_attention(q_ref[:, cs], [k_ref[pl.ds(ks, nk), cs], kc_ref[:, cs]],
                            [v_ref[pl.ds(ks, nk), cs], vc_ref[:, cs]], extra=extra)
        o_ref[:, cs] = o.astype(BF16)


def _lat_specs(width_q, width_k, width_v):
    nq = DEC_SEQ // TQ
    return [
        pl.BlockSpec((TQ, width_q), lambda b, t: (b * nq + t, 0)),
        pl.BlockSpec((DEC_SEQ, width_k), lambda b, t: (b, 0)),
        pl.BlockSpec((DEC_SEQ, width_v), lambda b, t: (b, 0)),
        pl.BlockSpec((PAST_LEN, width_k), lambda b, t: (b, 0)),
        pl.BlockSpec((PAST_LEN, width_v), lambda b, t: (b, 0)),
    ]


def _lat_call(kernel, name, args, in_specs, nb):
    nq = DEC_SEQ // TQ
    return pl.pallas_call(
        kernel,
        grid=(nb, nq),
        in_specs=in_specs,
        out_specs=pl.BlockSpec((TQ, 512), lambda b, t: (b * nq + t, 0)),
        out_shape=jax.ShapeDtypeStruct((nb * DEC_SEQ, 512), BF16),
        compiler_params=_cp(("arbitrary", "arbitrary")),
        name=name,
    )(*args)


def _na_attention(q, k, v, kc, vc, bias, nb):
    nk = NA_KROWS * GRID_W
    specs = _lat_specs(512, 512, 512)
    specs.append(pl.BlockSpec((1, 8, TQ, nk),
                              lambda b, t: (jnp.minimum(t, 2) + jnp.maximum(t - 5, 0), 0, 0, 0)))
    return _lat_call(_na_kernel, "neighbourhood_attention", (q, k, v, kc, vc, bias), specs, nb)


def _mla_kernel(q_ref, k_ref, v_ref, kc_ref, vc_ref, o_ref):
    for hd in range(MLA_HEADS):
        ks = slice(hd * 256, (hd + 1) * 256)
        vs = slice(hd * 128, (hd + 1) * 128)
        q = q_ref[:, ks]
        scores = [_dot_nt(q, k_ref[:, ks]), _dot_nt(q, kc_ref[:, ks])]
        o_ref[:, vs] = _softmax_pv(scores, [v_ref[:, vs], vc_ref[:, vs]]).astype(BF16)


def _mla_attention(q, k, v, kc, vc, nb):
    return _lat_call(_mla_kernel, "latent_attention", (q, k, v, kc, vc), _lat_specs(1024, 1024, 512), nb)


def _win_kernel(sink_ref, q_ref, k_ref, v_ref, kc_ref, vc_ref, o_ref):
    qt = pl.program_id(1)
    nk = 4 * WINDOW
    ks = pl.multiple_of(jnp.clip(qt * TQ - WINDOW, 0, DEC_SEQ - nk), WINDOW)
    qpos = qt * TQ + lax.broadcasted_iota(jnp.int32, (TQ, nk), 0)
    kpos = ks + lax.broadcasted_iota(jnp.int32, (TQ, nk), 1)
    ok = jnp.abs(kpos - qpos) <= WINDOW

    def extra(e, idx, s):
        return jnp.where(ok, s, NEG) if idx == 0 else s

    for hp in range(4):
        cs = slice(hp * LANES, (hp + 1) * LANES)
        kcs = slice((hp // 2) * LANES, (hp // 2 + 1) * LANES)
        o = _pair_attention(q_ref[:, cs], [k_ref[pl.ds(ks, nk), kcs], kc_ref[:, kcs]],
                            [v_ref[pl.ds(ks, nk), kcs], vc_ref[:, kcs]], extra=extra,
                            sinks=(sink_ref[2 * hp], sink_ref[2 * hp + 1]))
        o_ref[:, cs] = o.astype(BF16)


def _win_attention(sink, q, k, v, kc, vc, nb):
    specs = [pl.BlockSpec(memory_space=pltpu.SMEM)] + _lat_specs(512, 256, 256)
    return _lat_call(_win_kernel, "window_attention", (sink, q, k, v, kc, vc), specs, nb)


def _fourier_kernel(x_ref, cc_ref, sc_ref, cs_ref, ss_ref, o_ref, xc_ref, xs_ref):
    @pl.when(pl.program_id(1) == 0)
    def _():
        for g in range(FN_GROUPS):
            gs = slice(g * FN_CH, (g + 1) * FN_CH)
            xg = x_ref[:, gs]
            xc_ref[:, gs] = _dot(xg, cc_ref[...]).astype(BF16)
            xs_ref[:, gs] = _dot(xg, sc_ref[...]).astype(BF16)

    y = _dot(cs_ref[...], xc_ref[...]) - _dot(ss_ref[...], xs_ref[...])
    o_ref[...] = (y * (1.0 / math.sqrt(DEC_SEQ * FN_CH))).astype(BF16)


def _fourier(x, dft, nb):
    nq = DEC_SEQ // TQ
    return pl.pallas_call(
        _fourier_kernel,
        grid=(nb, nq),
        in_specs=[
            pl.BlockSpec((DEC_SEQ, 512), lambda b, t: (b, 0)),
            _const_spec((FN_CH, FN_CH)), _const_spec((FN_CH, FN_CH)),
            pl.BlockSpec((TQ, DEC_SEQ), lambda b, t: (t, 0)),
            pl.BlockSpec((TQ, DEC_SEQ), lambda b, t: (t, 0)),
        ],
        out_specs=pl.BlockSpec((TQ, 512), lambda b, t: (b * nq + t, 0)),
        out_shape=jax.ShapeDtypeStruct((nb * DEC_SEQ, 512), BF16),
        scratch_shapes=[pltpu.VMEM((DEC_SEQ, 512), BF16), pltpu.VMEM((DEC_SEQ, 512), BF16)],
        compiler_params=_cp(("arbitrary", "arbitrary")),
        name="fourier_mix",
    )(x, dft['cc'], dft['sc'], dft['cs_lat'], dft['ss_lat'])


def _out_kernel(oa, ob, oc, od, x_ref, mod_ref, gffn_ref, w_ref, x1_o, h2_o):
    acc = _dot(oa[...], w_ref[0:512, :])
    acc += _dot(ob[...], w_ref[512:1024, :])
    acc += _dot(oc[...], w_ref[1024:1536, :])
    acc += _dot(od[...], w_ref[1536:2048, :])
    x1 = x_ref[...] + mod_ref[0, 2:3, :] * acc
    x1_o[...] = x1
    rinv = lax.rsqrt(jnp.mean(x1 * x1, axis=-1, keepdims=True) + EPS)
    h2 = (x1 * rinv) * (gffn_ref[...] * (1.0 + mod_ref[0, 4:5, :])) + mod_ref[0, 3:4, :]
    h2_o[...] = h2.astype(BF16)


def _mod_map(layer, tm, latent):
    if latent:
        return lambda i, *_: (layer * MOD_ROWS + 1 + (i * tm) // DEC_SEQ, 0, 0)
    return lambda i, *_: (layer * MOD_ROWS, 0, 0)


def _out_project(o4, x, mod, layer, p, latent):
    n = x.shape[0]
    tm = TM_OUT
    row = lambda i: (i, 0)
    return pl.pallas_call(
        _out_kernel,
        grid=(n // tm,),
        in_specs=[pl.BlockSpec((tm, 512), row)] * 4 + [
            pl.BlockSpec((tm, D_MODEL), row),
            pl.BlockSpec((1, MOD_CHUNKS, D_MODEL), _mod_map(layer, tm, latent)),
            _const_spec((1, D_MODEL)),
            _const_spec((D_MODEL, D_MODEL)),
        ],
        out_specs=[pl.BlockSpec((tm, D_MODEL), row), pl.BlockSpec((tm, D_MODEL), row)],
        out_shape=[jax.ShapeDtypeStruct((n, D_MODEL), F32), jax.ShapeDtypeStruct((n, D_MODEL), BF16)],
        compiler_params=_cp(("arbitrary",)),
        name="out_proj_latent" if latent else "out_proj_context",
    )(*o4, x, mod, p['g_ffn'], p['w_out'])


def _ffn_kernel(h_ref, hp_ref, hn_ref, x1_ref, mod_ref, wg_ref, wu_ref, wc_ref, wd_ref, o_ref, hext_ref, acc_ref,
                *, seq_len):
    i = pl.program_id(0)
    j = pl.program_id(1)
    tm = h_ref.shape[0]

    @pl.when(j == 0)
    def _():
        hext_ref[0:HALO, :] = hp_ref[...]
        hext_ref[HALO:HALO + tm, :] = h_ref[...]
        hext_ref[HALO + tm:, :] = hn_ref[...]
        acc_ref[...] = jnp.zeros_like(acc_ref)

    g_ext = _dot(hext_ref[...], wg_ref[...])
    rows_ext = g_ext.shape[0]
    pos = (i * tm + lax.broadcasted_iota(jnp.int32, (tm, 1), 0)) % seq_len
    g_mid = g_ext[HALO:HALO + tm]
    g_prev = pltpu.roll(g_ext, 1, 0)[HALO:HALO + tm]
    g_next = pltpu.roll(g_ext, rows_ext - 1, 0)[HALO:HALO + tm]
    g_prev = jnp.where(pos == 0, 0.0, g_prev)
    g_next = jnp.where(pos == seq_len - 1, 0.0, g_next)
    g = g_prev * wc_ref[0:1, :] + g_mid * wc_ref[1:2, :] + g_next * wc_ref[2:3, :]
    u = _dot(h_ref[...], wu_ref[...])
    act = (g / (1.0 + jnp.exp(-g))) * u
    acc_ref[...] += _dot(act.astype(BF16), wd_ref[...])

    @pl.when(j == pl.num_programs(1) - 1)
    def _():
        o_ref[...] = x1_ref[...] + mod_ref[0, 5:6, :] * acc_ref[...]


def _ffn(h2, x1, mod, layer, p, latent):
    n = x1.shape[0]
    tm, tf = TM_FFN, TF_FFN
    seq_len = DEC_SEQ if latent else SEQ
    assert n % tm == 0 and tm % HALO == 0 and D_FF % tf == 0
    hb = tm // HALO
    nhalo = n // HALO
    return pl.pallas_call(
        functools.partial(_ffn_kernel, seq_len=seq_len),
        grid=(n // tm, D_FF // tf),
        in_specs=[
            pl.BlockSpec((tm, D_MODEL), lambda i, j: (i, 0)),
            pl.BlockSpec((HALO, D_MODEL), lambda i, j: (jnp.maximum(i * hb - 1, 0), 0)),
            pl.BlockSpec((HALO, D_MODEL), lambda i, j: (jnp.minimum((i + 1) * hb, nhalo - 1), 0)),
            pl.BlockSpec((tm, D_MODEL), lambda i, j: (i, 0)),
            pl.BlockSpec((1, MOD_CHUNKS, D_MODEL), _mod_map(layer, tm, latent)),
            pl.BlockSpec((D_MODEL, tf), lambda i, j: (0, j)),
            pl.BlockSpec((D_MODEL, tf), lambda i, j: (0, j)),
            pl.BlockSpec((8, tf), lambda i, j: (0, j)),
            pl.BlockSpec((tf, D_MODEL), lambda i, j: (j, 0)),
        ],
        out_specs=pl.BlockSpec((tm, D_MODEL), lambda i, j: (i, 0)),
        out_shape=jax.ShapeDtypeStruct((n, D_MODEL), F32),
        scratch_shapes=[pltpu.VMEM((tm + 2 * HALO, D_MODEL), BF16), pltpu.VMEM((tm, D_MODEL), F32)],
        compiler_params=_cp(("arbitrary", "arbitrary")),
        name="ffn_latent" if latent else "ffn_context",
    )(h2, h2, h2, x1, mod, p['w_gate'], p['w_up'], p['w_conv'], p['w_down'])


def _tile_row(g, reps, scale=1.0):
    return (jnp.tile(g, reps) * scale).reshape(1, -1)


def _prep_layer(w, l):
    win = w['w_in'][l]
    zeros64 = jnp.zeros((D_MODEL, 64), F32)
    ck = win[:, 2624:2752]
    cv = win[:, 2752:2880]
    dup = lambda t: jnp.concatenate([t[:, :64], t[:, :64], t[:, 64:], t[:, 64:]], axis=1)
    win_r = jnp.concatenate([
        win[:, 0:1536], win[:, 1536:1920], win[:, 1920:2048], win[:, 2048:2112], zeros64,
        win[:, 2112:2624], dup(ck), dup(cv), win[:, 2880:3392]], axis=1).astype(BF16)

    wq = w['w_q_up'][l].reshape(MLA_Q_LORA, MLA_HEADS, MLA_QK)
    wq_r = jnp.concatenate([wq, jnp.zeros((MLA_Q_LORA, MLA_HEADS, 256 - MLA_QK), F32)], axis=2)
    wq_r = wq_r.reshape(MLA_Q_LORA, MLA_HEADS * 256).astype(BF16)
    wkv = w['w_kv_up'][l].reshape(MLA_KV_LORA, MLA_HEADS, 2 * MLA_NOPE)
    wkv_r = jnp.concatenate([wkv[:, :, :MLA_NOPE].reshape(MLA_KV_LORA, -1),
                             wkv[:, :, MLA_NOPE:].reshape(MLA_KV_LORA, -1)], axis=1).astype(BF16)

    pad_head = lambda g, s: jnp.tile(jnp.concatenate([g, jnp.zeros((256 - MLA_QK,), F32)]) * s, MLA_HEADS)
    wc = jnp.concatenate([w['w_conv'][l], jnp.zeros((5, D_FF), F32)], axis=0)
    return dict(
        g_mix=w['g_mix'][l].reshape(1, -1), g_ffn=w['g_ffn'][l].reshape(1, -1),
        w_in=win_r, w_q_up=wq_r, w_kv_up=wkv_r,
        gq_na=_tile_row(w['g_qn_na'][l], 8, HEAD_DIM ** -0.5), gk_na=_tile_row(w['g_kn_na'][l], 8),
        g_q_lora=w['g_q_lora'][l].reshape(1, -1), g_kv_lora=w['g_kv_lora'][l].reshape(1, -1),
        gq_mla=pad_head(w['g_qn_mla'][l], MLA_QK ** -0.5).reshape(1, -1),
        gk_mla=pad_head(w['g_kn_mla'][l], 1.0).reshape(1, -1),
        gq_win=_tile_row(w['g_qn_win'][l], 8, HEAD_DIM ** -0.5), gk_win=_tile_row(w['g_kn_win'][l], 4),
        sink=w['sink_win'][l],
        w_out=w['w_out'][l].astype(BF16), w_gate=w['w_gate'][l].astype(BF16), w_up=w['w_up'][l].astype(BF16),
        w_conv=wc, w_down=w['w_down'][l].astype(BF16),
    )


def _na_bias_tables(rpb):
    rows = DEC_SEQ // GRID_W
    nh = rpb.shape[0]
    ext = jnp.concatenate([jnp.repeat(rpb[..., :1], 48, axis=-1), rpb, jnp.repeat(rpb[..., -1:], 48, axis=-1)], axis=-1)
    blk = jnp.stack([ext[..., 63 - c:127 - c] for c in range(GRID_W)], axis=2)
    c = np.arange(GRID_W)
    cs = np.clip(c - 8, 0, GRID_W - 16)
    col_ok = (c[None, :] >= cs[:, None]) & (c[None, :] < cs[:, None] + 16)
    blk = jnp.where(jnp.asarray(col_ok)[None, None], blk, NEG)
    masked = jnp.full((nh, GRID_W, GRID_W), NEG, F32)
    tabs = []
    for r0 in (0, 4, 8, 24, 28):
        kstart = int(np.clip(r0 - 4, 0, rows - NA_KROWS))
        qrows = []
        for r in range(r0, r0 + TQ // GRID_W):
            rs = int(np.clip(r - 4, 0, rows - 8))
            qrows.append(jnp.concatenate(
                [blk[:, kr - r + 7] if rs <= kr < rs + 8 else masked for kr in range(kstart, kstart + NA_KROWS)],
                axis=-1))
        tabs.append(jnp.concatenate(qrows, axis=1))
    return jnp.stack(tabs, axis=0)


@functools.lru_cache(maxsize=None)
def _dft_tables():
    def cs(n):
        k = (np.arange(n)[:, None] * np.arange(n)[None, :]) % n
        ang = 2.0 * np.pi * k.astype(np.float64) / n
        return np.cos(ang), np.sin(ang)

    out = {}
    for name, n in (('ctx', SEQ), ('lat', DEC_SEQ)):
        c, s = cs(n)
        out['cs_' + name], out['ss_' + name] = c, s
    out['cc'], out['sc'] = cs(FN_CH)
    return out


@functools.lru_cache(maxsize=None)
def _rope_tables():
    t = np.arange(DEC_SEQ)
    quarter = MLA_ROPE // 4
    inv = ROPE_BASE ** (-np.arange(quarter, dtype=np.float64) / quarter)
    j = np.arange(MLA_ROPE)
    pos = np.where(j[None, :] < MLA_ROPE // 2, (t // GRID_W)[:, None], (t % GRID_W)[:, None]).astype(np.float64)
    ang = pos * inv[j % quarter][None, :]
    sign = np.where((j % 32) < 16, -1.0, 1.0)
    cos = np.cos(ang)
    sin = np.sin(ang) * sign[None, :]
    return np.tile(cos, (1, 2)).astype(np.float32), np.tile(sin, (1, 2)).astype(np.float32)


def kernel(x_prompt, x_sample, cache_na_k, cache_na_v, cache_mla_ckv, cache_mla_krope, cache_win_k, cache_win_v,
           c, c_ctx, w_mod, b_mod, g_mix, g_ffn, w_in, g_qn_na, g_kn_na, rpb_na, g_q_lora, w_q_up, g_kv_lora,
           w_kv_up, g_qn_mla, g_kn_mla, g_qn_win, g_kn_win, sink_win, w_out, w_gate, w_up, w_conv, w_down):
    w = dict(g_mix=g_mix, g_ffn=g_ffn, w_in=w_in, g_qn_na=g_qn_na, g_kn_na=g_kn_na, g_q_lora=g_q_lora,
             w_q_up=w_q_up, g_kv_lora=g_kv_lora, w_kv_up=w_kv_up, g_qn_mla=g_qn_mla, g_kn_mla=g_kn_mla,
             g_qn_win=g_qn_win, g_kn_win=g_kn_win, sink_win=sink_win, w_out=w_out, w_gate=w_gate, w_up=w_up,
             w_conv=w_conv, w_down=w_down)
    nb_ctx, nb_lat = x_prompt.shape[0], x_sample.shape[0]
    dft = {k: jnp.asarray(v, F32).astype(BF16) for k, v in _dft_tables().items()}
    rope_tabs = tuple(jnp.asarray(t) for t in _rope_tables())

    mod = _modulation(jnp.concatenate([c_ctx[None, :], c], axis=0), w_mod, b_mod)

    xp = x_prompt.reshape(nb_ctx * SEQ, D_MODEL)
    xs = x_sample.reshape(nb_lat * DEC_SEQ, D_MODEL)
    caches = []
    for l in range(DEPTH):
        p = _prep_layer(w, l)

        pr = _project(xp, mod, l, p, False, None)
        caches.append(pr[10:])
        o4 = _ctx_attention(pr, p['sink'], dft)
        x1, h2 = _out_project(o4, xp, mod, l, p, False)
        xp = _ffn(h2, x1, mod, l, p, False)

        qna, kna, vna, qm, km, vm, qw, kw, vw, fv = _project(xs, mod, l, p, True, rope_tabs)
        nctx = nb_lat * PAST_LEN
        kna_c = cache_na_k[:, l].reshape(nctx, 512).astype(BF16)
        vna_c = cache_na_v[:, l].reshape(nctx, 512).astype(BF16)
        kr_c = jnp.pad(cache_mla_krope[:, l].reshape(nctx, MLA_ROPE), ((0, 0), (0, LANES - MLA_ROPE)))
        km_c, vm_c = _mla_ctx(cache_mla_ckv[:, l].reshape(nctx, MLA_KV_LORA), kr_c, p)
        dup = lambda t: jnp.concatenate([t[:, :64], t[:, :64], t[:, 64:], t[:, 64:]], axis=1).astype(BF16)
        kw_c = dup(cache_win_k[:, l].reshape(nctx, 128))
        vw_c = dup(cache_win_v[:, l].reshape(nctx, 128))
        o_a = _na_attention(qna, kna, vna, kna_c, vna_c, _na_bias_tables(rpb_na[l]), nb_lat)
        o_b = _mla_attention(qm, km, vm, km_c, vm_c, nb_lat)
        o_c = _win_attention(p['sink'], qw, kw, vw, kw_c, vw_c, nb_lat)
        o_d = _fourier(fv, dft, nb_lat)
        x1, h2 = _out_project((o_a, o_b, o_c, o_d), xs, mod, l, p, True)
        xs = _ffn(h2, x1, mod, l, p, True)

    def stack(idx, shape):
        return jnp.stack([caches[l][idx].reshape((nb_ctx, SEQ) + shape) for l in range(DEPTH)], axis=1)

    return (xp.reshape(nb_ctx, SEQ, D_MODEL), xs.reshape(nb_lat, DEC_SEQ, D_MODEL),
            stack(0, (8, HEAD_DIM)), stack(1, (8, HEAD_DIM)), stack(2, (MLA_KV_LORA,)), stack(3, (MLA_ROPE,)),
            stack(4, (2, HEAD_DIM)), stack(5, (2, HEAD_DIM)))
```

```python
import functools
import math

import numpy as np
import jax
import jax.numpy as jnp
from jax import lax
from jax.experimental import pallas as pl
from jax.experimental.pallas import tpu as pltpu

F32 = jnp.float32
BF16 = jnp.bfloat16

D_MODEL = 2048
DEPTH = 2
SEQ = 256
DEC_SEQ = 2048
PAST_LEN = 512
GRID_W = 64
HEAD_DIM = 64
GROUP_W = 512
MLA_HEADS = 4
MLA_NOPE = 128
MLA_ROPE = 64
MLA_QK = MLA_NOPE + MLA_ROPE
MLA_Q_LORA = 384
MLA_KV_LORA = 128
WINDOW = 128
FN_GROUPS = 4
FN_CH = 128
D_FF = 5632
MOD_CHUNKS = 6
ROPE_BASE = 10000.0
EPS = 1e-6
NEG = -1e30

LANES = 128
MOD_ROWS = 8
VMEM_LIMIT = 56 * 1024 * 1024

C_AQ, C_AK, C_AV = 0, 512, 1024
C_MQ = 1536
C_CKV = 1920
C_KR = 2048
C_CQ = 2176
C_CK = 2688
C_CV = 2944
C_FV = 3200
IN_COLS_R = 3712

TM_PROJ = 512
TM_OUT = 512
TM_FFN = 512
TF_FFN = 512
HALO = 16
TQ = 256
TQ_MLA = 512
NA_KROWS = 12
NA_TILES = 5


def _cp(sem):
    return pltpu.CompilerParams(dimension_semantics=sem, vmem_limit_bytes=VMEM_LIMIT)


def _const_spec(shape):
    n = len(shape)
    return pl.BlockSpec(shape, lambda *_: (0,) * n)


def _layer_spec(shape, layer, resident=False):
    n = len(shape)
    mode = dict(pipeline_mode=pl.Buffered(1)) if resident else {}
    return pl.BlockSpec((None,) + tuple(shape), lambda *_: (layer,) + (0,) * n, **mode)


def _dot(a, b):
    return jnp.dot(a, b, preferred_element_type=F32)


def _dot_nt(a, b):
    return lax.dot_general(a, b, (((1,), (1,)), ((), ())), preferred_element_type=F32)


def _rms(t, n):
    return t * lax.rsqrt(jnp.sum(t * t, axis=-1, keepdims=True) * (1.0 / n) + EPS)


def _seg_rms64(t):
    rows, width = t.shape
    lo = lax.broadcasted_iota(jnp.int32, (rows, LANES), 1) < HEAD_DIM
    outs = []
    for c in range(width // LANES):
        tc = t[:, c * LANES:(c + 1) * LANES]
        sq = tc * tc
        s_lo = jnp.sum(jnp.where(lo, sq, 0.0), axis=-1, keepdims=True)
        s_hi = jnp.sum(jnp.where(lo, 0.0, sq), axis=-1, keepdims=True)
        ms = jnp.where(lo, s_lo, s_hi) * (1.0 / HEAD_DIM)
        outs.append(tc * lax.rsqrt(ms + EPS))
    return jnp.concatenate(outs, axis=-1)


def _rope128(t, cos, sin_signed):
    first = (lax.broadcasted_iota(jnp.int32, t.shape, 1) % 32) < 16
    partner = jnp.where(first, pltpu.roll(t, LANES - 16, 1), pltpu.roll(t, 16, 1))
    return t * cos + partner * sin_signed


def _rope_wide(t, cos, sin_signed):
    return jnp.concatenate(
        [_rope128(t[:, c * LANES:(c + 1) * LANES], cos, sin_signed) for c in range(t.shape[1] // LANES)], axis=-1)


MOD_TN = 512


def _mod_kernel(ct_ref, w_ref, b_ref, o_ref, xb_ref, *, nvec):
    tn = w_ref.shape[1]
    nct = tn // LANES

    @pl.when((pl.program_id(0) == 0) & (pl.program_id(1) == 0))
    def _():
        cv = ct_ref[...]
        sx = cv / (1.0 + jnp.exp(-cv))
        for v in range(nvec):
            xb_ref[v] = jnp.broadcast_to(sx[:, v:v + 1], (D_MODEL, LANES))

    def body(kc, accs):
        k0 = pl.multiple_of(kc * 8, 8)
        ws = [w_ref[pl.ds(k0, 8), t * LANES:(t + 1) * LANES] for t in range(nct)]
        new = []
        for v in range(nvec):
            xv = xb_ref[v, pl.ds(k0, 8), :]
            new.extend(accs[v * nct + t] + xv * ws[t] for t in range(nct))
        return tuple(new)

    zero = jnp.zeros((8, LANES), F32)
    accs = lax.fori_loop(0, w_ref.shape[0] // 8, body, (zero,) * (nvec * nct), unroll=2)
    rows = [jnp.concatenate([jnp.sum(accs[v * nct + t], axis=0, keepdims=True) for t in range(nct)], axis=1)
            for v in range(nvec)]
    rows.append(jnp.zeros((MOD_ROWS - nvec, tn), F32))
    o_ref[...] = jnp.concatenate(rows, axis=0) + b_ref[...]


def _modulation(cvecs, w_mod, b_mod):
    nvec = cvecs.shape[0]
    ct = jnp.zeros((D_MODEL, MOD_ROWS), F32).at[:, :nvec].set(cvecs.T)
    ncol = w_mod.shape[2]
    out = pl.pallas_call(
        functools.partial(_mod_kernel, nvec=nvec),
        grid=(DEPTH, ncol // MOD_TN),
        in_specs=[
            _const_spec((D_MODEL, MOD_ROWS)),
            pl.BlockSpec((None, D_MODEL, MOD_TN), lambda l, j: (l, 0, j)),
            pl.BlockSpec((None, 1, MOD_TN), lambda l, j: (l, 0, j)),
        ],
        out_specs=pl.BlockSpec((None, MOD_ROWS, MOD_TN), lambda l, j: (l, 0, j)),
        out_shape=jax.ShapeDtypeStruct((DEPTH, MOD_ROWS, ncol), F32),
        scratch_shapes=[pltpu.VMEM((nvec, D_MODEL, LANES), F32)],
        compiler_params=_cp(("arbitrary", "arbitrary")),
        name="modulation",
    )(ct, w_mod, b_mod.reshape(DEPTH, 1, ncol))
    return out.reshape(DEPTH * MOD_ROWS, MOD_CHUNKS, D_MODEL)


def _mla_kv_heads(ckv_n, kr, wkv_ref, gk_ref, rope):
    kv = _dot(ckv_n.astype(BF16), wkv_ref[...])
    ss_r = jnp.sum(kr * kr, axis=-1, keepdims=True)
    ks = []
    for h in range(MLA_HEADS):
        kn = kv[:, h * MLA_NOPE:(h + 1) * MLA_NOPE]
        ms = (jnp.sum(kn * kn, axis=-1, keepdims=True) + ss_r) * (1.0 / MLA_QK)
        r = lax.rsqrt(ms + EPS)
        tail = kr * r * gk_ref[:, h * 256 + 128:(h + 1) * 256]
        if rope is not None:
            tail = _rope128(tail, *rope)
        ks.append(kn * r * gk_ref[:, h * 256:h * 256 + 128])
        ks.append(tail)
    return jnp.concatenate(ks, axis=-1), kv[:, MLA_HEADS * MLA_NOPE:]


def _proj_kernel(*refs, latent):
    (x_ref, mod_ref, gmix_ref, win_ref, wq_ref, wkv_ref, gqna_ref, gkna_ref, gql_ref, gkvl_ref,
     gqm_ref, gkm_ref, gqw_ref, gkw_ref) = refs[:14]
    pos = 14
    rope = None
    if latent:
        rope = (refs[14][...], refs[15][...])
        pos = 16
    (qna_o, kna_o, vna_o, qm_o, km_o, vm_o, qw_o, kw_o, vw_o, fv_o) = refs[pos:pos + 10]
    cache_o = refs[pos + 10:]

    x = x_ref[...]
    rinv = lax.rsqrt(jnp.mean(x * x, axis=-1, keepdims=True) + EPS)
    h = (x * rinv) * (gmix_ref[...] * (1.0 + mod_ref[0, 1:2, :])) + mod_ref[0, 0:1, :]
    hb = h.astype(BF16)

    def proj(c0, c1):
        return _dot(hb, win_ref[:, c0:c1])

    qna_o[...] = (_seg_rms64(proj(C_AQ, C_AK)) * gqna_ref[...]).astype(BF16)
    k_na = _seg_rms64(proj(C_AK, C_AV)) * gkna_ref[...]
    kna_o[...] = k_na.astype(BF16)
    v_na = proj(C_AV, C_MQ)
    vna_o[...] = v_na.astype(BF16)

    cq = (_rms(proj(C_MQ, C_CKV), MLA_Q_LORA) * gql_ref[...]).astype(BF16)
    qu = _dot(cq, wq_ref[...])
    qs = []
    for hd in range(MLA_HEADS):
        blk = qu[:, hd * 256:(hd + 1) * 256]
        blk = blk * lax.rsqrt(jnp.sum(blk * blk, axis=-1, keepdims=True) * (1.0 / MLA_QK) + EPS)
        blk = blk * gqm_ref[:, hd * 256:(hd + 1) * 256]
        if latent:
            blk = jnp.concatenate([blk[:, :LANES], _rope128(blk[:, LANES:], *rope)], axis=-1)
        qs.append(blk)
    qm_o[...] = jnp.concatenate(qs, axis=-1).astype(BF16)
    ckv_n = _rms(proj(C_CKV, C_KR), MLA_KV_LORA) * gkvl_ref[...]
    kr = proj(C_KR, C_CQ)
    k_m, v_m = _mla_kv_heads(ckv_n, kr, wkv_ref, gkm_ref, rope)
    km_o[...] = k_m.astype(BF16)
    vm_o[...] = v_m.astype(BF16)

    q_w = _seg_rms64(proj(C_CQ, C_CK)) * gqw_ref[...]
    k_w = _seg_rms64(proj(C_CK, C_CV)) * gkw_ref[...]
    if latent:
        q_w = _rope_wide(q_w, *rope)
        k_w = _rope_wide(k_w, *rope)
    qw_o[...] = q_w.astype(BF16)
    kw_o[...] = k_w.astype(BF16)
    v_w = proj(C_CV, C_FV)
    vw_o[...] = v_w.astype(BF16)

    fv_o[...] = proj(C_FV, IN_COLS_R).astype(BF16)

    if not latent:
        kna32_o, vna32_o, ckv32_o, kr32_o, kw32_o, vw32_o = cache_o
        lo = lax.broadcasted_iota(jnp.int32, (x.shape[0], LANES), 1) < HEAD_DIM
        kna32_o[...] = k_na
        vna32_o[...] = v_na
        ckv32_o[...] = ckv_n
        kr32_o[...] = kr[:, :MLA_ROPE]
        kw32_o[...] = jnp.where(lo, k_w[:, :LANES], k_w[:, LANES:])
        vw32_o[...] = jnp.where(lo, v_w[:, :LANES], v_w[:, LANES:])


def _project(x, mod, layer, p, latent, rope_tabs):
    n = x.shape[0]
    tm = TM_PROJ
    row = lambda i: (i, 0)
    if latent:
        mod_map = lambda i: (layer * MOD_ROWS + 1 + (i * tm) // DEC_SEQ, 0, 0)
    else:
        mod_map = lambda i: (layer * MOD_ROWS, 0, 0)
    in_specs = [
        pl.BlockSpec((tm, D_MODEL), row),
        pl.BlockSpec((1, MOD_CHUNKS, D_MODEL), mod_map),
        _layer_spec((1, D_MODEL), layer),
        _layer_spec((D_MODEL, IN_COLS_R), layer, resident=True),
        _layer_spec((MLA_Q_LORA, 1024), layer, resident=True),
        _layer_spec((MLA_KV_LORA, 1024), layer, resident=True),
    ] + [_layer_spec((1, wd), layer) for wd in (512, 512, MLA_Q_LORA, MLA_KV_LORA, 1024, 1024, 512, 256)]
    args = [x, mod, p['g_mix'], p['w_in'], p['w_q_up'], p['w_kv_up'], p['gq_na'], p['gk_na'], p['g_q_lora'],
            p['g_kv_lora'], p['gq_mla'], p['gk_mla'], p['gq_win'], p['gk_win']]
    if latent:
        nt = DEC_SEQ // tm
        in_specs += [pl.BlockSpec((tm, LANES), lambda i: (i % nt, 0))] * 2
        args += list(rope_tabs)
    widths = [512, 512, 512, 1024, 1024, 512, 512, 256, 256, 512]
    out_specs = [pl.BlockSpec((tm, w), row) for w in widths]
    out_shape = [jax.ShapeDtypeStruct((n, w), BF16) for w in widths]
    if not latent:
        cache_w = [512, 512, MLA_KV_LORA, MLA_ROPE, 128, 128]
        out_specs += [pl.BlockSpec((tm, w), row) for w in cache_w]
        out_shape += [jax.ShapeDtypeStruct((n, w), F32) for w in cache_w]
    return pl.pallas_call(
        functools.partial(_proj_kernel, latent=latent),
        grid=(n // tm,),
        in_specs=in_specs,
        out_specs=out_specs,
        out_shape=out_shape,
        compiler_params=_cp(("arbitrary",)),
        name="proj_latent" if latent else "proj_context",
    )(*args)


def _mla_ctx_kernel(ckv_ref, kr_ref, wkv_ref, gk_ref, k_o, v_o):
    k_m, v_m = _mla_kv_heads(ckv_ref[...], kr_ref[...], wkv_ref, gk_ref, None)
    k_o[...] = k_m.astype(BF16)
    v_o[...] = v_m.astype(BF16)


def _mla_ctx(ckv, kr128, p, layer, nb):
    n = nb * PAST_LEN
    src = lambda b: (b * DEPTH + layer, 0)
    row = lambda b: (b, 0)
    return pl.pallas_call(
        _mla_ctx_kernel,
        grid=(nb,),
        in_specs=[pl.BlockSpec((PAST_LEN, MLA_KV_LORA), src), pl.BlockSpec((PAST_LEN, LANES), src),
                  _layer_spec((MLA_KV_LORA, 1024), layer), _layer_spec((1, 1024), layer)],
        out_specs=[pl.BlockSpec((PAST_LEN, 1024), row), pl.BlockSpec((PAST_LEN, 512), row)],
        out_shape=[jax.ShapeDtypeStruct((n, 1024), BF16), jax.ShapeDtypeStruct((n, 512), BF16)],
        compiler_params=_cp(("arbitrary",)),
        name="mla_cached_kv",
    )(ckv, kr128, p['w_kv_up'], p['gk_mla'])


def _softmax_pv(scores, values, sink=None):
    m = functools.reduce(jnp.maximum, [jnp.max(s, axis=-1, keepdims=True) for s in scores])
    if sink is not None:
        m = jnp.maximum(m, sink)
    ps = [jnp.exp(s - m) for s in scores]
    l = functools.reduce(lambda a, b: a + b, [jnp.sum(p, axis=-1, keepdims=True) for p in ps])
    if sink is not None:
        l = l + jnp.exp(sink - m)
    o = functools.reduce(lambda a, b: a + b, [_dot(p.astype(BF16), v) for p, v in zip(ps, values)])
    return o / l


def _half_masks():
    lane = lax.broadcasted_iota(jnp.int32, (1, LANES), 1)
    lo = lane < HEAD_DIM
    return jnp.where(lo, 1.0, 0.0).astype(BF16), jnp.where(lo, 0.0, 1.0).astype(BF16)


def _pair_attention(q2, keys, values, extra=None, sinks=None):
    rows = q2.shape[0]
    lo = lax.broadcasted_iota(jnp.int32, (rows, LANES), 1) < HEAD_DIM
    outs = []
    for e, msk in enumerate(_half_masks()):
        qm = q2 * msk
        scores = []
        for idx, k2 in enumerate(keys):
            s = _dot_nt(qm, k2)
            if extra is not None:
                s = extra(e, idx, s)
            scores.append(s)
        sink = None
        if sinks is not None:
            sink = jnp.full((rows, 1), sinks[e], F32)
        outs.append(_softmax_pv(scores, values, sink))
    return jnp.where(lo, outs[0], outs[1])


def _dft_real(x, cc_ref, sc_ref, cs, ss, scale):
    outs = []
    for g in range(FN_GROUPS):
        xg = x[:, g * FN_CH:(g + 1) * FN_CH]
        xc = _dot(xg, cc_ref[...]).astype(BF16)
        xs = _dot(xg, sc_ref[...]).astype(BF16)
        outs.append(_dot(cs, xc) - _dot(ss, xs))
    return jnp.concatenate(outs, axis=-1) * scale


def _ctx_attn_kernel(sink_ref, qna, kna, vna, qm, km, vm, qw, kw, vw, fv, cs_ref, ss_ref, cc_ref, sc_ref,
                     oa, ob, oc, od, *, layer):
    for hp in range(4):
        cs = slice(hp * LANES, (hp + 1) * LANES)
        oa[:, cs] = _pair_attention(qna[:, cs], [kna[:, cs]], [vna[:, cs]]).astype(BF16)
    for hd in range(MLA_HEADS):
        s = _dot_nt(qm[:, hd * 256:(hd + 1) * 256], km[:, hd * 256:(hd + 1) * 256])
        ob[:, hd * 128:(hd + 1) * 128] = _softmax_pv([s], [vm[:, hd * 128:(hd + 1) * 128]]).astype(BF16)
    for hp in range(4):
        cs = slice(hp * LANES, (hp + 1) * LANES)
        kcs = slice((hp // 2) * LANES, (hp // 2 + 1) * LANES)
        oc[:, cs] = _pair_attention(qw[:, cs], [kw[:, kcs]], [vw[:, kcs]],
                                    sinks=(sink_ref[layer, 2 * hp], sink_ref[layer, 2 * hp + 1])).astype(BF16)
    od[...] = _dft_real(fv[...], cc_ref, sc_ref, cs_ref[...], ss_ref[...],
                        1.0 / math.sqrt(SEQ * FN_CH)).astype(BF16)


def _ctx_attention(pr, sink, dft, layer):
    qna, kna, vna, qm, km, vm, qw, kw, vw, fv = pr[:10]
    n = qna.shape[0]
    row = lambda b: (b, 0)
    widths = [512, 512, 512, 1024, 1024, 512, 512, 256, 256, 512]
    in_specs = [pl.BlockSpec(memory_space=pltpu.SMEM)]
    in_specs += [pl.BlockSpec((SEQ, w), row) for w in widths]
    in_specs += [_const_spec((SEQ, SEQ)), _const_spec((SEQ, SEQ)), _const_spec((FN_CH, FN_CH)),
                 _const_spec((FN_CH, FN_CH))]
    return pl.pallas_call(
        functools.partial(_ctx_attn_kernel, layer=layer),
        grid=(n // SEQ,),
        in_specs=in_specs,
        out_specs=[pl.BlockSpec((SEQ, 512), row)] * 4,
        out_shape=[jax.ShapeDtypeStruct((n, 512), BF16)] * 4,
        compiler_params=_cp(("arbitrary",)),
        name="context_attention",
    )(sink, qna, kna, vna, qm, km, vm, qw, kw, vw, fv, dft['cs_ctx'], dft['ss_ctx'], dft['cc'], dft['sc'])


def _na_kernel(q_ref, k_ref, v_ref, kc_ref, vc_ref, bias_ref, o_ref):
    qt = pl.program_id(1)
    ks = pl.multiple_of(jnp.clip(qt * 4 - 4, 0, DEC_SEQ // GRID_W - NA_KROWS) * GRID_W, GRID_W)
    nk = NA_KROWS * GRID_W
    for hp in range(4):
        cs = slice(hp * LANES, (hp + 1) * LANES)

        def extra(e, idx, s, hp=hp):
            return s + bias_ref[0, 2 * hp + e] if idx == 0 else s

        o = _pair_attention(q_ref[:, cs], [k_ref[pl.ds(ks, nk), cs], kc_ref[:, cs]],
                            [v_ref[pl.ds(ks, nk), cs], vc_ref[:, cs]], extra=extra)
        o_ref[:, cs] = o.astype(BF16)


def _lat_specs(width_q, width_k, width_v, ctx_map, tq=TQ):
    nq = DEC_SEQ // tq
    ctx_nd = len(ctx_map(0, 0))
    lead = (None,) * (ctx_nd - 2)
    return [
        pl.BlockSpec((tq, width_q), lambda b, t: (b * nq + t, 0)),
        pl.BlockSpec((DEC_SEQ, width_k), lambda b, t: (b, 0)),
        pl.BlockSpec((DEC_SEQ, width_v), lambda b, t: (b, 0)),
        pl.BlockSpec(lead + (PAST_LEN, width_k), ctx_map),
        pl.BlockSpec(lead + (PAST_LEN, width_v), ctx_map),
    ]


def _lat_call(kernel, name, args, in_specs, nb, tq=TQ):
    nq = DEC_SEQ // tq
    return pl.pallas_call(
        kernel,
        grid=(nb, nq),
        in_specs=in_specs,
        out_specs=pl.BlockSpec((tq, 512), lambda b, t: (b * nq + t, 0)),
        out_shape=jax.ShapeDtypeStruct((nb * DEC_SEQ, 512), BF16),
        compiler_params=_cp(("arbitrary", "arbitrary")),
        name=name,
    )(*args)


def _na_attention(q, k, v, kc, vc, bias, nb, layer):
    nk = NA_KROWS * GRID_W
    specs = _lat_specs(512, 512, 512, lambda b, t: (b, layer, 0, 0))
    specs.append(pl.BlockSpec((1, 8, TQ, nk),
                              lambda b, t: (jnp.minimum(t, 2) + jnp.maximum(t - 5, 0), 0, 0, 0)))
    return _lat_call(_na_kernel, "neighbourhood_attention", (q, k, v, kc, vc, bias), specs, nb)


def _mla_kernel(q_ref, k_ref, v_ref, kc_ref, vc_ref, o_ref):
    for hd in range(MLA_HEADS):
        ks = slice(hd * 256, (hd + 1) * 256)
        vs = slice(hd * 128, (hd + 1) * 128)
        q = q_ref[:, ks]
        scores = [_dot_nt(q, k_ref[:, ks]), _dot_nt(q, kc_ref[:, ks])]
        o_ref[:, vs] = _softmax_pv(scores, [v_ref[:, vs], vc_ref[:, vs]]).astype(BF16)


def _mla_attention(q, k, v, kc, vc, nb):
    specs = _lat_specs(1024, 1024, 512, lambda b, t: (b, 0), tq=TQ_MLA)
    return _lat_call(_mla_kernel, "latent_attention", (q, k, v, kc, vc), specs, nb, tq=TQ_MLA)


def _win_kernel(sink_ref, q_ref, k_ref, v_ref, kc_ref, vc_ref, o_ref, *, layer):
    qt = pl.program_id(1)
    nk = 4 * WINDOW
    ks = pl.multiple_of(jnp.clip(qt * TQ - WINDOW, 0, DEC_SEQ - nk), WINDOW)
    qpos = qt * TQ + lax.broadcasted_iota(jnp.int32, (TQ, nk), 0)
    kpos = ks + lax.broadcasted_iota(jnp.int32, (TQ, nk), 1)
    ok = jnp.abs(kpos - qpos) <= WINDOW

    def extra(e, idx, s):
        return jnp.where(ok, s, NEG) if idx == 0 else s

    for hp in range(4):
        cs = slice(hp * LANES, (hp + 1) * LANES)
        kcs = slice((hp // 2) * LANES, (hp // 2 + 1) * LANES)
        o = _pair_attention(q_ref[:, cs], [k_ref[pl.ds(ks, nk), kcs], kc_ref[:, kcs]],
                            [v_ref[pl.ds(ks, nk), kcs], vc_ref[:, kcs]], extra=extra,
                            sinks=(sink_ref[layer, 2 * hp], sink_ref[layer, 2 * hp + 1]))
        o_ref[:, cs] = o.astype(BF16)


def _win_attention(sink, q, k, v, kc, vc, nb, layer):
    specs = [pl.BlockSpec(memory_space=pltpu.SMEM)]
    specs += _lat_specs(512, 256, 256, lambda b, t: (b, layer, 0, 0))
    return _lat_call(functools.partial(_win_kernel, layer=layer), "window_attention", (sink, q, k, v, kc, vc),
                     specs, nb)


def _fourier_kernel(x_ref, cc_ref, sc_ref, cs_ref, ss_ref, o_ref, xc_ref, xs_ref):
    @pl.when(pl.program_id(1) == 0)
    def _():
        for g in range(FN_GROUPS):
            gs = slice(g * FN_CH, (g + 1) * FN_CH)
            xg = x_ref[:, gs]
            xc_ref[:, gs] = _dot(xg, cc_ref[...]).astype(BF16)
            xs_ref[:, gs] = _dot(xg, sc_ref[...]).astype(BF16)

    y = _dot(cs_ref[...], xc_ref[...]) - _dot(ss_ref[...], xs_ref[...])
    o_ref[...] = (y * (1.0 / math.sqrt(DEC_SEQ * FN_CH))).astype(BF16)


def _fourier(x, dft, nb):
    nq = DEC_SEQ // TQ
    return pl.pallas_call(
        _fourier_kernel,
        grid=(nb, nq),
        in_specs=[
            pl.BlockSpec((DEC_SEQ, 512), lambda b, t: (b, 0)),
            _const_spec((FN_CH, FN_CH)), _const_spec((FN_CH, FN_CH)),
            pl.BlockSpec((TQ, DEC_SEQ), lambda b, t: (t, 0)),
            pl.BlockSpec((TQ, DEC_SEQ), lambda b, t: (t, 0)),
        ],
        out_specs=pl.BlockSpec((TQ, 512), lambda b, t: (b * nq + t, 0)),
        out_shape=jax.ShapeDtypeStruct((nb * DEC_SEQ, 512), BF16),
        scratch_shapes=[pltpu.VMEM((DEC_SEQ, 512), BF16), pltpu.VMEM((DEC_SEQ, 512), BF16)],
        compiler_params=_cp(("arbitrary", "arbitrary")),
        name="fourier_mix",
    )(x, dft['cc'], dft['sc'], dft['cs_lat'], dft['ss_lat'])


def _out_kernel(oa, ob, oc, od, x_ref, mod_ref, gffn_ref, w_ref, x1_o, h2_o):
    acc = _dot(oa[...], w_ref[0:512, :])
    acc += _dot(ob[...], w_ref[512:1024, :])
    acc += _dot(oc[...], w_ref[1024:1536, :])
    acc += _dot(od[...], w_ref[1536:2048, :])
    x1 = x_ref[...] + mod_ref[0, 2:3, :] * acc
    x1_o[...] = x1
    rinv = lax.rsqrt(jnp.mean(x1 * x1, axis=-1, keepdims=True) + EPS)
    h2 = (x1 * rinv) * (gffn_ref[...] * (1.0 + mod_ref[0, 4:5, :])) + mod_ref[0, 3:4, :]
    h2_o[...] = h2.astype(BF16)


def _mod_map(layer, tm, latent):
    if latent:
        return lambda i, *_: (layer * MOD_ROWS + 1 + (i * tm) // DEC_SEQ, 0, 0)
    return lambda i, *_: (layer * MOD_ROWS, 0, 0)


def _out_project(o4, x, mod, layer, p, latent):
    n = x.shape[0]
    tm = TM_OUT
    row = lambda i: (i, 0)
    return pl.pallas_call(
        _out_kernel,
        grid=(n // tm,),
        in_specs=[pl.BlockSpec((tm, 512), row)] * 4 + [
            pl.BlockSpec((tm, D_MODEL), row),
            pl.BlockSpec((1, MOD_CHUNKS, D_MODEL), _mod_map(layer, tm, latent)),
            _layer_spec((1, D_MODEL), layer),
            _layer_spec((D_MODEL, D_MODEL), layer, resident=True),
        ],
        out_specs=[pl.BlockSpec((tm, D_MODEL), row), pl.BlockSpec((tm, D_MODEL), row)],
        out_shape=[jax.ShapeDtypeStruct((n, D_MODEL), F32), jax.ShapeDtypeStruct((n, D_MODEL), BF16)],
        compiler_params=_cp(("arbitrary",)),
        name="out_proj_latent" if latent else "out_proj_context",
    )(*o4, x, mod, p['g_ffn'], p['w_out'])


def _ffn_kernel(h_ref, hp_ref, hn_ref, x1_ref, mod_ref, wg_ref, wu_ref, wc_ref, wd_ref, o_ref, hext_ref, acc_ref,
                *, seq_len):
    i = pl.program_id(0)
    j = pl.program_id(1)
    tm = h_ref.shape[0]

    @pl.when(j == 0)
    def _():
        hext_ref[0:HALO, :] = hp_ref[...]
        hext_ref[HALO:HALO + tm, :] = h_ref[...]
        hext_ref[HALO + tm:, :] = hn_ref[...]
        acc_ref[...] = jnp.zeros_like(acc_ref)

    g_ext = _dot(hext_ref[...], wg_ref[...])
    rows_ext = g_ext.shape[0]
    pos = (i * tm + lax.broadcasted_iota(jnp.int32, (tm, 1), 0)) % seq_len
    g_mid = g_ext[HALO:HALO + tm]
    g_prev = pltpu.roll(g_ext, 1, 0)[HALO:HALO + tm]
    g_next = pltpu.roll(g_ext, rows_ext - 1, 0)[HALO:HALO + tm]
    g_prev = jnp.where(pos == 0, 0.0, g_prev)
    g_next = jnp.where(pos == seq_len - 1, 0.0, g_next)
    g = g_prev * wc_ref[0:1, :] + g_mid * wc_ref[1:2, :] + g_next * wc_ref[2:3, :]
    u = _dot(h_ref[...], wu_ref[...])
    act = (g / (1.0 + jnp.exp(-g))) * u
    acc_ref[...] += _dot(act.astype(BF16), wd_ref[...])

    @pl.when(j == pl.num_programs(1) - 1)
    def _():
        o_ref[...] = x1_ref[...] + mod_ref[0, 5:6, :] * acc_ref[...]


def _ffn(h2, x1, mod, layer, p, latent):
    n = x1.shape[0]
    tm, tf = TM_FFN, TF_FFN
    seq_len = DEC_SEQ if latent else SEQ
    assert n % tm == 0 and tm % HALO == 0 and D_FF % tf == 0
    hb = tm // HALO
    nhalo = n // HALO
    return pl.pallas_call(
        functools.partial(_ffn_kernel, seq_len=seq_len),
        grid=(n // tm, D_FF // tf),
        in_specs=[
            pl.BlockSpec((tm, D_MODEL), lambda i, j: (i, 0)),
            pl.BlockSpec((HALO, D_MODEL), lambda i, j: (jnp.maximum(i * hb - 1, 0), 0)),
            pl.BlockSpec((HALO, D_MODEL), lambda i, j: (jnp.minimum((i + 1) * hb, nhalo - 1), 0)),
            pl.BlockSpec((tm, D_MODEL), lambda i, j: (i, 0)),
            pl.BlockSpec((1, MOD_CHUNKS, D_MODEL), _mod_map(layer, tm, latent)),
            pl.BlockSpec((None, D_MODEL, tf), lambda i, j: (layer, 0, j)),
            pl.BlockSpec((None, D_MODEL, tf), lambda i, j: (layer, 0, j)),
            pl.BlockSpec((None, 8, tf), lambda i, j: (layer, 0, j)),
            pl.BlockSpec((None, tf, D_MODEL), lambda i, j: (layer, j, 0)),
        ],
        out_specs=pl.BlockSpec((tm, D_MODEL), lambda i, j: (i, 0)),
        out_shape=jax.ShapeDtypeStruct((n, D_MODEL), F32),
        scratch_shapes=[pltpu.VMEM((tm + 2 * HALO, D_MODEL), BF16), pltpu.VMEM((tm, D_MODEL), F32)],
        compiler_params=_cp(("arbitrary", "arbitrary")),
        name="ffn_latent" if latent else "ffn_context",
    )(h2, h2, h2, x1, mod, p['w_gate'], p['w_up'], p['w_conv'], p['w_down'])


def _tile_row(g, reps, scale=1.0):
    return (jnp.tile(g, (1, reps)) * scale)[:, None, :]


def _prep(w):
    win = w['w_in']
    zeros64 = jnp.zeros((DEPTH, D_MODEL, 64), F32)
    dup = lambda t: jnp.concatenate([t[..., :64], t[..., :64], t[..., 64:], t[..., 64:]], axis=-1)
    win_r = jnp.concatenate([
        win[..., 0:2112], zeros64, win[..., 2112:2624], dup(win[..., 2624:2752]), dup(win[..., 2752:2880]),
        win[..., 2880:3392]], axis=-1).astype(BF16)

    wq = w['w_q_up'].reshape(DEPTH, MLA_Q_LORA, MLA_HEADS, MLA_QK)
    wq_r = jnp.concatenate([wq, jnp.zeros((DEPTH, MLA_Q_LORA, MLA_HEADS, 256 - MLA_QK), F32)], axis=-1)
    wq_r = wq_r.reshape(DEPTH, MLA_Q_LORA, MLA_HEADS * 256).astype(BF16)
    wkv = w['w_kv_up'].reshape(DEPTH, MLA_KV_LORA, MLA_HEADS, 2 * MLA_NOPE)
    wkv_r = jnp.concatenate([wkv[..., :MLA_NOPE].reshape(DEPTH, MLA_KV_LORA, -1),
                             wkv[..., MLA_NOPE:].reshape(DEPTH, MLA_KV_LORA, -1)], axis=-1).astype(BF16)

    def pad_head(g, s):
        return _tile_row(jnp.concatenate([g, jnp.zeros((DEPTH, 256 - MLA_QK), F32)], axis=-1), MLA_HEADS, s)

    wc = jnp.concatenate([w['w_conv'], jnp.zeros((DEPTH, 5, D_FF), F32)], axis=1)
    return dict(
        g_mix=w['g_mix'][:, None, :], g_ffn=w['g_ffn'][:, None, :],
        w_in=win_r, w_q_up=wq_r, w_kv_up=wkv_r,
        gq_na=_tile_row(w['g_qn_na'], 8, HEAD_DIM ** -0.5), gk_na=_tile_row(w['g_kn_na'], 8),
        g_q_lora=w['g_q_lora'][:, None, :], g_kv_lora=w['g_kv_lora'][:, None, :],
        gq_mla=pad_head(w['g_qn_mla'], MLA_QK ** -0.5), gk_mla=pad_head(w['g_kn_mla'], 1.0),
        gq_win=_tile_row(w['g_qn_win'], 8, HEAD_DIM ** -0.5), gk_win=_tile_row(w['g_kn_win'], 4),
        sink=w['sink_win'],
        w_out=w['w_out'].astype(BF16), w_gate=w['w_gate'].astype(BF16), w_up=w['w_up'].astype(BF16),
        w_conv=wc, w_down=w['w_down'].astype(BF16),
    )


def _na_bias_tables(rpb):
    rows = DEC_SEQ // GRID_W
    nh = rpb.shape[0]
    ext = jnp.concatenate([jnp.repeat(rpb[..., :1], 48, axis=-1), rpb, jnp.repeat(rpb[..., -1:], 48, axis=-1)], axis=-1)
    blk = jnp.stack([ext[..., 63 - c:127 - c] for c in range(GRID_W)], axis=2)
    c = np.arange(GRID_W)
    cs = np.clip(c - 8, 0, GRID_W - 16)
    col_ok = (c[None, :] >= cs[:, None]) & (c[None, :] < cs[:, None] + 16)
    blk = jnp.where(jnp.asarray(col_ok)[None, None], blk, NEG)
    masked = jnp.full((nh, GRID_W, GRID_W), NEG, F32)
    tabs = []
    for r0 in (0, 4, 8, 24, 28):
        kstart = int(np.clip(r0 - 4, 0, rows - NA_KROWS))
        qrows = []
        for r in range(r0, r0 + TQ // GRID_W):
            rs = int(np.clip(r - 4, 0, rows - 8))
            qrows.append(jnp.concatenate(
                [blk[:, kr - r + 7] if rs <= kr < rs + 8 else masked for kr in range(kstart, kstart + NA_KROWS)],
                axis=-1))
        tabs.append(jnp.concatenate(qrows, axis=1))
    return jnp.stack(tabs, axis=0)


@functools.lru_cache(maxsize=None)
def _dft_tables():
    def cs(n):
        k = (np.arange(n)[:, None] * np.arange(n)[None, :]) % n
        ang = 2.0 * np.pi * k.astype(np.float64) / n
        return np.cos(ang), np.sin(ang)

    out = {}
    for name, n in (('ctx', SEQ), ('lat', DEC_SEQ)):
        c, s = cs(n)
        out['cs_' + name], out['ss_' + name] = c, s
    out['cc'], out['sc'] = cs(FN_CH)
    return out


@functools.lru_cache(maxsize=None)
def _rope_tables():
    t = np.arange(DEC_SEQ)
    quarter = MLA_ROPE // 4
    inv = ROPE_BASE ** (-np.arange(quarter, dtype=np.float64) / quarter)
    j = np.arange(MLA_ROPE)
    pos = np.where(j[None, :] < MLA_ROPE // 2, (t // GRID_W)[:, None], (t % GRID_W)[:, None]).astype(np.float64)
    ang = pos * inv[j % quarter][None, :]
    sign = np.where((j % 32) < 16, -1.0, 1.0)
    cos = np.cos(ang)
    sin = np.sin(ang) * sign[None, :]
    return np.tile(cos, (1, 2)).astype(np.float32), np.tile(sin, (1, 2)).astype(np.float32)


def kernel(x_prompt, x_sample, cache_na_k, cache_na_v, cache_mla_ckv, cache_mla_krope, cache_win_k, cache_win_v,
           c, c_ctx, w_mod, b_mod, g_mix, g_ffn, w_in, g_qn_na, g_kn_na, rpb_na, g_q_lora, w_q_up, g_kv_lora,
           w_kv_up, g_qn_mla, g_kn_mla, g_qn_win, g_kn_win, sink_win, w_out, w_gate, w_up, w_conv, w_down):
    w = dict(g_mix=g_mix, g_ffn=g_ffn, w_in=w_in, g_qn_na=g_qn_na, g_kn_na=g_kn_na, g_q_lora=g_q_lora,
             w_q_up=w_q_up, g_kv_lora=g_kv_lora, w_kv_up=w_kv_up, g_qn_mla=g_qn_mla, g_kn_mla=g_kn_mla,
             g_qn_win=g_qn_win, g_kn_win=g_kn_win, sink_win=sink_win, w_out=w_out, w_gate=w_gate, w_up=w_up,
             w_conv=w_conv, w_down=w_down)
    nb_ctx, nb_lat = x_prompt.shape[0], x_sample.shape[0]
    dft = {k: jnp.asarray(v, F32).astype(BF16) for k, v in _dft_tables().items()}
    rope_tabs = tuple(jnp.asarray(t) for t in _rope_tables())

    mod = _modulation(jnp.concatenate([c_ctx[None, :], c], axis=0), w_mod, b_mod)

    xp = x_prompt.reshape(nb_ctx * SEQ, D_MODEL)
    xs = x_sample.reshape(nb_lat * DEC_SEQ, D_MODEL)
    caches = []
    p = _prep(w)
    nctx = nb_lat * DEPTH * PAST_LEN
    dup = lambda t: jnp.concatenate([t[..., :64], t[..., :64], t[..., 64:], t[..., 64:]], axis=-1).astype(BF16)
    kna_c = cache_na_k.reshape(nb_lat, DEPTH, PAST_LEN, 512).astype(BF16)
    vna_c = cache_na_v.reshape(nb_lat, DEPTH, PAST_LEN, 512).astype(BF16)
    kw_c = dup(cache_win_k.reshape(nb_lat, DEPTH, PAST_LEN, 128))
    vw_c = dup(cache_win_v.reshape(nb_lat, DEPTH, PAST_LEN, 128))
    ckv_c = cache_mla_ckv.reshape(nctx, MLA_KV_LORA)
    kr_c = jnp.pad(cache_mla_krope.reshape(nctx, MLA_ROPE), ((0, 0), (0, LANES - MLA_ROPE)))
    for l in range(DEPTH):

        pr = _project(xp, mod, l, p, False, None)
        caches.append(pr[10:])
        o4 = _ctx_attention(pr, p['sink'], dft, l)
        x1, h2 = _out_project(o4, xp, mod, l, p, False)
        xp = _ffn(h2, x1, mod, l, p, False)

        qna, kna, vna, qm, km, vm, qw, kw, vw, fv = _project(xs, mod, l, p, True, rope_tabs)
        km_c, vm_c = _mla_ctx(ckv_c, kr_c, p, l, nb_lat)
        o_a = _na_attention(qna, kna, vna, kna_c, vna_c, _na_bias_tables(rpb_na[l]), nb_lat, l)
        o_b = _mla_attention(qm, km, vm, km_c, vm_c, nb_lat)
        o_c = _win_attention(p['sink'], qw, kw, vw, kw_c, vw_c, nb_lat, l)
        o_d = _fourier(fv, dft, nb_lat)
        x1, h2 = _out_project((o_a, o_b, o_c, o_d), xs, mod, l, p, True)
        xs = _ffn(h2, x1, mod, l, p, True)

    def stack(idx, shape):
        return jnp.stack([caches[l][idx].reshape((nb_ctx, SEQ) + shape) for l in range(DEPTH)], axis=1)

    return (xp.reshape(nb_ctx, SEQ, D_MODEL), xs.reshape(nb_lat, DEC_SEQ, D_MODEL),
            stack(0, (8, HEAD_DIM)), stack(1, (8, HEAD_DIM)), stack(2, (MLA_KV_LORA,)), stack(3, (MLA_ROPE,)),
            stack(4, (2, HEAD_DIM)), stack(5, (2, HEAD_DIM)))
```

```python
import functools
import math

import numpy as np
import jax
import jax.numpy as jnp
from jax import lax
from jax.experimental import pallas as pl
from jax.experimental.pallas import tpu as pltpu

F32 = jnp.float32
BF16 = jnp.bfloat16

D_MODEL = 2048
DEPTH = 2
SEQ = 256
DEC_SEQ = 2048
PAST_LEN = 512
GRID_W = 64
HEAD_DIM = 64
GROUP_W = 512
MLA_HEADS = 4
MLA_NOPE = 128
MLA_ROPE = 64
MLA_QK = MLA_NOPE + MLA_ROPE
MLA_Q_LORA = 384
MLA_KV_LORA = 128
WINDOW = 128
FN_GROUPS = 4
FN_CH = 128
D_FF = 5632
MOD_CHUNKS = 6
ROPE_BASE = 10000.0
EPS = 1e-6
NEG = -1e30

LANES = 128
MOD_ROWS = 8
VMEM_LIMIT = 56 * 1024 * 1024

C_AQ, C_AK, C_AV = 0, 512, 1024
C_MQ = 1536
C_CKV = 1920
C_KR = 2048
IN_COLS = 3392
IN_COLS_P = 3456

TM_PROJ = 512
TM_OUT = 512
TM_FFN = 512
TF_FFN = 512
HALO = 16
TQ = 256
TQ_MLA = 512
NA_KH = 8
NA_KW = 16
NA_KROWS = 12
NA_MASKED = 15


def _cp(sem):
    return pltpu.CompilerParams(dimension_semantics=sem, vmem_limit_bytes=VMEM_LIMIT)


def _const_spec(shape, resident=False):
    n = len(shape)
    mode = dict(pipeline_mode=pl.Buffered(1)) if resident else {}
    return pl.BlockSpec(shape, lambda *_: (0,) * n, **mode)


def _layer_spec(shape, layer, resident=False):
    n = len(shape)
    mode = dict(pipeline_mode=pl.Buffered(1)) if resident else {}
    return pl.BlockSpec((None,) + tuple(shape), lambda *_: (layer,) + (0,) * n, **mode)


def _dot(a, b):
    return jnp.dot(a, b, preferred_element_type=F32)


def _dot_nt(a, b):
    return lax.dot_general(a, b, (((1,), (1,)), ((), ())), preferred_element_type=F32)


def _rms(t, n):
    return t * lax.rsqrt(jnp.sum(t * t, axis=-1, keepdims=True) * (1.0 / n) + EPS)


def _seg_rms64(t):
    rows, width = t.shape
    lo = lax.broadcasted_iota(jnp.int32, (rows, LANES), 1) < HEAD_DIM
    outs = []
    for c in range(width // LANES):
        tc = t[:, c * LANES:(c + 1) * LANES]
        sq = tc * tc
        s_lo = jnp.sum(jnp.where(lo, sq, 0.0), axis=-1, keepdims=True)
        s_hi = jnp.sum(jnp.where(lo, 0.0, sq), axis=-1, keepdims=True)
        ms = jnp.where(lo, s_lo, s_hi) * (1.0 / HEAD_DIM)
        outs.append(tc * lax.rsqrt(ms + EPS))
    return jnp.concatenate(outs, axis=-1)


def _rope128(t, cos, sin_signed):
    first = (lax.broadcasted_iota(jnp.int32, t.shape, 1) % 32) < 16
    partner = jnp.where(first, pltpu.roll(t, LANES - 16, 1), pltpu.roll(t, 16, 1))
    return t * cos + partner * sin_signed


def _rope_wide(t, cos, sin_signed):
    return jnp.concatenate(
        [_rope128(t[:, c * LANES:(c + 1) * LANES], cos, sin_signed) for c in range(t.shape[1] // LANES)], axis=-1)


MOD_TN = 512


def _mod_kernel(ct_ref, w_ref, b_ref, o_ref, xb_ref, *, nvec):
    tn = w_ref.shape[1]
    nct = tn // LANES

    @pl.when((pl.program_id(0) == 0) & (pl.program_id(1) == 0))
    def _():
        cv = ct_ref[...]
        sx = cv / (1.0 + jnp.exp(-cv))
        for v in range(nvec):
            xb_ref[v] = jnp.broadcast_to(sx[:, v:v + 1], (D_MODEL, LANES))

    def body(kc, accs):
        k0 = pl.multiple_of(kc * 8, 8)
        ws = [w_ref[pl.ds(k0, 8), t * LANES:(t + 1) * LANES] for t in range(nct)]
        new = []
        for v in range(nvec):
            xv = xb_ref[v, pl.ds(k0, 8), :]
            new.extend(accs[v * nct + t] + xv * ws[t] for t in range(nct))
        return tuple(new)

    zero = jnp.zeros((8, LANES), F32)
    accs = lax.fori_loop(0, w_ref.shape[0] // 8, body, (zero,) * (nvec * nct), unroll=2)
    rows = [jnp.concatenate([jnp.sum(accs[v * nct + t], axis=0, keepdims=True) for t in range(nct)], axis=1)
            for v in range(nvec)]
    rows.append(jnp.zeros((MOD_ROWS - nvec, tn), F32))
    o_ref[...] = jnp.concatenate(rows, axis=0) + b_ref[...]


def _modulation(cvecs, w_mod, b_mod):
    nvec = cvecs.shape[0]
    ct = jnp.zeros((D_MODEL, MOD_ROWS), F32).at[:, :nvec].set(cvecs.T)
    ncol = w_mod.shape[2]
    out = pl.pallas_call(
        functools.partial(_mod_kernel, nvec=nvec),
        grid=(DEPTH, ncol // MOD_TN),
        in_specs=[
            _const_spec((D_MODEL, MOD_ROWS)),
            pl.BlockSpec((None, D_MODEL, MOD_TN), lambda l, j: (l, 0, j)),
            pl.BlockSpec((None, 1, MOD_TN), lambda l, j: (l, 0, j)),
        ],
        out_specs=pl.BlockSpec((None, MOD_ROWS, MOD_TN), lambda l, j: (l, 0, j)),
        out_shape=jax.ShapeDtypeStruct((DEPTH, MOD_ROWS, ncol), F32),
        scratch_shapes=[pltpu.VMEM((nvec, D_MODEL, LANES), F32)],
        compiler_params=_cp(("arbitrary", "arbitrary")),
        name="modulation",
    )(ct, w_mod, b_mod.reshape(DEPTH, 1, ncol))
    return out.reshape(DEPTH * MOD_ROWS, MOD_CHUNKS, D_MODEL)


def _mla_kv_heads(ckv_n, kr, wkv_ref, gk_ref, rope):
    kv = _dot(ckv_n.astype(BF16), wkv_ref[...])
    ss_r = jnp.sum(kr * kr, axis=-1, keepdims=True)
    ks = []
    for h in range(MLA_HEADS):
        kn = kv[:, h * MLA_NOPE:(h + 1) * MLA_NOPE]
        ms = (jnp.sum(kn * kn, axis=-1, keepdims=True) + ss_r) * (1.0 / MLA_QK)
        r = lax.rsqrt(ms + EPS)
        tail = kr * r * gk_ref[:, h * 256 + 128:(h + 1) * 256]
        if rope is not None:
            tail = _rope128(tail, *rope)
        ks.append(kn * r * gk_ref[:, h * 256:h * 256 + 128])
        ks.append(tail)
    return jnp.concatenate(ks, axis=-1), kv[:, MLA_HEADS * MLA_NOPE:]


def _proj_kernel(*refs, latent):
    (x_ref, mod_ref, gmix_ref, win_ref, wq_ref, wkv_ref, gqna_ref, gkna_ref, gql_ref, gkvl_ref,
     gqm_ref, gkm_ref, gqw_ref, gkw_ref) = refs[:14]
    pos = 14
    rope = None
    if latent:
        rope = (refs[14][...], refs[15][...])
        pos = 16
    (qna_o, kna_o, vna_o, qm_o, km_o, vm_o, qw_o, kw_o, vw_o, fv_o) = refs[pos:pos + 10]
    cache_o = refs[pos + 10:]

    x = x_ref[...]
    rinv = lax.rsqrt(jnp.mean(x * x, axis=-1, keepdims=True) + EPS)
    h = (x * rinv) * (gmix_ref[...] * (1.0 + mod_ref[0, 1:2, :])) + mod_ref[0, 0:1, :]
    hb = h.astype(BF16)

    def proj(c0, c1):
        return _dot(hb, win_ref[:, c0:c1])

    qna_o[...] = (_seg_rms64(proj(C_AQ, C_AK)) * gqna_ref[...]).astype(BF16)
    k_na = _seg_rms64(proj(C_AK, C_AV)) * gkna_ref[...]
    kna_o[...] = k_na.astype(BF16)
    v_na = proj(C_AV, C_MQ)
    vna_o[...] = v_na.astype(BF16)

    cq = (_rms(proj(C_MQ, C_CKV), MLA_Q_LORA) * gql_ref[...]).astype(BF16)
    qu = _dot(cq, wq_ref[...])
    qs = []
    for hd in range(MLA_HEADS):
        blk = qu[:, hd * 256:(hd + 1) * 256]
        blk = blk * lax.rsqrt(jnp.sum(blk * blk, axis=-1, keepdims=True) * (1.0 / MLA_QK) + EPS)
        blk = blk * gqm_ref[:, hd * 256:(hd + 1) * 256]
        if latent:
            blk = jnp.concatenate([blk[:, :LANES], _rope128(blk[:, LANES:], *rope)], axis=-1)
        qs.append(blk)
    qm_o[...] = jnp.concatenate(qs, axis=-1).astype(BF16)
    ckv_n = _rms(proj(C_CKV, C_KR), MLA_KV_LORA) * gkvl_ref[...]

    tail = proj(C_KR, IN_COLS_P)
    lo = lax.broadcasted_iota(jnp.int32, (x.shape[0], LANES), 1) < HEAD_DIM
    nslab = (IN_COLS_P - C_KR) // LANES
    slabs = [tail[:, c * LANES:(c + 1) * LANES] for c in range(nslab)]
    swapped = [pltpu.roll(t, HEAD_DIM, 1) for t in slabs]
    al = [jnp.where(lo, swapped[c], swapped[c + 1]) for c in range(nslab - 1)]
    kr = jnp.where(lo, slabs[0], 0.0)
    k_m, v_m = _mla_kv_heads(ckv_n, kr, wkv_ref, gkm_ref, rope)
    km_o[...] = k_m.astype(BF16)
    vm_o[...] = v_m.astype(BF16)

    def dup(t):
        sw = pltpu.roll(t, HEAD_DIM, 1)
        return jnp.concatenate([jnp.where(lo, t, sw), jnp.where(lo, sw, t)], axis=-1)

    q_w = _seg_rms64(jnp.concatenate(al[0:4], axis=-1)) * gqw_ref[...]
    k_w = _seg_rms64(al[4]) * gkw_ref[...]
    v_w = al[5]
    if latent:
        q_w = _rope_wide(q_w, *rope)
        k_w = _rope128(k_w, *rope)
    qw_o[...] = q_w.astype(BF16)
    kw_o[...] = dup(k_w).astype(BF16)
    vw_o[...] = dup(v_w).astype(BF16)

    fv_o[...] = jnp.concatenate(al[6:10], axis=-1).astype(BF16)

    if not latent:
        kna32_o, vna32_o, ckv32_o, kr32_o, kw32_o, vw32_o = cache_o
        kna32_o[...] = k_na
        vna32_o[...] = v_na
        ckv32_o[...] = ckv_n
        kr32_o[...] = kr[:, :MLA_ROPE]
        kw32_o[...] = k_w
        vw32_o[...] = v_w


def _project(x, mod, layer, p, latent, rope_tabs):
    n = x.shape[0]
    tm = TM_PROJ
    row = lambda i: (i, 0)
    if latent:
        mod_map = lambda i: (layer * MOD_ROWS + 1 + (i * tm) // DEC_SEQ, 0, 0)
    else:
        mod_map = lambda i: (layer * MOD_ROWS, 0, 0)
    in_specs = [
        pl.BlockSpec((tm, D_MODEL), row),
        pl.BlockSpec((1, MOD_CHUNKS, D_MODEL), mod_map),
        _layer_spec((1, D_MODEL), layer),
        _layer_spec((D_MODEL, IN_COLS_P), layer, resident=True),
        _layer_spec((MLA_Q_LORA, 1024), layer, resident=True),
        _layer_spec((MLA_KV_LORA, 1024), layer, resident=True),
    ] + [_layer_spec((1, wd), layer) for wd in (512, 512, MLA_Q_LORA, MLA_KV_LORA, 1024, 1024, 512, 128)]
    args = [x, mod, p['g_mix'], p['w_in'], p['w_q_up'], p['w_kv_up'], p['gq_na'], p['gk_na'], p['g_q_lora'],
            p['g_kv_lora'], p['gq_mla'], p['gk_mla'], p['gq_win'], p['gk_win']]
    if latent:
        nt = DEC_SEQ // tm
        in_specs += [pl.BlockSpec((tm, LANES), lambda i: (i % nt, 0))] * 2
        args += list(rope_tabs)
    widths = [512, 512, 512, 1024, 1024, 512, 512, 256, 256, 512]
    out_specs = [pl.BlockSpec((tm, w), row) for w in widths]
    out_shape = [jax.ShapeDtypeStruct((n, w), BF16) for w in widths]
    if not latent:
        cache_w = [512, 512, MLA_KV_LORA, MLA_ROPE, 128, 128]
        out_specs += [pl.BlockSpec((tm, w), row) for w in cache_w]
        out_shape += [jax.ShapeDtypeStruct((n, w), F32) for w in cache_w]
    return pl.pallas_call(
        functools.partial(_proj_kernel, latent=latent),
        grid=(n // tm,),
        in_specs=in_specs,
        out_specs=out_specs,
        out_shape=out_shape,
        compiler_params=_cp(("arbitrary",)),
        name="proj_latent" if latent else "proj_context",
    )(*args)


def _mla_ctx_kernel(ckv_ref, kr_ref, wkv_ref, gk_ref, k_o, v_o):
    k_m, v_m = _mla_kv_heads(ckv_ref[...], kr_ref[...], wkv_ref, gk_ref, None)
    k_o[...] = k_m.astype(BF16)
    v_o[...] = v_m.astype(BF16)


def _mla_ctx(ckv, kr128, p, layer, nb):
    n = nb * PAST_LEN
    src = lambda b: (b * DEPTH + layer, 0)
    row = lambda b: (b, 0)
    return pl.pallas_call(
        _mla_ctx_kernel,
        grid=(nb,),
        in_specs=[pl.BlockSpec((PAST_LEN, MLA_KV_LORA), src), pl.BlockSpec((PAST_LEN, LANES), src),
                  _layer_spec((MLA_KV_LORA, 1024), layer), _layer_spec((1, 1024), layer)],
        out_specs=[pl.BlockSpec((PAST_LEN, 1024), row), pl.BlockSpec((PAST_LEN, 512), row)],
        out_shape=[jax.ShapeDtypeStruct((n, 1024), BF16), jax.ShapeDtypeStruct((n, 512), BF16)],
        compiler_params=_cp(("arbitrary",)),
        name="mla_cached_kv",
    )(ckv, kr128, p['w_kv_up'], p['gk_mla'])


def _softmax_pv(scores, values, sink=None):
    m = functools.reduce(jnp.maximum, [jnp.max(s, axis=-1, keepdims=True) for s in scores])
    if sink is not None:
        m = jnp.maximum(m, sink)
    ps = [jnp.exp(s - m) for s in scores]
    l = functools.reduce(lambda a, b: a + b, [jnp.sum(p, axis=-1, keepdims=True) for p in ps])
    if sink is not None:
        l = l + jnp.exp(sink - m)
    o = functools.reduce(lambda a, b: a + b, [_dot(p.astype(BF16), v) for p, v in zip(ps, values)])
    return o / l


def _half_masks():
    lane = lax.broadcasted_iota(jnp.int32, (1, LANES), 1)
    lo = lane < HEAD_DIM
    return jnp.where(lo, 1.0, 0.0).astype(BF16), jnp.where(lo, 0.0, 1.0).astype(BF16)


def _pair_attention(q2, keys, values, extra=None, sinks=None):
    rows = q2.shape[0]
    lo = lax.broadcasted_iota(jnp.int32, (rows, LANES), 1) < HEAD_DIM
    outs = []
    for e, msk in enumerate(_half_masks()):
        qm = q2 * msk
        scores = []
        for idx, k2 in enumerate(keys):
            s = _dot_nt(qm, k2)
            if extra is not None:
                s = extra(e, idx, s)
            scores.append(s)
        sink = None
        if sinks is not None:
            sink = jnp.full((rows, 1), sinks[e], F32)
        outs.append(_softmax_pv(scores, values, sink))
    return jnp.where(lo, outs[0], outs[1])


def _dft_real(x, cc_ref, sc_ref, cs, ss, scale):
    outs = []
    for g in range(FN_GROUPS):
        xg = x[:, g * FN_CH:(g + 1) * FN_CH]
        xc = _dot(xg, cc_ref[...]).astype(BF16)
        xs = _dot(xg, sc_ref[...]).astype(BF16)
        outs.append(_dot(cs, xc) - _dot(ss, xs))
    return jnp.concatenate(outs, axis=-1) * scale


def _ctx_attn_kernel(sink_ref, qna, kna, vna, qm, km, vm, qw, kw, vw, fv, cs_ref, ss_ref, cc_ref, sc_ref,
                     oa, ob, oc, od, *, layer):
    for hp in range(4):
        cs = slice(hp * LANES, (hp + 1) * LANES)
        oa[:, cs] = _pair_attention(qna[:, cs], [kna[:, cs]], [vna[:, cs]]).astype(BF16)
    for hd in range(MLA_HEADS):
        s = _dot_nt(qm[:, hd * 256:(hd + 1) * 256], km[:, hd * 256:(hd + 1) * 256])
        ob[:, hd * 128:(hd + 1) * 128] = _softmax_pv([s], [vm[:, hd * 128:(hd + 1) * 128]]).astype(BF16)
    for hp in range(4):
        cs = slice(hp * LANES, (hp + 1) * LANES)
        kcs = slice((hp // 2) * LANES, (hp // 2 + 1) * LANES)
        oc[:, cs] = _pair_attention(qw[:, cs], [kw[:, kcs]], [vw[:, kcs]],
                                    sinks=(sink_ref[layer, 2 * hp], sink_ref[layer, 2 * hp + 1])).astype(BF16)
    od[...] = _dft_real(fv[...], cc_ref, sc_ref, cs_ref[...], ss_ref[...],
                        1.0 / math.sqrt(SEQ * FN_CH)).astype(BF16)


def _ctx_attention(pr, sink, dft, layer):
    qna, kna, vna, qm, km, vm, qw, kw, vw, fv = pr[:10]
    n = qna.shape[0]
    row = lambda b: (b, 0)
    widths = [512, 512, 512, 1024, 1024, 512, 512, 256, 256, 512]
    in_specs = [pl.BlockSpec(memory_space=pltpu.SMEM)]
    in_specs += [pl.BlockSpec((SEQ, w), row) for w in widths]
    in_specs += [_const_spec((SEQ, SEQ)), _const_spec((SEQ, SEQ)), _const_spec((FN_CH, FN_CH)),
                 _const_spec((FN_CH, FN_CH))]
    return pl.pallas_call(
        functools.partial(_ctx_attn_kernel, layer=layer),
        grid=(n // SEQ,),
        in_specs=in_specs,
        out_specs=[pl.BlockSpec((SEQ, 512), row)] * 4,
        out_shape=[jax.ShapeDtypeStruct((n, 512), BF16)] * 4,
        compiler_params=_cp(("arbitrary",)),
        name="context_attention",
    )(sink, qna, kna, vna, qm, km, vm, qw, kw, vw, fv, dft['cs_ctx'], dft['ss_ctx'], dft['cc'], dft['sc'])


def _na_kernel(q_ref, k_ref, v_ref, kc_ref, vc_ref, tl_ref, tr_ref, o_ref):
    qt = pl.program_id(1)
    rows = DEC_SEQ // GRID_W
    qrows = TQ // GRID_W
    kstart = jnp.clip(qt * qrows - NA_KH // 2, 0, rows - NA_KROWS)
    ks = pl.multiple_of(kstart * GRID_W, GRID_W)
    nk = NA_KROWS * GRID_W

    blk_idx = []
    for i in range(qrows):
        r = qt * qrows + i
        rs = jnp.clip(r - NA_KH // 2, 0, rows - NA_KH)
        blk_idx.append([jnp.where((kstart + j >= rs) & (kstart + j < rs + NA_KH), kstart + j - r + NA_KH - 1,
                                  NA_MASKED) for j in range(NA_KROWS)])

    def bias(h):
        return jnp.concatenate([
            jnp.concatenate([tl_ref[h, blk_idx[i][2 * jp]] + tr_ref[h, blk_idx[i][2 * jp + 1]]
                             for jp in range(NA_KROWS // 2)], axis=-1)
            for i in range(qrows)], axis=0)

    for hp in range(4):
        cs = slice(hp * LANES, (hp + 1) * LANES)

        def extra(e, idx, s, hp=hp):
            return s + bias(2 * hp + e) if idx == 0 else s

        o = _pair_attention(q_ref[:, cs], [k_ref[pl.ds(ks, nk), cs], kc_ref[:, cs]],
                            [v_ref[pl.ds(ks, nk), cs], vc_ref[:, cs]], extra=extra)
        o_ref[:, cs] = o.astype(BF16)


def _lat_specs(width_q, width_k, width_v, ctx_map, tq=TQ):
    nq = DEC_SEQ // tq
    ctx_nd = len(ctx_map(0, 0))
    lead = (None,) * (ctx_nd - 2)
    return [
        pl.BlockSpec((tq, width_q), lambda b, t: (b * nq + t, 0)),
        pl.BlockSpec((DEC_SEQ, width_k), lambda b, t: (b, 0)),
        pl.BlockSpec((DEC_SEQ, width_v), lambda b, t: (b, 0)),
        pl.BlockSpec(lead + (PAST_LEN, width_k), ctx_map),
        pl.BlockSpec(lead + (PAST_LEN, width_v), ctx_map),
    ]


def _lat_call(kernel, name, args, in_specs, nb, tq=TQ):
    nq = DEC_SEQ // tq
    return pl.pallas_call(
        kernel,
        grid=(nb, nq),
        in_specs=in_specs,
        out_specs=pl.BlockSpec((tq, 512), lambda b, t: (b * nq + t, 0)),
        out_shape=jax.ShapeDtypeStruct((nb * DEC_SEQ, 512), BF16),
        compiler_params=_cp(("arbitrary", "arbitrary")),
        name=name,
    )(*args)


def _na_attention(q, k, v, kc, vc, bias_tabs, nb, layer):
    specs = _lat_specs(512, 512, 512, lambda b, t: (b, layer, 0, 0))
    specs += [_const_spec(bias_tabs[0].shape, resident=True)] * 2
    return _lat_call(_na_kernel, "neighbourhood_attention", (q, k, v, kc, vc) + tuple(bias_tabs), specs, nb)


def _mla_kernel(q_ref, k_ref, v_ref, kc_ref, vc_ref, o_ref):
    for hd in range(MLA_HEADS):
        ks = slice(hd * 256, (hd + 1) * 256)
        vs = slice(hd * 128, (hd + 1) * 128)
        q = q_ref[:, ks]
        scores = [_dot_nt(q, k_ref[:, ks]), _dot_nt(q, kc_ref[:, ks])]
        o_ref[:, vs] = _softmax_pv(scores, [v_ref[:, vs], vc_ref[:, vs]]).astype(BF16)


def _mla_attention(q, k, v, kc, vc, nb):
    specs = _lat_specs(1024, 1024, 512, lambda b, t: (b, 0), tq=TQ_MLA)
    return _lat_call(_mla_kernel, "latent_attention", (q, k, v, kc, vc), specs, nb, tq=TQ_MLA)


def _win_kernel(sink_ref, q_ref, k_ref, v_ref, kc_ref, vc_ref, o_ref, *, layer):
    qt = pl.program_id(1)
    nk = 4 * WINDOW
    ks = pl.multiple_of(jnp.clip(qt * TQ - WINDOW, 0, DEC_SEQ - nk), WINDOW)
    qpos = qt * TQ + lax.broadcasted_iota(jnp.int32, (TQ, nk), 0)
    kpos = ks + lax.broadcasted_iota(jnp.int32, (TQ, nk), 1)
    ok = jnp.abs(kpos - qpos) <= WINDOW

    def extra(e, idx, s):
        return jnp.where(ok, s, NEG) if idx == 0 else s

    for hp in range(4):
        cs = slice(hp * LANES, (hp + 1) * LANES)
        kcs = slice((hp // 2) * LANES, (hp // 2 + 1) * LANES)
        o = _pair_attention(q_ref[:, cs], [k_ref[pl.ds(ks, nk), kcs], kc_ref[:, kcs]],
                            [v_ref[pl.ds(ks, nk), kcs], vc_ref[:, kcs]], extra=extra,
                            sinks=(sink_ref[layer, 2 * hp], sink_ref[layer, 2 * hp + 1]))
        o_ref[:, cs] = o.astype(BF16)


def _win_attention(sink, q, k, v, kc, vc, nb, layer):
    specs = [pl.BlockSpec(memory_space=pltpu.SMEM)]
    specs += _lat_specs(512, 256, 256, lambda b, t: (b, layer, 0, 0))
    return _lat_call(functools.partial(_win_kernel, layer=layer), "window_attention", (sink, q, k, v, kc, vc),
                     specs, nb)


def _fourier_kernel(x_ref, cc_ref, sc_ref, cs_ref, ss_ref, o_ref, xc_ref, xs_ref):
    @pl.when(pl.program_id(1) == 0)
    def _():
        for g in range(FN_GROUPS):
            gs = slice(g * FN_CH, (g + 1) * FN_CH)
            xg = x_ref[:, gs]
            xc_ref[:, gs] = _dot(xg, cc_ref[...]).astype(BF16)
            xs_ref[:, gs] = _dot(xg, sc_ref[...]).astype(BF16)

    y = _dot(cs_ref[...], xc_ref[...]) - _dot(ss_ref[...], xs_ref[...])
    o_ref[...] = (y * (1.0 / math.sqrt(DEC_SEQ * FN_CH))).astype(BF16)


def _fourier(x, dft, nb):
    nq = DEC_SEQ // TQ
    return pl.pallas_call(
        _fourier_kernel,
        grid=(nb, nq),
        in_specs=[
            pl.BlockSpec((DEC_SEQ, 512), lambda b, t: (b, 0)),
            _const_spec((FN_CH, FN_CH)), _const_spec((FN_CH, FN_CH)),
            pl.BlockSpec((TQ, DEC_SEQ), lambda b, t: (t, 0)),
            pl.BlockSpec((TQ, DEC_SEQ), lambda b, t: (t, 0)),
        ],
        out_specs=pl.BlockSpec((TQ, 512), lambda b, t: (b * nq + t, 0)),
        out_shape=jax.ShapeDtypeStruct((nb * DEC_SEQ, 512), BF16),
        scratch_shapes=[pltpu.VMEM((DEC_SEQ, 512), BF16), pltpu.VMEM((DEC_SEQ, 512), BF16)],
        compiler_params=_cp(("arbitrary", "arbitrary")),
        name="fourier_mix",
    )(x, dft['cc'], dft['sc'], dft['cs_lat'], dft['ss_lat'])


def _out_kernel(oa, ob, oc, od, x_ref, mod_ref, gffn_ref, w_ref, x1_o, h2_o):
    o = jnp.concatenate([oa[...], ob[...], oc[...], od[...]], axis=-1)
    acc = _dot(o, w_ref[...])
    x1 = x_ref[...] + mod_ref[0, 2:3, :] * acc
    x1_o[...] = x1
    rinv = lax.rsqrt(jnp.mean(x1 * x1, axis=-1, keepdims=True) + EPS)
    h2 = (x1 * rinv) * (gffn_ref[...] * (1.0 + mod_ref[0, 4:5, :])) + mod_ref[0, 3:4, :]
    h2_o[...] = h2.astype(BF16)


def _mod_map(layer, tm, latent):
    if latent:
        return lambda i, *_: (layer * MOD_ROWS + 1 + (i * tm) // DEC_SEQ, 0, 0)
    return lambda i, *_: (layer * MOD_ROWS, 0, 0)


def _out_project(o4, x, mod, layer, p, latent):
    n = x.shape[0]
    tm = TM_OUT
    row = lambda i: (i, 0)
    return pl.pallas_call(
        _out_kernel,
        grid=(n // tm,),
        in_specs=[pl.BlockSpec((tm, 512), row)] * 4 + [
            pl.BlockSpec((tm, D_MODEL), row),
            pl.BlockSpec((1, MOD_CHUNKS, D_MODEL), _mod_map(layer, tm, latent)),
            _layer_spec((1, D_MODEL), layer),
            _layer_spec((D_MODEL, D_MODEL), layer, resident=True),
        ],
        out_specs=[pl.BlockSpec((tm, D_MODEL), row), pl.BlockSpec((tm, D_MODEL), row)],
        out_shape=[jax.ShapeDtypeStruct((n, D_MODEL), F32), jax.ShapeDtypeStruct((n, D_MODEL), BF16)],
        compiler_params=_cp(("arbitrary",)),
        name="out_proj_latent" if latent else "out_proj_context",
    )(*o4, x, mod, p['g_ffn'], p['w_out'])


def _ffn_kernel(h_ref, hp_ref, hn_ref, x1_ref, mod_ref, wg_ref, wu_ref, wc_ref, wd_ref, o_ref, hext_ref, acc_ref,
                *, seq_len):
    i = pl.program_id(0)
    j = pl.program_id(1)
    tm = h_ref.shape[0]

    @pl.when(j == 0)
    def _():
        hext_ref[0:HALO, :] = hp_ref[...]
        hext_ref[HALO:HALO + tm, :] = h_ref[...]
        hext_ref[HALO + tm:, :] = hn_ref[...]
        acc_ref[...] = jnp.zeros_like(acc_ref)

    g_ext = _dot(hext_ref[...], wg_ref[...])
    rows_ext = g_ext.shape[0]
    pos = (i * tm + lax.broadcasted_iota(jnp.int32, (tm, 1), 0)) % seq_len
    g_mid = g_ext[HALO:HALO + tm]
    g_prev = pltpu.roll(g_ext, 1, 0)[HALO:HALO + tm]
    g_next = pltpu.roll(g_ext, rows_ext - 1, 0)[HALO:HALO + tm]
    g_prev = jnp.where(pos == 0, 0.0, g_prev)
    g_next = jnp.where(pos == seq_len - 1, 0.0, g_next)
    g = g_prev * wc_ref[0:1, :] + g_mid * wc_ref[1:2, :] + g_next * wc_ref[2:3, :]
    u = _dot(h_ref[...], wu_ref[...])
    act = (g / (1.0 + jnp.exp(-g))) * u
    acc_ref[...] += _dot(act.astype(BF16), wd_ref[...])

    @pl.when(j == pl.num_programs(1) - 1)
    def _():
        o_ref[...] = x1_ref[...] + mod_ref[0, 5:6, :] * acc_ref[...]


def _ffn(h2, x1, mod, layer, p, latent):
    n = x1.shape[0]
    tm, tf = TM_FFN, TF_FFN
    seq_len = DEC_SEQ if latent else SEQ
    assert n % tm == 0 and tm % HALO == 0 and D_FF % tf == 0
    hb = tm // HALO
    nhalo = n // HALO
    return pl.pallas_call(
        functools.partial(_ffn_kernel, seq_len=seq_len),
        grid=(n // tm, D_FF // tf),
        in_specs=[
            pl.BlockSpec((tm, D_MODEL), lambda i, j: (i, 0)),
            pl.BlockSpec((HALO, D_MODEL), lambda i, j: (jnp.maximum(i * hb - 1, 0), 0)),
            pl.BlockSpec((HALO, D_MODEL), lambda i, j: (jnp.minimum((i + 1) * hb, nhalo - 1), 0)),
            pl.BlockSpec((tm, D_MODEL), lambda i, j: (i, 0)),
            pl.BlockSpec((1, MOD_CHUNKS, D_MODEL), _mod_map(layer, tm, latent)),
            pl.BlockSpec((None, D_MODEL, tf), lambda i, j: (layer, 0, j)),
            pl.BlockSpec((None, D_MODEL, tf), lambda i, j: (layer, 0, j)),
            pl.BlockSpec((None, 8, tf), lambda i, j: (layer, 0, j)),
            pl.BlockSpec((None, tf, D_MODEL), lambda i, j: (layer, j, 0)),
        ],
        out_specs=pl.BlockSpec((tm, D_MODEL), lambda i, j: (i, 0)),
        out_shape=jax.ShapeDtypeStruct((n, D_MODEL), F32),
        scratch_shapes=[pltpu.VMEM((tm + 2 * HALO, D_MODEL), BF16), pltpu.VMEM((tm, D_MODEL), F32)],
        compiler_params=_cp(("arbitrary", "arbitrary")),
        name="ffn_latent" if latent else "ffn_context",
    )(h2, h2, h2, x1, mod, p['w_gate'], p['w_up'], p['w_conv'], p['w_down'])


def _tile_row(g, reps, scale=1.0):
    return (jnp.tile(g, (1, reps)) * scale)[:, None, :]


def _prep(w):
    win_r = jnp.pad(w['w_in'].astype(BF16), ((0, 0), (0, 0), (0, IN_COLS_P - IN_COLS)))

    wq = w['w_q_up'].reshape(DEPTH, MLA_Q_LORA, MLA_HEADS, MLA_QK)
    wq_r = jnp.concatenate([wq, jnp.zeros((DEPTH, MLA_Q_LORA, MLA_HEADS, 256 - MLA_QK), F32)], axis=-1)
    wq_r = wq_r.reshape(DEPTH, MLA_Q_LORA, MLA_HEADS * 256).astype(BF16)
    wkv = w['w_kv_up'].reshape(DEPTH, MLA_KV_LORA, MLA_HEADS, 2 * MLA_NOPE)
    wkv_r = jnp.concatenate([wkv[..., :MLA_NOPE].reshape(DEPTH, MLA_KV_LORA, -1),
                             wkv[..., MLA_NOPE:].reshape(DEPTH, MLA_KV_LORA, -1)], axis=-1).astype(BF16)

    def pad_head(g, s):
        return _tile_row(jnp.concatenate([g, jnp.zeros((DEPTH, 256 - MLA_QK), F32)], axis=-1), MLA_HEADS, s)

    wc = jnp.concatenate([w['w_conv'], jnp.zeros((DEPTH, 5, D_FF), F32)], axis=1)
    return dict(
        g_mix=w['g_mix'][:, None, :], g_ffn=w['g_ffn'][:, None, :],
        w_in=win_r, w_q_up=wq_r, w_kv_up=wkv_r,
        gq_na=_tile_row(w['g_qn_na'], 8, HEAD_DIM ** -0.5), gk_na=_tile_row(w['g_kn_na'], 8),
        g_q_lora=w['g_q_lora'][:, None, :], g_kv_lora=w['g_kv_lora'][:, None, :],
        gq_mla=pad_head(w['g_qn_mla'], MLA_QK ** -0.5), gk_mla=pad_head(w['g_kn_mla'], 1.0),
        gq_win=_tile_row(w['g_qn_win'], 8, HEAD_DIM ** -0.5), gk_win=_tile_row(w['g_kn_win'], 2),
        sink=w['sink_win'],
        w_out=w['w_out'].astype(BF16), w_gate=w['w_gate'].astype(BF16), w_up=w['w_up'].astype(BF16),
        w_conv=wc, w_down=w['w_down'].astype(BF16),
    )


def _na_bias_tables(rpb):
    nh = rpb.shape[0]
    ext = jnp.concatenate([jnp.repeat(rpb[..., :1], 48, axis=-1), rpb, jnp.repeat(rpb[..., -1:], 48, axis=-1)], axis=-1)
    blk = jnp.stack([ext[..., 63 - c:127 - c] for c in range(GRID_W)], axis=2)
    c = np.arange(GRID_W)
    cs = np.clip(c - NA_KW // 2, 0, GRID_W - NA_KW)
    col_ok = (c[None, :] >= cs[:, None]) & (c[None, :] < cs[:, None] + NA_KW)
    blk = jnp.where(jnp.asarray(col_ok)[None, None], blk, NEG)
    blk = jnp.concatenate([blk, jnp.full((nh, 1, GRID_W, GRID_W), NEG, F32)], axis=1)
    zero = jnp.zeros_like(blk)
    return jnp.concatenate([blk, zero], axis=-1), jnp.concatenate([zero, blk], axis=-1)


@functools.lru_cache(maxsize=None)
def _dft_tables():
    def cs(n):
        k = (np.arange(n)[:, None] * np.arange(n)[None, :]) % n
        ang = 2.0 * np.pi * k.astype(np.float64) / n
        return np.cos(ang), np.sin(ang)

    out = {}
    for name, n in (('ctx', SEQ), ('lat', DEC_SEQ)):
        c, s = cs(n)
        out['cs_' + name], out['ss_' + name] = c, s
    out['cc'], out['sc'] = cs(FN_CH)
    return out


@functools.lru_cache(maxsize=None)
def _rope_tables():
    t = np.arange(DEC_SEQ)
    quarter = MLA_ROPE // 4
    inv = ROPE_BASE ** (-np.arange(quarter, dtype=np.float64) / quarter)
    j = np.arange(MLA_ROPE)
    pos = np.where(j[None, :] < MLA_ROPE // 2, (t // GRID_W)[:, None], (t % GRID_W)[:, None]).astype(np.float64)
    ang = pos * inv[j % quarter][None, :]
    sign = np.where((j % 32) < 16, -1.0, 1.0)
    cos = np.cos(ang)
    sin = np.sin(ang) * sign[None, :]
    return np.tile(cos, (1, 2)).astype(np.float32), np.tile(sin, (1, 2)).astype(np.float32)


def kernel(x_prompt, x_sample, cache_na_k, cache_na_v, cache_mla_ckv, cache_mla_krope, cache_win_k, cache_win_v,
           c, c_ctx, w_mod, b_mod, g_mix, g_ffn, w_in, g_qn_na, g_kn_na, rpb_na, g_q_lora, w_q_up, g_kv_lora,
           w_kv_up, g_qn_mla, g_kn_mla, g_qn_win, g_kn_win, sink_win, w_out, w_gate, w_up, w_conv, w_down):
    w = dict(g_mix=g_mix, g_ffn=g_ffn, w_in=w_in, g_qn_na=g_qn_na, g_kn_na=g_kn_na, g_q_lora=g_q_lora,
             w_q_up=w_q_up, g_kv_lora=g_kv_lora, w_kv_up=w_kv_up, g_qn_mla=g_qn_mla, g_kn_mla=g_kn_mla,
             g_qn_win=g_qn_win, g_kn_win=g_kn_win, sink_win=sink_win, w_out=w_out, w_gate=w_gate, w_up=w_up,
             w_conv=w_conv, w_down=w_down)
    nb_ctx, nb_lat = x_prompt.shape[0], x_sample.shape[0]
    dft = {k: jnp.asarray(v, F32).astype(BF16) for k, v in _dft_tables().items()}
    rope_tabs = tuple(jnp.asarray(t) for t in _rope_tables())

    mod = _modulation(jnp.concatenate([c_ctx[None, :], c], axis=0), w_mod, b_mod)

    xp = x_prompt.reshape(nb_ctx * SEQ, D_MODEL)
    xs = x_sample.reshape(nb_lat * DEC_SEQ, D_MODEL)
    caches = []
    p = _prep(w)
    nctx = nb_lat * DEPTH * PAST_LEN
    dup = lambda t: jnp.concatenate([t[..., :64], t[..., :64], t[..., 64:], t[..., 64:]], axis=-1).astype(BF16)
    kna_c = cache_na_k.reshape(nb_lat, DEPTH, PAST_LEN, 512).astype(BF16)
    vna_c = cache_na_v.reshape(nb_lat, DEPTH, PAST_LEN, 512).astype(BF16)
    kw_c = dup(cache_win_k.reshape(nb_lat, DEPTH, PAST_LEN, 128))
    vw_c = dup(cache_win_v.reshape(nb_lat, DEPTH, PAST_LEN, 128))
    ckv_c = cache_mla_ckv.reshape(nctx, MLA_KV_LORA)
    kr_c = jnp.pad(cache_mla_krope.reshape(nctx, MLA_ROPE), ((0, 0), (0, LANES - MLA_ROPE)))
    for l in range(DEPTH):

        pr = _project(xp, mod, l, p, False, None)
        caches.append(pr[10:])
        o4 = _ctx_attention(pr, p['sink'], dft, l)
        x1, h2 = _out_project(o4, xp, mod, l, p, False)
        xp = _ffn(h2, x1, mod, l, p, False)

        qna, kna, vna, qm, km, vm, qw, kw, vw, fv = _project(xs, mod, l, p, True, rope_tabs)
        km_c, vm_c = _mla_ctx(ckv_c, kr_c, p, l, nb_lat)
        o_a = _na_attention(qna, kna, vna, kna_c, vna_c, _na_bias_tables(rpb_na[l]), nb_lat, l)
        o_b = _mla_attention(qm, km, vm, km_c, vm_c, nb_lat)
        o_c = _win_attention(p['sink'], qw, kw, vw, kw_c, vw_c, nb_lat, l)
        o_d = _fourier(fv, dft, nb_lat)
        x1, h2 = _out_project((o_a, o_b, o_c, o_d), xs, mod, l, p, True)
        xs = _ffn(h2, x1, mod, l, p, True)

    def stack(idx, shape):
        return jnp.stack([caches[l][idx].reshape((nb_ctx, SEQ) + shape) for l in range(DEPTH)], axis=1)

    return (xp.reshape(nb_ctx, SEQ, D_MODEL), xs.reshape(nb_lat, DEC_SEQ, D_MODEL),
            stack(0, (8, HEAD_DIM)), stack(1, (8, HEAD_DIM)), stack(2, (MLA_KV_LORA,)), stack(3, (MLA_ROPE,)),
            stack(4, (2, HEAD_DIM)), stack(5, (2, HEAD_DIM)))
```

```python
import functools
import math

import numpy as np
import jax
import jax.numpy as jnp
from jax import lax
from jax.experimental import pallas as pl
from jax.experimental.pallas import tpu as pltpu

F32 = jnp.float32
BF16 = jnp.bfloat16

D_MODEL = 2048
DEPTH = 2
SEQ = 256
DEC_SEQ = 2048
PAST_LEN = 512
GRID_W = 64
HEAD_DIM = 64
GROUP_W = 512
MLA_HEADS = 4
MLA_NOPE = 128
MLA_ROPE = 64
MLA_QK = MLA_NOPE + MLA_ROPE
MLA_Q_LORA = 384
MLA_KV_LORA = 128
WINDOW = 128
FN_GROUPS = 4
FN_CH = 128
D_FF = 5632
MOD_CHUNKS = 6
ROPE_BASE = 10000.0
EPS = 1e-6
NEG = -1e30
LOG2E = math.log2(math.e)

LANES = 128
MOD_ROWS = 8
VMEM_LIMIT = 56 * 1024 * 1024

C_AQ, C_AK, C_AV = 0, 512, 1024
C_MQ = 1536
C_CKV = 1920
C_KR = 2048
IN_COLS = 3392
IN_COLS_P = 3456

TM_PROJ = 512
TM_OUT = 512
TM_FFN = 512
TF_FFN = 512
HALO = 16
TQ = 256
TQ_MLA = 512
NA_KH = 8
NA_KW = 16
NA_KROWS = 12
NA_MASKED = 15


def _cp(sem):
    return pltpu.CompilerParams(dimension_semantics=sem, vmem_limit_bytes=VMEM_LIMIT)


def _const_spec(shape, resident=False):
    n = len(shape)
    mode = dict(pipeline_mode=pl.Buffered(1)) if resident else {}
    return pl.BlockSpec(shape, lambda *_: (0,) * n, **mode)


def _layer_spec(shape, layer, resident=False):
    n = len(shape)
    mode = dict(pipeline_mode=pl.Buffered(1)) if resident else {}
    return pl.BlockSpec((None,) + tuple(shape), lambda *_: (layer,) + (0,) * n, **mode)


def _dot(a, b):
    return jnp.dot(a, b, preferred_element_type=F32)


def _dot_nt(a, b):
    return lax.dot_general(a, b, (((1,), (1,)), ((), ())), preferred_element_type=F32)


def _rms(t, n):
    return t * lax.rsqrt(jnp.sum(t * t, axis=-1, keepdims=True) * (1.0 / n) + EPS)


def _seg_rms64(t):
    rows, width = t.shape
    lo = lax.broadcasted_iota(jnp.int32, (rows, LANES), 1) < HEAD_DIM
    outs = []
    for c in range(width // LANES):
        tc = t[:, c * LANES:(c + 1) * LANES]
        sq = tc * tc
        s_lo = jnp.sum(jnp.where(lo, sq, 0.0), axis=-1, keepdims=True)
        s_hi = jnp.sum(jnp.where(lo, 0.0, sq), axis=-1, keepdims=True)
        ms = jnp.where(lo, s_lo, s_hi) * (1.0 / HEAD_DIM)
        outs.append(tc * lax.rsqrt(ms + EPS))
    return jnp.concatenate(outs, axis=-1)


def _rope128(t, cos, sin_signed):
    first = (lax.broadcasted_iota(jnp.int32, t.shape, 1) % 32) < 16
    partner = jnp.where(first, pltpu.roll(t, LANES - 16, 1), pltpu.roll(t, 16, 1))
    return t * cos + partner * sin_signed


def _rope_wide(t, cos, sin_signed):
    return jnp.concatenate(
        [_rope128(t[:, c * LANES:(c + 1) * LANES], cos, sin_signed) for c in range(t.shape[1] // LANES)], axis=-1)


MOD_TN = 512


def _mod_kernel(ct_ref, w_ref, b_ref, o_ref, xb_ref, *, nvec):
    tn = w_ref.shape[1]
    nct = tn // LANES

    @pl.when((pl.program_id(0) == 0) & (pl.program_id(1) == 0))
    def _():
        cv = ct_ref[...]
        sx = cv / (1.0 + jnp.exp(-cv))
        for v in range(nvec):
            xb_ref[v] = jnp.broadcast_to(sx[:, v:v + 1], (D_MODEL, LANES))

    def body(kc, accs):
        k0 = pl.multiple_of(kc * 8, 8)
        ws = [w_ref[pl.ds(k0, 8), t * LANES:(t + 1) * LANES] for t in range(nct)]
        new = []
        for v in range(nvec):
            xv = xb_ref[v, pl.ds(k0, 8), :]
            new.extend(accs[v * nct + t] + xv * ws[t] for t in range(nct))
        return tuple(new)

    zero = jnp.zeros((8, LANES), F32)
    accs = lax.fori_loop(0, w_ref.shape[0] // 8, body, (zero,) * (nvec * nct), unroll=4)
    rows = [jnp.concatenate([jnp.sum(accs[v * nct + t], axis=0, keepdims=True) for t in range(nct)], axis=1)
            for v in range(nvec)]
    rows.append(jnp.zeros((MOD_ROWS - nvec, tn), F32))
    o_ref[...] = jnp.concatenate(rows, axis=0) + b_ref[...]


def _modulation(cvecs, w_mod, b_mod):
    nvec = cvecs.shape[0]
    ct = jnp.zeros((D_MODEL, MOD_ROWS), F32).at[:, :nvec].set(cvecs.T)
    ncol = w_mod.shape[2]
    out = pl.pallas_call(
        functools.partial(_mod_kernel, nvec=nvec),
        grid=(DEPTH, ncol // MOD_TN),
        in_specs=[
            _const_spec((D_MODEL, MOD_ROWS)),
            pl.BlockSpec((None, D_MODEL, MOD_TN), lambda l, j: (l, 0, j)),
            pl.BlockSpec((None, 1, MOD_TN), lambda l, j: (l, 0, j)),
        ],
        out_specs=pl.BlockSpec((None, MOD_ROWS, MOD_TN), lambda l, j: (l, 0, j)),
        out_shape=jax.ShapeDtypeStruct((DEPTH, MOD_ROWS, ncol), F32),
        scratch_shapes=[pltpu.VMEM((nvec, D_MODEL, LANES), F32)],
        compiler_params=_cp(("arbitrary", "arbitrary")),
        name="modulation",
    )(ct, w_mod, b_mod.reshape(DEPTH, 1, ncol))
    return out.reshape(DEPTH * MOD_ROWS, MOD_CHUNKS, D_MODEL)


def _mla_kv_heads(ckv_n, kr, wkv_ref, gk_ref, rope):
    kv = _dot(ckv_n.astype(BF16), wkv_ref[...])
    ss_r = jnp.sum(kr * kr, axis=-1, keepdims=True)
    ks = []
    for h in range(MLA_HEADS):
        kn = kv[:, h * MLA_NOPE:(h + 1) * MLA_NOPE]
        ms = (jnp.sum(kn * kn, axis=-1, keepdims=True) + ss_r) * (1.0 / MLA_QK)
        r = lax.rsqrt(ms + EPS)
        tail = kr * r * gk_ref[:, h * 256 + 128:(h + 1) * 256]
        if rope is not None:
            tail = _rope128(tail, *rope)
        ks.append(kn * r * gk_ref[:, h * 256:h * 256 + 128])
        ks.append(tail)
    return jnp.concatenate(ks, axis=-1), kv[:, MLA_HEADS * MLA_NOPE:]


def _proj_kernel(*refs, latent):
    (x_ref, mod_ref, gmix_ref, win_ref, wq_ref, wkv_ref, gqna_ref, gkna_ref, gql_ref, gkvl_ref,
     gqm_ref, gkm_ref, gqw_ref, gkw_ref) = refs[:14]
    pos = 14
    rope = None
    if latent:
        rope = (refs[14][...], refs[15][...])
        pos = 16
    (qna_o, kna_o, vna_o, qm_o, km_o, vm_o, qw_o, kw_o, vw_o, fv_o) = refs[pos:pos + 10]
    cache_o = refs[pos + 10:]

    x = x_ref[...]
    rinv = lax.rsqrt(jnp.mean(x * x, axis=-1, keepdims=True) + EPS)
    h = (x * rinv) * (gmix_ref[...] * (1.0 + mod_ref[0, 1:2, :])) + mod_ref[0, 0:1, :]
    hb = h.astype(BF16)

    projected = _dot(hb, win_ref[...])

    def proj(c0, c1):
        return projected[:, c0:c1]

    qna_o[...] = (_seg_rms64(proj(C_AQ, C_AK)) * gqna_ref[...]).astype(BF16)
    k_na = _seg_rms64(proj(C_AK, C_AV)) * gkna_ref[...]
    kna_o[...] = k_na.astype(BF16)
    v_na = proj(C_AV, C_MQ)
    vna_o[...] = v_na.astype(BF16)

    cq = (_rms(proj(C_MQ, C_CKV), MLA_Q_LORA) * gql_ref[...]).astype(BF16)
    qu = _dot(cq, wq_ref[...])
    qs = []
    for hd in range(MLA_HEADS):
        blk = qu[:, hd * 256:(hd + 1) * 256]
        blk = blk * lax.rsqrt(jnp.sum(blk * blk, axis=-1, keepdims=True) * (1.0 / MLA_QK) + EPS)
        blk = blk * gqm_ref[:, hd * 256:(hd + 1) * 256]
        if latent:
            blk = jnp.concatenate([blk[:, :LANES], _rope128(blk[:, LANES:], *rope)], axis=-1)
        qs.append(blk)
    qm_o[...] = jnp.concatenate(qs, axis=-1).astype(BF16)
    ckv_n = _rms(proj(C_CKV, C_KR), MLA_KV_LORA) * gkvl_ref[...]

    tail = proj(C_KR, IN_COLS_P)
    lo = lax.broadcasted_iota(jnp.int32, (x.shape[0], LANES), 1) < HEAD_DIM
    nslab = (IN_COLS_P - C_KR) // LANES
    slabs = [tail[:, c * LANES:(c + 1) * LANES] for c in range(nslab)]
    swapped = [pltpu.roll(t, HEAD_DIM, 1) for t in slabs]
    al = [jnp.where(lo, swapped[c], swapped[c + 1]) for c in range(nslab - 1)]
    kr = jnp.where(lo, slabs[0], 0.0)
    k_m, v_m = _mla_kv_heads(ckv_n, kr, wkv_ref, gkm_ref, rope)
    km_o[...] = k_m.astype(BF16)
    vm_o[...] = v_m.astype(BF16)

    def dup(t):
        sw = pltpu.roll(t, HEAD_DIM, 1)
        return jnp.concatenate([jnp.where(lo, t, sw), jnp.where(lo, sw, t)], axis=-1)

    q_w = _seg_rms64(jnp.concatenate(al[0:4], axis=-1)) * gqw_ref[...]
    k_w = _seg_rms64(al[4]) * gkw_ref[...]
    v_w = al[5]
    if latent:
        q_w = _rope_wide(q_w, *rope)
        k_w = _rope128(k_w, *rope)
    qw_o[...] = q_w.astype(BF16)
    kw_o[...] = dup(k_w).astype(BF16)
    vw_o[...] = dup(v_w).astype(BF16)

    fv_o[...] = jnp.concatenate(al[6:10], axis=-1).astype(BF16)

    if not latent:
        kna32_o, vna32_o, ckv32_o, kr32_o, kw32_o, vw32_o = cache_o
        kna32_o[...] = k_na
        vna32_o[...] = v_na
        ckv32_o[...] = ckv_n
        kr32_o[...] = kr[:, :MLA_ROPE]
        kw32_o[...] = k_w
        vw32_o[...] = v_w


def _project(x, mod, layer, p, latent, rope_tabs):
    n = x.shape[0]
    tm = TM_PROJ
    row = lambda i: (i, 0)
    if latent:
        mod_map = lambda i: (layer * MOD_ROWS + 1 + (i * tm) // DEC_SEQ, 0, 0)
    else:
        mod_map = lambda i: (layer * MOD_ROWS, 0, 0)
    in_specs = [
        pl.BlockSpec((tm, D_MODEL), row),
        pl.BlockSpec((1, MOD_CHUNKS, D_MODEL), mod_map),
        _layer_spec((1, D_MODEL), layer),
        _layer_spec((D_MODEL, IN_COLS_P), layer, resident=True),
        _layer_spec((MLA_Q_LORA, 1024), layer, resident=True),
        _layer_spec((MLA_KV_LORA, 1024), layer, resident=True),
    ] + [_layer_spec((1, wd), layer) for wd in (512, 512, MLA_Q_LORA, MLA_KV_LORA, 1024, 1024, 512, 128)]
    args = [x, mod, p['g_mix'], p['w_in'], p['w_q_up'], p['w_kv_up'], p['gq_na'], p['gk_na'], p['g_q_lora'],
            p['g_kv_lora'], p['gq_mla'], p['gk_mla'], p['gq_win'], p['gk_win']]
    if latent:
        nt = DEC_SEQ // tm
        in_specs += [pl.BlockSpec((tm, LANES), lambda i: (i % nt, 0))] * 2
        args += list(rope_tabs)
    widths = [512, 512, 512, 1024, 1024, 512, 512, 256, 256, 512]
    out_specs = [pl.BlockSpec((tm, w), row) for w in widths]
    out_shape = [jax.ShapeDtypeStruct((n, w), BF16) for w in widths]
    if not latent:
        cache_w = [512, 512, MLA_KV_LORA, MLA_ROPE, 128, 128]
        out_specs += [pl.BlockSpec((tm, w), row) for w in cache_w]
        out_shape += [jax.ShapeDtypeStruct((n, w), F32) for w in cache_w]
    return pl.pallas_call(
        functools.partial(_proj_kernel, latent=latent),
        grid=(n // tm,),
        in_specs=in_specs,
        out_specs=out_specs,
        out_shape=out_shape,
        compiler_params=_cp(("arbitrary",)),
        name="proj_latent" if latent else "proj_context",
    )(*args)


def _mla_ctx_kernel(ckv_ref, kr_ref, wkv_ref, gk_ref, k_o, v_o):
    k_m, v_m = _mla_kv_heads(ckv_ref[...], kr_ref[...], wkv_ref, gk_ref, None)
    k_o[...] = k_m.astype(BF16)
    v_o[...] = v_m.astype(BF16)


def _mla_ctx(ckv, kr128, p, layer, nb):
    n = nb * PAST_LEN
    src = lambda b: (b * DEPTH + layer, 0)
    row = lambda b: (b, 0)
    return pl.pallas_call(
        _mla_ctx_kernel,
        grid=(nb,),
        in_specs=[pl.BlockSpec((PAST_LEN, MLA_KV_LORA), src), pl.BlockSpec((PAST_LEN, LANES), src),
                  _layer_spec((MLA_KV_LORA, 1024), layer), _layer_spec((1, 1024), layer)],
        out_specs=[pl.BlockSpec((PAST_LEN, 1024), row), pl.BlockSpec((PAST_LEN, 512), row)],
        out_shape=[jax.ShapeDtypeStruct((n, 1024), BF16), jax.ShapeDtypeStruct((n, 512), BF16)],
        compiler_params=_cp(("arbitrary",)),
        name="mla_cached_kv",
    )(ckv, kr128, p['w_kv_up'], p['gk_mla'])


def _softmax_pv(scores, values, sink=None):
    m = functools.reduce(jnp.maximum, [jnp.max(s, axis=-1, keepdims=True) for s in scores])
    if sink is not None:
        m = jnp.maximum(m, sink)
    ps = [jnp.exp2(s - m) for s in scores]
    l = functools.reduce(lambda a, b: a + b, [jnp.sum(p, axis=-1, keepdims=True) for p in ps])
    if sink is not None:
        l = l + jnp.exp2(sink - m)
    o = functools.reduce(lambda a, b: a + b, [_dot(p.astype(BF16), v) for p, v in zip(ps, values)])
    return o / l


def _half_masks():
    lane = lax.broadcasted_iota(jnp.int32, (1, LANES), 1)
    lo = lane < HEAD_DIM
    return jnp.where(lo, 1.0, 0.0).astype(BF16), jnp.where(lo, 0.0, 1.0).astype(BF16)


def _pair_attention(q2, keys, values, extra=None, sinks=None):
    rows = q2.shape[0]
    lo = lax.broadcasted_iota(jnp.int32, (rows, LANES), 1) < HEAD_DIM
    outs = []
    for e, msk in enumerate(_half_masks()):
        qm = q2 * msk
        scores = []
        for idx, k2 in enumerate(keys):
            s = _dot_nt(qm, k2)
            if extra is not None:
                s = extra(e, idx, s)
            scores.append(s)
        sink = None
        if sinks is not None:
            sink = jnp.full((rows, 1), sinks[e], F32)
        outs.append(_softmax_pv(scores, values, sink))
    return jnp.where(lo, outs[0], outs[1])


def _dft_real(x, cc_ref, sc_ref, cs, ss, scale):
    outs = []
    for g in range(FN_GROUPS):
        xg = x[:, g * FN_CH:(g + 1) * FN_CH]
        xc = _dot(xg, cc_ref[...]).astype(BF16)
        xs = _dot(xg, sc_ref[...]).astype(BF16)
        outs.append(_dot(cs, xc) - _dot(ss, xs))
    return jnp.concatenate(outs, axis=-1) * scale


def _ctx_attn_kernel(sink_ref, qna, kna, vna, qm, km, vm, qw, kw, vw, fv, cs_ref, ss_ref, cc_ref, sc_ref,
                     oa, ob, oc, od, *, layer):
    for hp in range(4):
        cs = slice(hp * LANES, (hp + 1) * LANES)
        oa[:, cs] = _pair_attention(qna[:, cs], [kna[:, cs]], [vna[:, cs]]).astype(BF16)
    for hd in range(MLA_HEADS):
        s = _dot_nt(qm[:, hd * 256:(hd + 1) * 256], km[:, hd * 256:(hd + 1) * 256])
        ob[:, hd * 128:(hd + 1) * 128] = _softmax_pv([s], [vm[:, hd * 128:(hd + 1) * 128]]).astype(BF16)
    for hp in range(4):
        cs = slice(hp * LANES, (hp + 1) * LANES)
        kcs = slice((hp // 2) * LANES, (hp // 2 + 1) * LANES)
        oc[:, cs] = _pair_attention(qw[:, cs], [kw[:, kcs]], [vw[:, kcs]],
                                    sinks=(sink_ref[layer, 2 * hp], sink_ref[layer, 2 * hp + 1])).astype(BF16)
    od[...] = _dft_real(fv[...], cc_ref, sc_ref, cs_ref[...], ss_ref[...],
                        1.0 / math.sqrt(SEQ * FN_CH)).astype(BF16)


def _ctx_attention(pr, sink, dft, layer):
    qna, kna, vna, qm, km, vm, qw, kw, vw, fv = pr[:10]
    n = qna.shape[0]
    row = lambda b: (b, 0)
    widths = [512, 512, 512, 1024, 1024, 512, 512, 256, 256, 512]
    in_specs = [pl.BlockSpec(memory_space=pltpu.SMEM)]
    in_specs += [pl.BlockSpec((SEQ, w), row) for w in widths]
    in_specs += [_const_spec((SEQ, SEQ)), _const_spec((SEQ, SEQ)), _const_spec((FN_CH, FN_CH)),
                 _const_spec((FN_CH, FN_CH))]
    return pl.pallas_call(
        functools.partial(_ctx_attn_kernel, layer=layer),
        grid=(n // SEQ,),
        in_specs=in_specs,
        out_specs=[pl.BlockSpec((SEQ, 512), row)] * 4,
        out_shape=[jax.ShapeDtypeStruct((n, 512), BF16)] * 4,
        compiler_params=_cp(("arbitrary",)),
        name="context_attention",
    )(sink, qna, kna, vna, qm, km, vm, qw, kw, vw, fv, dft['cs_ctx'], dft['ss_ctx'], dft['cc'], dft['sc'])


def _na_kernel(q_ref, k_ref, v_ref, kc_ref, vc_ref, tl_ref, tr_ref, o_ref):
    qt = pl.program_id(1)
    rows = DEC_SEQ // GRID_W
    qrows = TQ // GRID_W
    kstart = jnp.clip(qt * qrows - NA_KH // 2, 0, rows - NA_KROWS)
    ks = pl.multiple_of(kstart * GRID_W, GRID_W)
    nk = NA_KROWS * GRID_W

    blk_idx = []
    for i in range(qrows):
        r = qt * qrows + i
        rs = jnp.clip(r - NA_KH // 2, 0, rows - NA_KH)
        blk_idx.append([jnp.where((kstart + j >= rs) & (kstart + j < rs + NA_KH), kstart + j - r + NA_KH - 1,
                                  NA_MASKED) for j in range(NA_KROWS)])

    def bias(h):
        return jnp.concatenate([
            jnp.concatenate([tl_ref[h, blk_idx[i][2 * jp]] + tr_ref[h, blk_idx[i][2 * jp + 1]]
                             for jp in range(NA_KROWS // 2)], axis=-1)
            for i in range(qrows)], axis=0)

    for hp in range(4):
        cs = slice(hp * LANES, (hp + 1) * LANES)

        def extra(e, idx, s, hp=hp):
            return s + bias(2 * hp + e) if idx == 0 else s

        o = _pair_attention(q_ref[:, cs], [k_ref[pl.ds(ks, nk), cs], kc_ref[:, cs]],
                            [v_ref[pl.ds(ks, nk), cs], vc_ref[:, cs]], extra=extra)
        o_ref[:, cs] = o.astype(BF16)


def _lat_specs(width_q, width_k, width_v, ctx_map, tq=TQ):
    nq = DEC_SEQ // tq
    ctx_nd = len(ctx_map(0, 0))
    lead = (None,) * (ctx_nd - 2)
    return [
        pl.BlockSpec((tq, width_q), lambda b, t: (b * nq + t, 0)),
        pl.BlockSpec((DEC_SEQ, width_k), lambda b, t: (b, 0)),
        pl.BlockSpec((DEC_SEQ, width_v), lambda b, t: (b, 0)),
        pl.BlockSpec(lead + (PAST_LEN, width_k), ctx_map),
        pl.BlockSpec(lead + (PAST_LEN, width_v), ctx_map),
    ]


def _lat_call(kernel, name, args, in_specs, nb, tq=TQ):
    nq = DEC_SEQ // tq
    return pl.pallas_call(
        kernel,
        grid=(nb, nq),
        in_specs=in_specs,
        out_specs=pl.BlockSpec((tq, 512), lambda b, t: (b * nq + t, 0)),
        out_shape=jax.ShapeDtypeStruct((nb * DEC_SEQ, 512), BF16),
        compiler_params=_cp(("arbitrary", "arbitrary")),
        name=name,
    )(*args)


def _na_attention(q, k, v, kc, vc, bias_tabs, nb, layer):
    specs = _lat_specs(512, 512, 512, lambda b, t: (b, layer, 0, 0))
    specs += [_layer_spec(bias_tabs[0].shape[1:], layer, resident=True)] * 2
    return _lat_call(_na_kernel, "neighbourhood_attention", (q, k, v, kc, vc) + tuple(bias_tabs), specs, nb)


def _mla_kernel(q_ref, k_ref, v_ref, kc_ref, vc_ref, o_ref):
    for hd in range(MLA_HEADS):
        ks = slice(hd * 256, (hd + 1) * 256)
        vs = slice(hd * 128, (hd + 1) * 128)
        q = q_ref[:, ks]
        scores = [_dot_nt(q, k_ref[:, ks]), _dot_nt(q, kc_ref[:, ks])]
        o_ref[:, vs] = _softmax_pv(scores, [v_ref[:, vs], vc_ref[:, vs]]).astype(BF16)


def _mla_attention(q, k, v, kc, vc, nb):
    specs = _lat_specs(1024, 1024, 512, lambda b, t: (b, 0), tq=TQ_MLA)
    return _lat_call(_mla_kernel, "latent_attention", (q, k, v, kc, vc), specs, nb, tq=TQ_MLA)


def _win_kernel(sink_ref, q_ref, k_ref, v_ref, kc_ref, vc_ref, o_ref, *, layer):
    qt = pl.program_id(1)
    nk = 4 * WINDOW
    ks = pl.multiple_of(jnp.clip(qt * TQ - WINDOW, 0, DEC_SEQ - nk), WINDOW)
    qpos = qt * TQ + lax.broadcasted_iota(jnp.int32, (TQ, nk), 0)
    kpos = ks + lax.broadcasted_iota(jnp.int32, (TQ, nk), 1)
    ok = jnp.abs(kpos - qpos) <= WINDOW

    def extra(e, idx, s):
        return jnp.where(ok, s, NEG) if idx == 0 else s

    for hp in range(4):
        cs = slice(hp * LANES, (hp + 1) * LANES)
        kcs = slice((hp // 2) * LANES, (hp // 2 + 1) * LANES)
        o = _pair_attention(q_ref[:, cs], [k_ref[pl.ds(ks, nk), kcs], kc_ref[:, kcs]],
                            [v_ref[pl.ds(ks, nk), kcs], vc_ref[:, kcs]], extra=extra,
                            sinks=(sink_ref[layer, 2 * hp], sink_ref[layer, 2 * hp + 1]))
        o_ref[:, cs] = o.astype(BF16)


def _win_attention(sink, q, k, v, kc, vc, nb, layer):
    specs = [pl.BlockSpec(memory_space=pltpu.SMEM)]
    specs += _lat_specs(512, 256, 256, lambda b, t: (b, layer, 0, 0))
    return _lat_call(functools.partial(_win_kernel, layer=layer), "window_attention", (sink, q, k, v, kc, vc),
                     specs, nb)


def _fourier_kernel(x_ref, cc_ref, sc_ref, cs_ref, ss_ref, o_ref, xc_ref, xs_ref):
    @pl.when(pl.program_id(1) == 0)
    def _():
        for g in range(FN_GROUPS):
            gs = slice(g * FN_CH, (g + 1) * FN_CH)
            xg = x_ref[:, gs]
            xc_ref[:, gs] = _dot(xg, cc_ref[...]).astype(BF16)
            xs_ref[:, gs] = _dot(xg, sc_ref[...]).astype(BF16)

    y = _dot(cs_ref[...], xc_ref[...]) - _dot(ss_ref[...], xs_ref[...])
    o_ref[...] = (y * (1.0 / math.sqrt(DEC_SEQ * FN_CH))).astype(BF16)


def _fourier(x, dft, nb):
    nq = DEC_SEQ // TQ
    return pl.pallas_call(
        _fourier_kernel,
        grid=(nb, nq),
        in_specs=[
            pl.BlockSpec((DEC_SEQ, 512), lambda b, t: (b, 0)),
            _const_spec((FN_CH, FN_CH)), _const_spec((FN_CH, FN_CH)),
            pl.BlockSpec((TQ, DEC_SEQ), lambda b, t: (t, 0)),
            pl.BlockSpec((TQ, DEC_SEQ), lambda b, t: (t, 0)),
        ],
        out_specs=pl.BlockSpec((TQ, 512), lambda b, t: (b * nq + t, 0)),
        out_shape=jax.ShapeDtypeStruct((nb * DEC_SEQ, 512), BF16),
        scratch_shapes=[pltpu.VMEM((DEC_SEQ, 512), BF16), pltpu.VMEM((DEC_SEQ, 512), BF16)],
        compiler_params=_cp(("arbitrary", "arbitrary")),
        name="fourier_mix",
    )(x, dft['cc'], dft['sc'], dft['cs_lat'], dft['ss_lat'])


def _out_kernel(oa, ob, oc, od, x_ref, mod_ref, gffn_ref, w_ref, x1_o, h2_o):
    o = jnp.concatenate([oa[...], ob[...], oc[...], od[...]], axis=-1)
    acc = _dot(o, w_ref[...])
    x1 = x_ref[...] + mod_ref[0, 2:3, :] * acc
    x1_o[...] = x1
    rinv = lax.rsqrt(jnp.mean(x1 * x1, axis=-1, keepdims=True) + EPS)
    h2 = (x1 * rinv) * (gffn_ref[...] * (1.0 + mod_ref[0, 4:5, :])) + mod_ref[0, 3:4, :]
    h2_o[...] = h2.astype(BF16)


def _mod_map(layer, tm, latent):
    if latent:
        return lambda i, *_: (layer * MOD_ROWS + 1 + (i * tm) // DEC_SEQ, 0, 0)
    return lambda i, *_: (layer * MOD_ROWS, 0, 0)


def _out_project(o4, x, mod, layer, p, latent):
    n = x.shape[0]
    tm = TM_OUT
    row = lambda i: (i, 0)
    return pl.pallas_call(
        _out_kernel,
        grid=(n // tm,),
        in_specs=[pl.BlockSpec((tm, 512), row)] * 4 + [
            pl.BlockSpec((tm, D_MODEL), row),
            pl.BlockSpec((1, MOD_CHUNKS, D_MODEL), _mod_map(layer, tm, latent)),
            _layer_spec((1, D_MODEL), layer),
            _layer_spec((D_MODEL, D_MODEL), layer, resident=True),
        ],
        out_specs=[pl.BlockSpec((tm, D_MODEL), row), pl.BlockSpec((tm, D_MODEL), row)],
        out_shape=[jax.ShapeDtypeStruct((n, D_MODEL), F32), jax.ShapeDtypeStruct((n, D_MODEL), BF16)],
        compiler_params=_cp(("arbitrary",)),
        name="out_proj_latent" if latent else "out_proj_context",
    )(*o4, x, mod, p['g_ffn'], p['w_out'])


def _ffn_kernel(h_ref, hp_ref, hn_ref, x1_ref, mod_ref, wg_ref, wu_ref, wc_ref, wd_ref, o_ref, hext_ref, acc_ref,
                *, seq_len):
    i = pl.program_id(0)
    j = pl.program_id(1)
    tm = h_ref.shape[0]

    @pl.when(j == 0)
    def _():
        hext_ref[0:HALO, :] = hp_ref[...]
        hext_ref[HALO:HALO + tm, :] = h_ref[...]
        hext_ref[HALO + tm:, :] = hn_ref[...]
        acc_ref[...] = jnp.zeros_like(acc_ref)

    g_ext = _dot(hext_ref[...], wg_ref[...])
    rows_ext = g_ext.shape[0]
    pos = (i * tm + lax.broadcasted_iota(jnp.int32, (tm, 1), 0)) % seq_len
    g_mid = g_ext[HALO:HALO + tm]
    g_prev = pltpu.roll(g_ext, 1, 0)[HALO:HALO + tm]
    g_next = pltpu.roll(g_ext, rows_ext - 1, 0)[HALO:HALO + tm]
    g_prev = jnp.where(pos == 0, 0.0, g_prev)
    g_next = jnp.where(pos == seq_len - 1, 0.0, g_next)
    g = g_prev * wc_ref[0:1, :] + g_mid * wc_ref[1:2, :] + g_next * wc_ref[2:3, :]
    u = _dot(h_ref[...], wu_ref[...])
    act = (g / (1.0 + jnp.exp(-g))) * u
    acc_ref[...] += _dot(act.astype(BF16), wd_ref[...])

    @pl.when(j == pl.num_programs(1) - 1)
    def _():
        o_ref[...] = x1_ref[...] + mod_ref[0, 5:6, :] * acc_ref[...]


def _ffn(h2, x1, mod, layer, p, latent):
    n = x1.shape[0]
    tm, tf = TM_FFN, TF_FFN
    seq_len = DEC_SEQ if latent else SEQ
    assert n % tm == 0 and tm % HALO == 0 and D_FF % tf == 0
    hb = tm // HALO
    nhalo = n // HALO
    return pl.pallas_call(
        functools.partial(_ffn_kernel, seq_len=seq_len),
        grid=(n // tm, D_FF // tf),
        in_specs=[
            pl.BlockSpec((tm, D_MODEL), lambda i, j: (i, 0)),
            pl.BlockSpec((HALO, D_MODEL), lambda i, j: (jnp.maximum(i * hb - 1, 0), 0)),
            pl.BlockSpec((HALO, D_MODEL), lambda i, j: (jnp.minimum((i + 1) * hb, nhalo - 1), 0)),
            pl.BlockSpec((tm, D_MODEL), lambda i, j: (i, 0)),
            pl.BlockSpec((1, MOD_CHUNKS, D_MODEL), _mod_map(layer, tm, latent)),
            pl.BlockSpec((None, D_MODEL, tf), lambda i, j: (layer, 0, j)),
            pl.BlockSpec((None, D_MODEL, tf), lambda i, j: (layer, 0, j)),
            pl.BlockSpec((None, 8, tf), lambda i, j: (layer, 0, j)),
            pl.BlockSpec((None, tf, D_MODEL), lambda i, j: (layer, j, 0)),
        ],
        out_specs=pl.BlockSpec((tm, D_MODEL), lambda i, j: (i, 0)),
        out_shape=jax.ShapeDtypeStruct((n, D_MODEL), F32),
        scratch_shapes=[pltpu.VMEM((tm + 2 * HALO, D_MODEL), BF16), pltpu.VMEM((tm, D_MODEL), F32)],
        compiler_params=_cp(("arbitrary", "arbitrary")),
        name="ffn_latent" if latent else "ffn_context",
    )(h2, h2, h2, x1, mod, p['w_gate'], p['w_up'], p['w_conv'], p['w_down'])


def _tile_row(g, reps, scale=1.0):
    return (jnp.tile(g, (1, reps)) * scale)[:, None, :]


def _prep(w):
    win_r = jnp.pad(w['w_in'].astype(BF16), ((0, 0), (0, 0), (0, IN_COLS_P - IN_COLS)))

    wq = w['w_q_up'].reshape(DEPTH, MLA_Q_LORA, MLA_HEADS, MLA_QK)
    wq_r = jnp.concatenate([wq, jnp.zeros((DEPTH, MLA_Q_LORA, MLA_HEADS, 256 - MLA_QK), F32)], axis=-1)
    wq_r = wq_r.reshape(DEPTH, MLA_Q_LORA, MLA_HEADS * 256).astype(BF16)
    wkv = w['w_kv_up'].reshape(DEPTH, MLA_KV_LORA, MLA_HEADS, 2 * MLA_NOPE)
    wkv_r = jnp.concatenate([wkv[..., :MLA_NOPE].reshape(DEPTH, MLA_KV_LORA, -1),
                             wkv[..., MLA_NOPE:].reshape(DEPTH, MLA_KV_LORA, -1)], axis=-1).astype(BF16)

    def pad_head(g, s):
        return _tile_row(jnp.concatenate([g, jnp.zeros((DEPTH, 256 - MLA_QK), F32)], axis=-1), MLA_HEADS, s)

    wc = jnp.concatenate([w['w_conv'], jnp.zeros((DEPTH, 5, D_FF), F32)], axis=1)
    return dict(
        g_mix=w['g_mix'][:, None, :], g_ffn=w['g_ffn'][:, None, :],
        w_in=win_r, w_q_up=wq_r, w_kv_up=wkv_r,
        gq_na=_tile_row(w['g_qn_na'], 8, HEAD_DIM ** -0.5 * LOG2E), gk_na=_tile_row(w['g_kn_na'], 8),
        g_q_lora=w['g_q_lora'][:, None, :], g_kv_lora=w['g_kv_lora'][:, None, :],
        gq_mla=pad_head(w['g_qn_mla'], MLA_QK ** -0.5 * LOG2E), gk_mla=pad_head(w['g_kn_mla'], 1.0),
        gq_win=_tile_row(w['g_qn_win'], 8, HEAD_DIM ** -0.5 * LOG2E), gk_win=_tile_row(w['g_kn_win'], 2),
        sink=w['sink_win'] * LOG2E,
        w_out=w['w_out'].astype(BF16), w_gate=w['w_gate'].astype(BF16), w_up=w['w_up'].astype(BF16),
        w_conv=wc, w_down=w['w_down'].astype(BF16),
    )


def _na_bias_tables(rpb):
    lead = rpb.shape[:-1]
    ext = jnp.concatenate([jnp.repeat(rpb[..., :1], 48, axis=-1), rpb, jnp.repeat(rpb[..., -1:], 48, axis=-1)], axis=-1)
    c = np.arange(GRID_W)
    cs = np.clip(c - NA_KW // 2, 0, GRID_W - NA_KW)
    col_ok = (c[None, :] >= cs[:, None]) & (c[None, :] < cs[:, None] + NA_KW)
    no, yes = np.zeros_like(col_ok), np.ones_like(col_ok)

    def toeplitz(e):
        t = jnp.tile(e, (1,) * len(lead) + (GRID_W + 1,))[..., :GRID_W * LANES].reshape(lead + (GRID_W, LANES))
        return t[..., ::-1, :]

    tabs = []
    for shift, halves in ((0, (col_ok, no)), (GRID_W - 1, (no, col_ok))):
        keep = np.concatenate(halves, axis=1)
        side = np.concatenate((yes, no) if shift == 0 else (no, yes), axis=1)
        fill = jnp.asarray(np.where(side, NEG, 0.0).astype(np.float32))
        t = jnp.where(jnp.asarray(keep), toeplitz(jnp.roll(ext, -shift, axis=-1)), fill)
        tabs.append(jnp.concatenate([t, jnp.broadcast_to(fill, lead[:-1] + (1, GRID_W, LANES))], axis=-3))
    return tuple(tabs)


@functools.lru_cache(maxsize=None)
def _dft_tables():
    def cs(n):
        k = (np.arange(n)[:, None] * np.arange(n)[None, :]) % n
        ang = 2.0 * np.pi * k.astype(np.float64) / n
        return np.cos(ang), np.sin(ang)

    out = {}
    for name, n in (('ctx', SEQ), ('lat', DEC_SEQ)):
        c, s = cs(n)
        out['cs_' + name], out['ss_' + name] = c, s
    out['cc'], out['sc'] = cs(FN_CH)
    return out


@functools.lru_cache(maxsize=None)
def _rope_tables():
    t = np.arange(DEC_SEQ)
    quarter = MLA_ROPE // 4
    inv = ROPE_BASE ** (-np.arange(quarter, dtype=np.float64) / quarter)
    j = np.arange(MLA_ROPE)
    pos = np.where(j[None, :] < MLA_ROPE // 2, (t // GRID_W)[:, None], (t % GRID_W)[:, None]).astype(np.float64)
    ang = pos * inv[j % quarter][None, :]
    sign = np.where((j % 32) < 16, -1.0, 1.0)
    cos = np.cos(ang)
    sin = np.sin(ang) * sign[None, :]
    return np.tile(cos, (1, 2)).astype(np.float32), np.tile(sin, (1, 2)).astype(np.float32)


def kernel(x_prompt, x_sample, cache_na_k, cache_na_v, cache_mla_ckv, cache_mla_krope, cache_win_k, cache_win_v,
           c, c_ctx, w_mod, b_mod, g_mix, g_ffn, w_in, g_qn_na, g_kn_na, rpb_na, g_q_lora, w_q_up, g_kv_lora,
           w_kv_up, g_qn_mla, g_kn_mla, g_qn_win, g_kn_win, sink_win, w_out, w_gate, w_up, w_conv, w_down):
    w = dict(g_mix=g_mix, g_ffn=g_ffn, w_in=w_in, g_qn_na=g_qn_na, g_kn_na=g_kn_na, g_q_lora=g_q_lora,
             w_q_up=w_q_up, g_kv_lora=g_kv_lora, w_kv_up=w_kv_up, g_qn_mla=g_qn_mla, g_kn_mla=g_kn_mla,
             g_qn_win=g_qn_win, g_kn_win=g_kn_win, sink_win=sink_win, w_out=w_out, w_gate=w_gate, w_up=w_up,
             w_conv=w_conv, w_down=w_down)
    nb_ctx, nb_lat = x_prompt.shape[0], x_sample.shape[0]
    dft = {k: jnp.asarray(v, F32).astype(BF16) for k, v in _dft_tables().items()}
    rope_tabs = tuple(jnp.asarray(t) for t in _rope_tables())

    mod = _modulation(jnp.concatenate([c_ctx[None, :], c], axis=0), w_mod, b_mod)

    xp = x_prompt.reshape(nb_ctx * SEQ, D_MODEL)
    xs = x_sample.reshape(nb_lat * DEC_SEQ, D_MODEL)
    caches = []
    p = _prep(w)
    nctx = nb_lat * DEPTH * PAST_LEN
    dup = lambda t: jnp.concatenate([t[..., :64], t[..., :64], t[..., 64:], t[..., 64:]], axis=-1).astype(BF16)
    kna_c = cache_na_k.reshape(nb_lat, DEPTH, PAST_LEN, 512).astype(BF16)
    vna_c = cache_na_v.reshape(nb_lat, DEPTH, PAST_LEN, 512).astype(BF16)
    kw_c = dup(cache_win_k.reshape(nb_lat, DEPTH, PAST_LEN, 128))
    vw_c = dup(cache_win_v.reshape(nb_lat, DEPTH, PAST_LEN, 128))
    ckv_c = cache_mla_ckv.reshape(nctx, MLA_KV_LORA)
    kr_c = jnp.pad(cache_mla_krope.reshape(nctx, MLA_ROPE), ((0, 0), (0, LANES - MLA_ROPE)))
    bias_tabs = _na_bias_tables(rpb_na * LOG2E)
    for l in range(DEPTH):

        pr = _project(xp, mod, l, p, False, None)
        caches.append(pr[10:])
        o4 = _ctx_attention(pr, p['sink'], dft, l)
        x1, h2 = _out_project(o4, xp, mod, l, p, False)
        xp = _ffn(h2, x1, mod, l, p, False)

        qna, kna, vna, qm, km, vm, qw, kw, vw, fv = _project(xs, mod, l, p, True, rope_tabs)
        km_c, vm_c = _mla_ctx(ckv_c, kr_c, p, l, nb_lat)
        o_a = _na_attention(qna, kna, vna, kna_c, vna_c, bias_tabs, nb_lat, l)
        o_b = _mla_attention(qm, km, vm, km_c, vm_c, nb_lat)
        o_c = _win_attention(p['sink'], qw, kw, vw, kw_c, vw_c, nb_lat, l)
        o_d = _fourier(fv, dft, nb_lat)
        x1, h2 = _out_project((o_a, o_b, o_c, o_d), xs, mod, l, p, True)
        xs = _ffn(h2, x1, mod, l, p, True)

    def stack(idx, shape):
        return jnp.stack([caches[l][idx].reshape((nb_ctx, SEQ) + shape) for l in range(DEPTH)], axis=1)

    return (xp.reshape(nb_ctx, SEQ, D_MODEL), xs.reshape(nb_lat, DEC_SEQ, D_MODEL),
            stack(0, (8, HEAD_DIM)), stack(1, (8, HEAD_DIM)), stack(2, (MLA_KV_LORA,)), stack(3, (MLA_ROPE,)),
            stack(4, (2, HEAD_DIM)), stack(5, (2, HEAD_DIM)))
```

```python
import functools
import math

import numpy as np
import jax
import jax.numpy as jnp
from jax import lax
from jax.experimental import pallas as pl
from jax.experimental.pallas import tpu as pltpu

F32 = jnp.float32
BF16 = jnp.bfloat16

D_MODEL = 2048
DEPTH = 2
SEQ = 256
DEC_SEQ = 2048
PAST_LEN = 512
GRID_W = 64
HEAD_DIM = 64
GROUP_W = 512
MLA_HEADS = 4
MLA_NOPE = 128
MLA_ROPE = 64
MLA_QK = MLA_NOPE + MLA_ROPE
MLA_Q_LORA = 384
MLA_KV_LORA = 128
WINDOW = 128
FN_GROUPS = 4
FN_CH = 128
D_FF = 5632
MOD_CHUNKS = 6
ROPE_BASE = 10000.0
EPS = 1e-6
NEG = -1e30
LOG2E = math.log2(math.e)

LANES = 128
MOD_ROWS = 8
VMEM_LIMIT = 56 * 1024 * 1024

C_AQ, C_AK, C_AV = 0, 512, 1024
C_MQ = 1536
C_CKV = 1920
C_KR = 2048
IN_COLS = 3392
IN_COLS_P = 3456

TM_PROJ = 512
TM_OUT = 512
TM_FFN = 512
TF_FFN = 512
HALO = 16
TQ = 256
TQ_MLA = 512
NA_KH = 8
NA_KW = 16
NA_KROWS = 12
NA_MASKED = 15


def _cp(sem):
    return pltpu.CompilerParams(dimension_semantics=sem, vmem_limit_bytes=VMEM_LIMIT)


def _const_spec(shape, resident=False):
    n = len(shape)
    mode = dict(pipeline_mode=pl.Buffered(1)) if resident else {}
    return pl.BlockSpec(shape, lambda *_: (0,) * n, **mode)


def _layer_spec(shape, layer, resident=False):
    n = len(shape)
    mode = dict(pipeline_mode=pl.Buffered(1)) if resident else {}
    return pl.BlockSpec((None,) + tuple(shape), lambda *_: (layer,) + (0,) * n, **mode)


def _dot(a, b):
    return jnp.dot(a, b, preferred_element_type=F32)


def _dot_nt(a, b):
    return lax.dot_general(a, b, (((1,), (1,)), ((), ())), preferred_element_type=F32)


def _rms(t, n):
    return t * lax.rsqrt(jnp.sum(t * t, axis=-1, keepdims=True) * (1.0 / n) + EPS)


def _seg_rms64(t, seg_ref):
    outs = []
    for c in range(t.shape[1] // LANES):
        tc = t[:, c * LANES:(c + 1) * LANES]
        ms = _dot((tc * tc).astype(BF16), seg_ref[...])
        outs.append(tc * lax.rsqrt(ms + EPS))
    return jnp.concatenate(outs, axis=-1)


def _partner128(t):
    first = (lax.broadcasted_iota(jnp.int32, t.shape, 1) % 32) < 16
    return jnp.where(first, pltpu.roll(t, LANES - 16, 1), pltpu.roll(t, 16, 1))


def _rope128(t, cos, sin_signed):
    return t * cos + _partner128(t) * sin_signed


def _rope_wide(t, cos, sin_signed):
    return jnp.concatenate(
        [_rope128(t[:, c * LANES:(c + 1) * LANES], cos, sin_signed) for c in range(t.shape[1] // LANES)], axis=-1)


MOD_TN = 512


def _mod_kernel(ct_ref, w_ref, b_ref, o_ref, xb_ref, *, nvec):
    tn = w_ref.shape[1]
    nct = tn // LANES

    @pl.when((pl.program_id(0) == 0) & (pl.program_id(1) == 0))
    def _():
        cv = ct_ref[...]
        sx = cv / (1.0 + jnp.exp(-cv))
        for v in range(nvec):
            xb_ref[v] = jnp.broadcast_to(sx[:, v:v + 1], (D_MODEL, LANES))

    def body(kc, accs):
        k0 = pl.multiple_of(kc * 8, 8)
        ws = [w_ref[pl.ds(k0, 8), t * LANES:(t + 1) * LANES] for t in range(nct)]
        new = []
        for v in range(nvec):
            xv = xb_ref[v, pl.ds(k0, 8), :]
            new.extend(accs[v * nct + t] + xv * ws[t] for t in range(nct))
        return tuple(new)

    zero = jnp.zeros((8, LANES), F32)
    accs = lax.fori_loop(0, w_ref.shape[0] // 8, body, (zero,) * (nvec * nct), unroll=4)
    rows = [jnp.concatenate([jnp.sum(accs[v * nct + t], axis=0, keepdims=True) for t in range(nct)], axis=1)
            for v in range(nvec)]
    rows.append(jnp.zeros((MOD_ROWS - nvec, tn), F32))
    o_ref[...] = jnp.concatenate(rows, axis=0) + b_ref[...]


def _modulation(cvecs, w_mod, b_mod):
    nvec = cvecs.shape[0]
    ct = jnp.zeros((D_MODEL, MOD_ROWS), F32).at[:, :nvec].set(cvecs.T)
    ncol = w_mod.shape[2]
    out = pl.pallas_call(
        functools.partial(_mod_kernel, nvec=nvec),
        grid=(DEPTH, ncol // MOD_TN),
        in_specs=[
            _const_spec((D_MODEL, MOD_ROWS)),
            pl.BlockSpec((None, D_MODEL, MOD_TN), lambda l, j: (l, 0, j)),
            pl.BlockSpec((None, 1, MOD_TN), lambda l, j: (l, 0, j)),
        ],
        out_specs=pl.BlockSpec((None, MOD_ROWS, MOD_TN), lambda l, j: (l, 0, j)),
        out_shape=jax.ShapeDtypeStruct((DEPTH, MOD_ROWS, ncol), F32),
        scratch_shapes=[pltpu.VMEM((nvec, D_MODEL, LANES), F32)],
        compiler_params=_cp(("arbitrary", "arbitrary")),
        name="modulation",
    )(ct, w_mod, b_mod.reshape(DEPTH, 1, ncol))
    return out.reshape(DEPTH * MOD_ROWS, MOD_CHUNKS, D_MODEL)


def _mla_kv_heads(ckv_n, kr, wkv_ref, gk_ref, gkp_ref, rope):
    kv = _dot(ckv_n.astype(BF16), wkv_ref[...])
    ss_r = jnp.sum(kr * kr, axis=-1, keepdims=True)
    pk = _partner128(kr) if rope is not None else None
    ks = []
    for h in range(MLA_HEADS):
        kn = kv[:, h * MLA_NOPE:(h + 1) * MLA_NOPE]
        ms = (jnp.sum(kn * kn, axis=-1, keepdims=True) + ss_r) * (1.0 / MLA_QK)
        r = lax.rsqrt(ms + EPS)
        tail = kr * r * gk_ref[:, h * 256 + 128:(h + 1) * 256]
        if rope is not None:
            tail = tail * rope[0] + (pk * r * gkp_ref[:, h * LANES:(h + 1) * LANES]) * rope[1]
        ks.append(kn * r * gk_ref[:, h * 256:h * 256 + 128])
        ks.append(tail)
    return jnp.concatenate(ks, axis=-1), kv[:, MLA_HEADS * MLA_NOPE:]


def _proj_kernel(*refs, latent):
    (x_ref, mod_ref, gmix_ref, win_ref, wq_ref, wkv_ref, gqna_ref, gkna_ref, gql_ref, gkvl_ref,
     gqm_ref, gkm_ref, gqw_ref, gkw_ref, gqp_ref, gkp_ref, seg_ref) = refs[:17]
    pos = 17
    rope = None
    if latent:
        rope = (refs[17][...], refs[18][...])
        pos = 19
    (qna_o, kna_o, vna_o, qm_o, km_o, vm_o, qw_o, kw_o, vw_o, fv_o) = refs[pos:pos + 10]
    cache_o = refs[pos + 10:]

    x = x_ref[...]
    rinv = lax.rsqrt(jnp.mean(x * x, axis=-1, keepdims=True) + EPS)
    h = (x * rinv) * (gmix_ref[...] * (1.0 + mod_ref[0, 1:2, :])) + mod_ref[0, 0:1, :]
    hb = h.astype(BF16)

    projected = _dot(hb, win_ref[...])

    def proj(c0, c1):
        return projected[:, c0:c1]

    qna_o[...] = (_seg_rms64(proj(C_AQ, C_AK), seg_ref) * gqna_ref[...]).astype(BF16)
    k_na = _seg_rms64(proj(C_AK, C_AV), seg_ref) * gkna_ref[...]
    kna_o[...] = k_na.astype(BF16)
    v_na = proj(C_AV, C_MQ)
    vna_o[...] = v_na.astype(BF16)

    cq = (_rms(proj(C_MQ, C_CKV), MLA_Q_LORA) * gql_ref[...]).astype(BF16)
    qu = _dot(cq, wq_ref[...] if latent else wq_ref[:, :MLA_HEADS * 256])
    qs = []
    for hd in range(MLA_HEADS):
        blk = qu[:, hd * 256:(hd + 1) * 256]
        rinv = lax.rsqrt(jnp.sum(blk * blk, axis=-1, keepdims=True) * (1.0 / MLA_QK) + EPS)
        blk = blk * rinv * gqm_ref[:, hd * 256:(hd + 1) * 256]
        if latent:
            ps = slice(MLA_HEADS * 256 + hd * LANES, MLA_HEADS * 256 + (hd + 1) * LANES)
            partner = qu[:, ps] * rinv * gqp_ref[:, hd * LANES:(hd + 1) * LANES]
            blk = jnp.concatenate([blk[:, :LANES], blk[:, LANES:] * rope[0] + partner * rope[1]], axis=-1)
        qs.append(blk)
    qm_o[...] = jnp.concatenate(qs, axis=-1).astype(BF16)
    ckv_n = _rms(proj(C_CKV, C_KR), MLA_KV_LORA) * gkvl_ref[...]

    tail = proj(C_KR, IN_COLS_P)
    lo = lax.broadcasted_iota(jnp.int32, (x.shape[0], LANES), 1) < HEAD_DIM
    nslab = (IN_COLS_P - C_KR) // LANES
    slabs = [tail[:, c * LANES:(c + 1) * LANES] for c in range(nslab)]
    swapped = [pltpu.roll(t, HEAD_DIM, 1) for t in slabs]
    al = [jnp.where(lo, swapped[c], swapped[c + 1]) for c in range(nslab - 1)]
    kr = jnp.where(lo, slabs[0], 0.0)
    k_m, v_m = _mla_kv_heads(ckv_n, kr, wkv_ref, gkm_ref, gkp_ref, rope)
    km_o[...] = k_m.astype(BF16)
    vm_o[...] = v_m.astype(BF16)

    def dup(t):
        sw = pltpu.roll(t, HEAD_DIM, 1)
        return jnp.concatenate([jnp.where(lo, t, sw), jnp.where(lo, sw, t)], axis=-1)

    q_w = _seg_rms64(jnp.concatenate(al[0:4], axis=-1), seg_ref) * gqw_ref[...]
    k_w = _seg_rms64(al[4], seg_ref) * gkw_ref[...]
    v_w = al[5]
    if latent:
        q_w = _rope_wide(q_w, *rope)
        k_w = _rope128(k_w, *rope)
    qw_o[...] = q_w.astype(BF16)
    kw_o[...] = dup(k_w).astype(BF16)
    vw_o[...] = dup(v_w).astype(BF16)

    fv_o[...] = jnp.concatenate(al[6:10], axis=-1).astype(BF16)

    if not latent:
        kna32_o, vna32_o, ckv32_o, kr32_o, kw32_o, vw32_o = cache_o
        kna32_o[...] = k_na
        vna32_o[...] = v_na
        ckv32_o[...] = ckv_n
        kr32_o[...] = kr[:, :MLA_ROPE]
        kw32_o[...] = k_w
        vw32_o[...] = v_w


def _project(x, mod, layer, p, latent, rope_tabs):
    n = x.shape[0]
    tm = TM_PROJ
    row = lambda i: (i, 0)
    if latent:
        mod_map = lambda i: (layer * MOD_ROWS + 1 + (i * tm) // DEC_SEQ, 0, 0)
    else:
        mod_map = lambda i: (layer * MOD_ROWS, 0, 0)
    in_specs = [
        pl.BlockSpec((tm, D_MODEL), row),
        pl.BlockSpec((1, MOD_CHUNKS, D_MODEL), mod_map),
        _layer_spec((1, D_MODEL), layer),
        _layer_spec((D_MODEL, IN_COLS_P), layer, resident=True),
        _layer_spec((MLA_Q_LORA, MLA_HEADS * (256 + LANES)), layer, resident=True),
        _layer_spec((MLA_KV_LORA, 1024), layer, resident=True),
    ] + [_layer_spec((1, wd), layer) for wd in (512, 512, MLA_Q_LORA, MLA_KV_LORA, 1024, 1024, 512, 128, 512, 512)]
    in_specs.append(_const_spec((LANES, LANES)))
    args = [x, mod, p['g_mix'], p['w_in'], p['w_q_up'], p['w_kv_up'], p['gq_na'], p['gk_na'], p['g_q_lora'],
            p['g_kv_lora'], p['gq_mla'], p['gk_mla'], p['gq_win'], p['gk_win'], p['gqp_mla'], p['gkp_mla'],
            p['seg']]
    if latent:
        nt = DEC_SEQ // tm
        in_specs += [pl.BlockSpec((tm, LANES), lambda i: (i % nt, 0))] * 2
        args += list(rope_tabs)
    widths = [512, 512, 512, 1024, 1024, 512, 512, 256, 256, 512]
    out_specs = [pl.BlockSpec((tm, w), row) for w in widths]
    out_shape = [jax.ShapeDtypeStruct((n, w), BF16) for w in widths]
    if not latent:
        cache_w = [512, 512, MLA_KV_LORA, MLA_ROPE, 128, 128]
        out_specs += [pl.BlockSpec((tm, w), row) for w in cache_w]
        out_shape += [jax.ShapeDtypeStruct((n, w), F32) for w in cache_w]
    return pl.pallas_call(
        functools.partial(_proj_kernel, latent=latent),
        grid=(n // tm,),
        in_specs=in_specs,
        out_specs=out_specs,
        out_shape=out_shape,
        compiler_params=_cp(("arbitrary",)),
        name="proj_latent" if latent else "proj_context",
    )(*args)


def _mla_ctx_kernel(ckv_ref, kr_ref, wkv_ref, gk_ref, k_o, v_o):
    k_m, v_m = _mla_kv_heads(ckv_ref[...], kr_ref[...], wkv_ref, gk_ref, None, None)
    k_o[...] = k_m.astype(BF16)
    v_o[...] = v_m.astype(BF16)


def _mla_ctx(ckv, kr128, p, layer, nb):
    n = nb * PAST_LEN
    src = lambda b: (b * DEPTH + layer, 0)
    row = lambda b: (b, 0)
    return pl.pallas_call(
        _mla_ctx_kernel,
        grid=(nb,),
        in_specs=[pl.BlockSpec((PAST_LEN, MLA_KV_LORA), src), pl.BlockSpec((PAST_LEN, LANES), src),
                  _layer_spec((MLA_KV_LORA, 1024), layer), _layer_spec((1, 1024), layer)],
        out_specs=[pl.BlockSpec((PAST_LEN, 1024), row), pl.BlockSpec((PAST_LEN, 512), row)],
        out_shape=[jax.ShapeDtypeStruct((n, 1024), BF16), jax.ShapeDtypeStruct((n, 512), BF16)],
        compiler_params=_cp(("arbitrary",)),
        name="mla_cached_kv",
    )(ckv, kr128, p['w_kv_up'], p['gk_mla'])


def _softmax_pv(scores, values, sink=None):
    m = functools.reduce(jnp.maximum, [jnp.max(s, axis=-1, keepdims=True) for s in scores])
    if sink is not None:
        m = jnp.maximum(m, sink)
    ps = [jnp.exp2(s - m) for s in scores]
    l = functools.reduce(lambda a, b: a + b, [jnp.sum(p, axis=-1, keepdims=True) for p in ps])
    if sink is not None:
        l = l + jnp.exp2(sink - m)
    o = functools.reduce(lambda a, b: a + b, [_dot(p.astype(BF16), v) for p, v in zip(ps, values)])
    return o / l


def _half_masks():
    lane = lax.broadcasted_iota(jnp.int32, (1, LANES), 1)
    lo = lane < HEAD_DIM
    return jnp.where(lo, 1.0, 0.0).astype(BF16), jnp.where(lo, 0.0, 1.0).astype(BF16)


def _pair_attention(q2, keys, values, extra=None, sinks=None):
    rows = q2.shape[0]
    lo = lax.broadcasted_iota(jnp.int32, (rows, LANES), 1) < HEAD_DIM
    outs = []
    for e, msk in enumerate(_half_masks()):
        qm = q2 * msk
        scores = []
        for idx, k2 in enumerate(keys):
            s = _dot_nt(qm, k2)
            if extra is not None:
                s = extra(e, idx, s)
            scores.append(s)
        sink = None
        if sinks is not None:
            sink = jnp.full((rows, 1), sinks[e], F32)
        outs.append(_softmax_pv(scores, values, sink))
    return jnp.where(lo, outs[0], outs[1])


def _dft_real(x, cc_ref, sc_ref, cs, ss, scale):
    outs = []
    for g in range(FN_GROUPS):
        xg = x[:, g * FN_CH:(g + 1) * FN_CH]
        xc = _dot(xg, cc_ref[...]).astype(BF16)
        xs = _dot(xg, sc_ref[...]).astype(BF16)
        outs.append(_dot(cs, xc) - _dot(ss, xs))
    return jnp.concatenate(outs, axis=-1) * scale


def _ctx_attn_kernel(sink_ref, qna, kna, vna, qm, km, vm, qw, kw, vw, fv, cs_ref, ss_ref, cc_ref, sc_ref,
                     oa, ob, oc, od, *, layer):
    for hp in range(4):
        cs = slice(hp * LANES, (hp + 1) * LANES)
        oa[:, cs] = _pair_attention(qna[:, cs], [kna[:, cs]], [vna[:, cs]]).astype(BF16)
    for hd in range(MLA_HEADS):
        s = _dot_nt(qm[:, hd * 256:(hd + 1) * 256], km[:, hd * 256:(hd + 1) * 256])
        ob[:, hd * 128:(hd + 1) * 128] = _softmax_pv([s], [vm[:, hd * 128:(hd + 1) * 128]]).astype(BF16)
    for hp in range(4):
        cs = slice(hp * LANES, (hp + 1) * LANES)
        kcs = slice((hp // 2) * LANES, (hp // 2 + 1) * LANES)
        oc[:, cs] = _pair_attention(qw[:, cs], [kw[:, kcs]], [vw[:, kcs]],
                                    sinks=(sink_ref[layer, 2 * hp], sink_ref[layer, 2 * hp + 1])).astype(BF16)
    od[...] = _dft_real(fv[...], cc_ref, sc_ref, cs_ref[...], ss_ref[...],
                        1.0 / math.sqrt(SEQ * FN_CH)).astype(BF16)


def _ctx_attention(pr, sink, dft, layer):
    qna, kna, vna, qm, km, vm, qw, kw, vw, fv = pr[:10]
    n = qna.shape[0]
    row = lambda b: (b, 0)
    widths = [512, 512, 512, 1024, 1024, 512, 512, 256, 256, 512]
    in_specs = [pl.BlockSpec(memory_space=pltpu.SMEM)]
    in_specs += [pl.BlockSpec((SEQ, w), row) for w in widths]
    in_specs += [_const_spec((SEQ, SEQ)), _const_spec((SEQ, SEQ)), _const_spec((FN_CH, FN_CH)),
                 _const_spec((FN_CH, FN_CH))]
    return pl.pallas_call(
        functools.partial(_ctx_attn_kernel, layer=layer),
        grid=(n // SEQ,),
        in_specs=in_specs,
        out_specs=[pl.BlockSpec((SEQ, 512), row)] * 4,
        out_shape=[jax.ShapeDtypeStruct((n, 512), BF16)] * 4,
        compiler_params=_cp(("arbitrary",)),
        name="context_attention",
    )(sink, qna, kna, vna, qm, km, vm, qw, kw, vw, fv, dft['cs_ctx'], dft['ss_ctx'], dft['cc'], dft['sc'])


def _na_kernel(q_ref, k_ref, v_ref, kc_ref, vc_ref, tl_ref, tr_ref, o_ref):
    qt = pl.program_id(1)
    rows = DEC_SEQ // GRID_W
    qrows = TQ // GRID_W
    kstart = jnp.clip(qt * qrows - NA_KH // 2, 0, rows - NA_KROWS)
    ks = pl.multiple_of(kstart * GRID_W, GRID_W)
    nk = NA_KROWS * GRID_W

    blk_idx = []
    for i in range(qrows):
        r = qt * qrows + i
        rs = jnp.clip(r - NA_KH // 2, 0, rows - NA_KH)
        blk_idx.append([jnp.where((kstart + j >= rs) & (kstart + j < rs + NA_KH), kstart + j - r + NA_KH - 1,
                                  NA_MASKED) for j in range(NA_KROWS)])

    def bias(h):
        return jnp.concatenate([
            jnp.concatenate([tl_ref[h, blk_idx[i][2 * jp]] + tr_ref[h, blk_idx[i][2 * jp + 1]]
                             for jp in range(NA_KROWS // 2)], axis=-1)
            for i in range(qrows)], axis=0)

    for hp in range(4):
        cs = slice(hp * LANES, (hp + 1) * LANES)

        def extra(e, idx, s, hp=hp):
            return s + bias(2 * hp + e) if idx == 0 else s

        o = _pair_attention(q_ref[:, cs], [k_ref[pl.ds(ks, nk), cs], kc_ref[:, cs]],
                            [v_ref[pl.ds(ks, nk), cs], vc_ref[:, cs]], extra=extra)
        o_ref[:, cs] = o.astype(BF16)


def _lat_specs(width_q, width_k, width_v, ctx_map, tq=TQ):
    nq = DEC_SEQ // tq
    ctx_nd = len(ctx_map(0, 0))
    lead = (None,) * (ctx_nd - 2)
    return [
        pl.BlockSpec((tq, width_q), lambda b, t: (b * nq + t, 0)),
        pl.BlockSpec((DEC_SEQ, width_k), lambda b, t: (b, 0)),
        pl.BlockSpec((DEC_SEQ, width_v), lambda b, t: (b, 0)),
        pl.BlockSpec(lead + (PAST_LEN, width_k), ctx_map),
        pl.BlockSpec(lead + (PAST_LEN, width_v), ctx_map),
    ]


def _lat_call(kernel, name, args, in_specs, nb, tq=TQ):
    nq = DEC_SEQ // tq
    return pl.pallas_call(
        kernel,
        grid=(nb, nq),
        in_specs=in_specs,
        out_specs=pl.BlockSpec((tq, 512), lambda b, t: (b * nq + t, 0)),
        out_shape=jax.ShapeDtypeStruct((nb * DEC_SEQ, 512), BF16),
        compiler_params=_cp(("arbitrary", "arbitrary")),
        name=name,
    )(*args)


def _na_attention(q, k, v, kc, vc, bias_tabs, nb, layer):
    specs = _lat_specs(512, 512, 512, lambda b, t: (b, layer, 0, 0))
    specs += [_layer_spec(bias_tabs[0].shape[1:], layer, resident=True)] * 2
    return _lat_call(_na_kernel, "neighbourhood_attention", (q, k, v, kc, vc) + tuple(bias_tabs), specs, nb)


def _mla_kernel(q_ref, k_ref, v_ref, kc_ref, vc_ref, o_ref):
    for hd in range(MLA_HEADS):
        ks = slice(hd * 256, (hd + 1) * 256)
        vs = slice(hd * 128, (hd + 1) * 128)
        q = q_ref[:, ks]
        scores = [_dot_nt(q, k_ref[:, ks]), _dot_nt(q, kc_ref[:, ks])]
        o_ref[:, vs] = _softmax_pv(scores, [v_ref[:, vs], vc_ref[:, vs]]).astype(BF16)


def _mla_attention(q, k, v, kc, vc, nb):
    specs = _lat_specs(1024, 1024, 512, lambda b, t: (b, 0), tq=TQ_MLA)
    return _lat_call(_mla_kernel, "latent_attention", (q, k, v, kc, vc), specs, nb, tq=TQ_MLA)


def _win_kernel(sink_ref, q_ref, k_ref, v_ref, kc_ref, vc_ref, o_ref, *, layer):
    qt = pl.program_id(1)
    nk = 4 * WINDOW
    ks = pl.multiple_of(jnp.clip(qt * TQ - WINDOW, 0, DEC_SEQ - nk), WINDOW)
    qpos = qt * TQ + lax.broadcasted_iota(jnp.int32, (TQ, nk), 0)
    kpos = ks + lax.broadcasted_iota(jnp.int32, (TQ, nk), 1)
    ok = jnp.abs(kpos - qpos) <= WINDOW

    def extra(e, idx, s):
        return jnp.where(ok, s, NEG) if idx == 0 else s

    for hp in range(4):
        cs = slice(hp * LANES, (hp + 1) * LANES)
        kcs = slice((hp // 2) * LANES, (hp // 2 + 1) * LANES)
        o = _pair_attention(q_ref[:, cs], [k_ref[pl.ds(ks, nk), kcs], kc_ref[:, kcs]],
                            [v_ref[pl.ds(ks, nk), kcs], vc_ref[:, kcs]], extra=extra,
                            sinks=(sink_ref[layer, 2 * hp], sink_ref[layer, 2 * hp + 1]))
        o_ref[:, cs] = o.astype(BF16)


def _win_attention(sink, q, k, v, kc, vc, nb, layer):
    specs = [pl.BlockSpec(memory_space=pltpu.SMEM)]
    specs += _lat_specs(512, 256, 256, lambda b, t: (b, layer, 0, 0))
    return _lat_call(functools.partial(_win_kernel, layer=layer), "window_attention", (sink, q, k, v, kc, vc),
                     specs, nb)


def _fourier_kernel(x_ref, cc_ref, sc_ref, cs_ref, ss_ref, o_ref, xc_ref, xs_ref):
    @pl.when(pl.program_id(1) == 0)
    def _():
        for g in range(FN_GROUPS):
            gs = slice(g * FN_CH, (g + 1) * FN_CH)
            xg = x_ref[:, gs]
            xc_ref[:, gs] = _dot(xg, cc_ref[...]).astype(BF16)
            xs_ref[:, gs] = _dot(xg, sc_ref[...]).astype(BF16)

    y = _dot(cs_ref[...], xc_ref[...]) - _dot(ss_ref[...], xs_ref[...])
    o_ref[...] = (y * (1.0 / math.sqrt(DEC_SEQ * FN_CH))).astype(BF16)


def _fourier(x, dft, nb):
    nq = DEC_SEQ // TQ
    return pl.pallas_call(
        _fourier_kernel,
        grid=(nb, nq),
        in_specs=[
            pl.BlockSpec((DEC_SEQ, 512), lambda b, t: (b, 0)),
            _const_spec((FN_CH, FN_CH)), _const_spec((FN_CH, FN_CH)),
            pl.BlockSpec((TQ, DEC_SEQ), lambda b, t: (t, 0)),
            pl.BlockSpec((TQ, DEC_SEQ), lambda b, t: (t, 0)),
        ],
        out_specs=pl.BlockSpec((TQ, 512), lambda b, t: (b * nq + t, 0)),
        out_shape=jax.ShapeDtypeStruct((nb * DEC_SEQ, 512), BF16),
        scratch_shapes=[pltpu.VMEM((DEC_SEQ, 512), BF16), pltpu.VMEM((DEC_SEQ, 512), BF16)],
        compiler_params=_cp(("arbitrary", "arbitrary")),
        name="fourier_mix",
    )(x, dft['cc'], dft['sc'], dft['cs_lat'], dft['ss_lat'])


def _out_kernel(oa, ob, oc, od, x_ref, mod_ref, gffn_ref, w_ref, x1_o, h2_o):
    o = jnp.concatenate([oa[...], ob[...], oc[...], od[...]], axis=-1)
    acc = _dot(o, w_ref[...])
    x1 = x_ref[...] + mod_ref[0, 2:3, :] * acc
    x1_o[...] = x1
    rinv = lax.rsqrt(jnp.mean(x1 * x1, axis=-1, keepdims=True) + EPS)
    h2 = (x1 * rinv) * (gffn_ref[...] * (1.0 + mod_ref[0, 4:5, :])) + mod_ref[0, 3:4, :]
    h2_o[...] = h2.astype(BF16)


def _mod_map(layer, tm, latent):
    if latent:
        return lambda i, *_: (layer * MOD_ROWS + 1 + (i * tm) // DEC_SEQ, 0, 0)
    return lambda i, *_: (layer * MOD_ROWS, 0, 0)


def _out_project(o4, x, mod, layer, p, latent):
    n = x.shape[0]
    tm = TM_OUT
    row = lambda i: (i, 0)
    return pl.pallas_call(
        _out_kernel,
        grid=(n // tm,),
        in_specs=[pl.BlockSpec((tm, 512), row)] * 4 + [
            pl.BlockSpec((tm, D_MODEL), row),
            pl.BlockSpec((1, MOD_CHUNKS, D_MODEL), _mod_map(layer, tm, latent)),
            _layer_spec((1, D_MODEL), layer),
            _layer_spec((D_MODEL, D_MODEL), layer, resident=True),
        ],
        out_specs=[pl.BlockSpec((tm, D_MODEL), row), pl.BlockSpec((tm, D_MODEL), row)],
        out_shape=[jax.ShapeDtypeStruct((n, D_MODEL), F32), jax.ShapeDtypeStruct((n, D_MODEL), BF16)],
        compiler_params=_cp(("arbitrary",)),
        name="out_proj_latent" if latent else "out_proj_context",
    )(*o4, x, mod, p['g_ffn'], p['w_out'])


def _ffn_kernel(h_ref, hp_ref, hn_ref, x1_ref, mod_ref, wg_ref, wu_ref, wc_ref, wd_ref, o_ref, hext_ref, acc_ref,
                *, seq_len):
    i = pl.program_id(0)
    j = pl.program_id(1)
    tm = h_ref.shape[0]

    @pl.when(j == 0)
    def _():
        hext_ref[0:HALO, :] = hp_ref[...]
        hext_ref[HALO:HALO + tm, :] = h_ref[...]
        hext_ref[HALO + tm:, :] = hn_ref[...]
        acc_ref[...] = jnp.zeros_like(acc_ref)

    g_ext = _dot(hext_ref[...], wg_ref[...])
    rows_ext = g_ext.shape[0]
    pos = (i * tm + lax.broadcasted_iota(jnp.int32, (tm, 1), 0)) % seq_len
    g_mid = g_ext[HALO:HALO + tm]
    g_prev = pltpu.roll(g_ext, 1, 0)[HALO:HALO + tm]
    g_next = pltpu.roll(g_ext, rows_ext - 1, 0)[HALO:HALO + tm]
    g_prev = jnp.where(pos == 0, 0.0, g_prev)
    g_next = jnp.where(pos == seq_len - 1, 0.0, g_next)
    g = g_prev * wc_ref[0:1, :] + g_mid * wc_ref[1:2, :] + g_next * wc_ref[2:3, :]
    u = _dot(h_ref[...], wu_ref[...])
    act = (g / (1.0 + jnp.exp(-g))) * u
    acc_ref[...] += _dot(act.astype(BF16), wd_ref[...])

    @pl.when(j == pl.num_programs(1) - 1)
    def _():
        o_ref[...] = x1_ref[...] + mod_ref[0, 5:6, :] * acc_ref[...]


def _ffn(h2, x1, mod, layer, p, latent):
    n = x1.shape[0]
    tm, tf = TM_FFN, TF_FFN
    seq_len = DEC_SEQ if latent else SEQ
    assert n % tm == 0 and tm % HALO == 0 and D_FF % tf == 0
    hb = tm // HALO
    nhalo = n // HALO
    return pl.pallas_call(
        functools.partial(_ffn_kernel, seq_len=seq_len),
        grid=(n // tm, D_FF // tf),
        in_specs=[
            pl.BlockSpec((tm, D_MODEL), lambda i, j: (i, 0)),
            pl.BlockSpec((HALO, D_MODEL), lambda i, j: (jnp.maximum(i * hb - 1, 0), 0)),
            pl.BlockSpec((HALO, D_MODEL), lambda i, j: (jnp.minimum((i + 1) * hb, nhalo - 1), 0)),
            pl.BlockSpec((tm, D_MODEL), lambda i, j: (i, 0)),
            pl.BlockSpec((1, MOD_CHUNKS, D_MODEL), _mod_map(layer, tm, latent)),
            pl.BlockSpec((None, D_MODEL, tf), lambda i, j: (layer, 0, j)),
            pl.BlockSpec((None, D_MODEL, tf), lambda i, j: (layer, 0, j)),
            pl.BlockSpec((None, 8, tf), lambda i, j: (layer, 0, j)),
            pl.BlockSpec((None, tf, D_MODEL), lambda i, j: (layer, j, 0)),
        ],
        out_specs=pl.BlockSpec((tm, D_MODEL), lambda i, j: (i, 0)),
        out_shape=jax.ShapeDtypeStruct((n, D_MODEL), F32),
        scratch_shapes=[pltpu.VMEM((tm + 2 * HALO, D_MODEL), BF16), pltpu.VMEM((tm, D_MODEL), F32)],
        compiler_params=_cp(("arbitrary", "arbitrary")),
        name="ffn_latent" if latent else "ffn_context",
    )(h2, h2, h2, x1, mod, p['w_gate'], p['w_up'], p['w_conv'], p['w_down'])


def _tile_row(g, reps, scale=1.0):
    return (jnp.tile(g, (1, reps)) * scale)[:, None, :]


def _prep(w):
    win_r = jnp.pad(w['w_in'].astype(BF16), ((0, 0), (0, 0), (0, IN_COLS_P - IN_COLS)))

    perm = np.array([j + 16 if j % 32 < 16 else j - 16 for j in range(MLA_ROPE)])
    half_pad = lambda t: jnp.concatenate([t, jnp.zeros(t.shape[:-1] + (LANES - MLA_ROPE,), F32)], axis=-1)
    wq = w['w_q_up'].reshape(DEPTH, MLA_Q_LORA, MLA_HEADS, MLA_QK)
    wq_r = jnp.concatenate([wq, jnp.zeros((DEPTH, MLA_Q_LORA, MLA_HEADS, 256 - MLA_QK), F32)], axis=-1)
    wq_p = half_pad(wq[..., MLA_NOPE:][..., perm])
    wq_r = jnp.concatenate([wq_r.reshape(DEPTH, MLA_Q_LORA, -1), wq_p.reshape(DEPTH, MLA_Q_LORA, -1)],
                           axis=-1).astype(BF16)
    wkv = w['w_kv_up'].reshape(DEPTH, MLA_KV_LORA, MLA_HEADS, 2 * MLA_NOPE)
    wkv_r = jnp.concatenate([wkv[..., :MLA_NOPE].reshape(DEPTH, MLA_KV_LORA, -1),
                             wkv[..., MLA_NOPE:].reshape(DEPTH, MLA_KV_LORA, -1)], axis=-1).astype(BF16)

    def pad_head(g, s):
        return _tile_row(jnp.concatenate([g, jnp.zeros((DEPTH, 256 - MLA_QK), F32)], axis=-1), MLA_HEADS, s)

    def partner_gain(g, s):
        return _tile_row(half_pad(g[:, MLA_NOPE:][:, perm]), MLA_HEADS, s)

    seg = np.kron(np.eye(LANES // HEAD_DIM), np.full((HEAD_DIM, HEAD_DIM), 1.0 / HEAD_DIM)).astype(np.float32)
    wc = jnp.concatenate([w['w_conv'], jnp.zeros((DEPTH, 5, D_FF), F32)], axis=1)
    return dict(
        g_mix=w['g_mix'][:, None, :], g_ffn=w['g_ffn'][:, None, :],
        w_in=win_r, w_q_up=wq_r, w_kv_up=wkv_r,
        gq_na=_tile_row(w['g_qn_na'], 8, HEAD_DIM ** -0.5 * LOG2E), gk_na=_tile_row(w['g_kn_na'], 8),
        g_q_lora=w['g_q_lora'][:, None, :], g_kv_lora=w['g_kv_lora'][:, None, :],
        gq_mla=pad_head(w['g_qn_mla'], MLA_QK ** -0.5 * LOG2E), gk_mla=pad_head(w['g_kn_mla'], 1.0),
        gqp_mla=partner_gain(w['g_qn_mla'], MLA_QK ** -0.5 * LOG2E), gkp_mla=partner_gain(w['g_kn_mla'], 1.0),
        seg=jnp.asarray(seg).astype(BF16),
        gq_win=_tile_row(w['g_qn_win'], 8, HEAD_DIM ** -0.5 * LOG2E), gk_win=_tile_row(w['g_kn_win'], 2),
        sink=w['sink_win'] * LOG2E,
        w_out=w['w_out'].astype(BF16), w_gate=w['w_gate'].astype(BF16), w_up=w['w_up'].astype(BF16),
        w_conv=wc, w_down=w['w_down'].astype(BF16),
    )


def _na_bias_tables(rpb):
    lead = rpb.shape[:-1]
    ext = jnp.concatenate([jnp.repeat(rpb[..., :1], 48, axis=-1), rpb, jnp.repeat(rpb[..., -1:], 48, axis=-1)], axis=-1)
    c = np.arange(GRID_W)
    cs = np.clip(c - NA_KW // 2, 0, GRID_W - NA_KW)
    col_ok = (c[None, :] >= cs[:, None]) & (c[None, :] < cs[:, None] + NA_KW)
    no, yes = np.zeros_like(col_ok), np.ones_like(col_ok)

    def toeplitz(e):
        t = jnp.tile(e, (1,) * len(lead) + (GRID_W + 1,))[..., :GRID_W * LANES].reshape(lead + (GRID_W, LANES))
        return t[..., ::-1, :]

    tabs = []
    for shift, halves in ((0, (col_ok, no)), (GRID_W - 1, (no, col_ok))):
        keep = np.concatenate(halves, axis=1)
        side = np.concatenate((yes, no) if shift == 0 else (no, yes), axis=1)
        fill = jnp.asarray(np.where(side, NEG, 0.0).astype(np.float32))
        t = jnp.where(jnp.asarray(keep), toeplitz(jnp.roll(ext, -shift, axis=-1)), fill)
        tabs.append(jnp.concatenate([t, jnp.broadcast_to(fill, lead[:-1] + (1, GRID_W, LANES))], axis=-3))
    return tuple(tabs)


@functools.lru_cache(maxsize=None)
def _dft_tables():
    def cs(n):
        k = (np.arange(n)[:, None] * np.arange(n)[None, :]) % n
        ang = 2.0 * np.pi * k.astype(np.float64) / n
        return np.cos(ang), np.sin(ang)

    out = {}
    for name, n in (('ctx', SEQ), ('lat', DEC_SEQ)):
        c, s = cs(n)
        out['cs_' + name], out['ss_' + name] = c, s
    out['cc'], out['sc'] = cs(FN_CH)
    return out


@functools.lru_cache(maxsize=None)
def _rope_tables():
    t = np.arange(DEC_SEQ)
    quarter = MLA_ROPE // 4
    inv = ROPE_BASE ** (-np.arange(quarter, dtype=np.float64) / quarter)
    j = np.arange(MLA_ROPE)
    pos = np.where(j[None, :] < MLA_ROPE // 2, (t // GRID_W)[:, None], (t % GRID_W)[:, None]).astype(np.float64)
    ang = pos * inv[j % quarter][None, :]
    sign = np.where((j % 32) < 16, -1.0, 1.0)
    cos = np.cos(ang)
    sin = np.sin(ang) * sign[None, :]
    return np.tile(cos, (1, 2)).astype(np.float32), np.tile(sin, (1, 2)).astype(np.float32)


def kernel(x_prompt, x_sample, cache_na_k, cache_na_v, cache_mla_ckv, cache_mla_krope, cache_win_k, cache_win_v,
           c, c_ctx, w_mod, b_mod, g_mix, g_ffn, w_in, g_qn_na, g_kn_na, rpb_na, g_q_lora, w_q_up, g_kv_lora,
           w_kv_up, g_qn_mla, g_kn_mla, g_qn_win, g_kn_win, sink_win, w_out, w_gate, w_up, w_conv, w_down):
    w = dict(g_mix=g_mix, g_ffn=g_ffn, w_in=w_in, g_qn_na=g_qn_na, g_kn_na=g_kn_na, g_q_lora=g_q_lora,
             w_q_up=w_q_up, g_kv_lora=g_kv_lora, w_kv_up=w_kv_up, g_qn_mla=g_qn_mla, g_kn_mla=g_kn_mla,
             g_qn_win=g_qn_win, g_kn_win=g_kn_win, sink_win=sink_win, w_out=w_out, w_gate=w_gate, w_up=w_up,
             w_conv=w_conv, w_down=w_down)
    nb_ctx, nb_lat = x_prompt.shape[0], x_sample.shape[0]
    dft = {k: jnp.asarray(v, F32).astype(BF16) for k, v in _dft_tables().items()}
    rope_tabs = tuple(jnp.asarray(t) for t in _rope_tables())

    mod = _modulation(jnp.concatenate([c_ctx[None, :], c], axis=0), w_mod, b_mod)

    xp = x_prompt.reshape(nb_ctx * SEQ, D_MODEL)
    xs = x_sample.reshape(nb_lat * DEC_SEQ, D_MODEL)
    caches = []
    p = _prep(w)
    nctx = nb_lat * DEPTH * PAST_LEN
    dup = lambda t: jnp.concatenate([t[..., :64], t[..., :64], t[..., 64:], t[..., 64:]], axis=-1).astype(BF16)
    kna_c = cache_na_k.reshape(nb_lat, DEPTH, PAST_LEN, 512).astype(BF16)
    vna_c = cache_na_v.reshape(nb_lat, DEPTH, PAST_LEN, 512).astype(BF16)
    kw_c = dup(cache_win_k.reshape(nb_lat, DEPTH, PAST_LEN, 128))
    vw_c = dup(cache_win_v.reshape(nb_lat, DEPTH, PAST_LEN, 128))
    ckv_c = cache_mla_ckv.reshape(nctx, MLA_KV_LORA)
    kr_c = jnp.pad(cache_mla_krope.reshape(nctx, MLA_ROPE), ((0, 0), (0, LANES - MLA_ROPE)))
    bias_tabs = _na_bias_tables(rpb_na * LOG2E)
    for l in range(DEPTH):

        pr = _project(xp, mod, l, p, False, None)
        caches.append(pr[10:])
        o4 = _ctx_attention(pr, p['sink'], dft, l)
        x1, h2 = _out_project(o4, xp, mod, l, p, False)
        xp = _ffn(h2, x1, mod, l, p, False)

        qna, kna, vna, qm, km, vm, qw, kw, vw, fv = _project(xs, mod, l, p, True, rope_tabs)
        km_c, vm_c = _mla_ctx(ckv_c, kr_c, p, l, nb_lat)
        o_a = _na_attention(qna, kna, vna, kna_c, vna_c, bias_tabs, nb_lat, l)
        o_b = _mla_attention(qm, km, vm, km_c, vm_c, nb_lat)
        o_c = _win_attention(p['sink'], qw, kw, vw, kw_c, vw_c, nb_lat, l)
        o_d = _fourier(fv, dft, nb_lat)
        x1, h2 = _out_project((o_a, o_b, o_c, o_d), xs, mod, l, p, True)
        xs = _ffn(h2, x1, mod, l, p, True)

    def stack(idx, shape):
        return jnp.stack([caches[l][idx].reshape((nb_ctx, SEQ) + shape) for l in range(DEPTH)], axis=1)

    return (xp.reshape(nb_ctx, SEQ, D_MODEL), xs.reshape(nb_lat, DEC_SEQ, D_MODEL),
            stack(0, (8, HEAD_DIM)), stack(1, (8, HEAD_DIM)), stack(2, (MLA_KV_LORA,)), stack(3, (MLA_ROPE,)),
            stack(4, (2, HEAD_DIM)), stack(5, (2, HEAD_DIM)))
```

```python
import functools
import math

import numpy as np
import jax
import jax.numpy as jnp
from jax import lax
from jax.experimental import pallas as pl
from jax.experimental.pallas import tpu as pltpu

F32 = jnp.float32
BF16 = jnp.bfloat16

D_MODEL = 2048
DEPTH = 2
SEQ = 256
DEC_SEQ = 2048
PAST_LEN = 512
GRID_W = 64
HEAD_DIM = 64
GROUP_W = 512
MLA_HEADS = 4
MLA_NOPE = 128
MLA_ROPE = 64
MLA_QK = MLA_NOPE + MLA_ROPE
MLA_Q_LORA = 384
MLA_KV_LORA = 128
WINDOW = 128
FN_GROUPS = 4
FN_CH = 128
D_FF = 5632
MOD_CHUNKS = 6
ROPE_BASE = 10000.0
EPS = 1e-6
NEG = -1e30
LOG2E = math.log2(math.e)

LANES = 128
MOD_ROWS = 8
VMEM_LIMIT = 56 * 1024 * 1024

C_AQ, C_AK, C_AV = 0, 512, 1024
C_MQ = 1536
C_CKV = 1920
C_KR = 2048
IN_COLS = 3392
IN_COLS_P = 3456

TM_PROJ = 512
TM_OUT = 512
TM_FFN = 512
TF_FFN = 512
HALO = 16
TQ = 256
TQ_MLA = 512
NA_KH = 8
NA_KW = 16
NA_KROWS = 12
NA_MASKED = 15


def _cp(sem):
    return pltpu.CompilerParams(dimension_semantics=sem, vmem_limit_bytes=VMEM_LIMIT)


def _const_spec(shape, resident=False):
    n = len(shape)
    mode = dict(pipeline_mode=pl.Buffered(1)) if resident else {}
    return pl.BlockSpec(shape, lambda *_: (0,) * n, **mode)


def _layer_spec(shape, layer, resident=False):
    n = len(shape)
    mode = dict(pipeline_mode=pl.Buffered(1)) if resident else {}
    return pl.BlockSpec((None,) + tuple(shape), lambda *_: (layer,) + (0,) * n, **mode)


def _dot(a, b):
    return jnp.dot(a, b, preferred_element_type=F32)


def _dot_nt(a, b):
    return lax.dot_general(a, b, (((1,), (1,)), ((), ())), preferred_element_type=F32)


def _rms(t, n):
    return t * lax.rsqrt(jnp.sum(t * t, axis=-1, keepdims=True) * (1.0 / n) + EPS)


def _seg_rms64(t, seg_ref):
    outs = []
    for c in range(t.shape[1] // LANES):
        tc = t[:, c * LANES:(c + 1) * LANES]
        ms = _dot((tc * tc).astype(BF16), seg_ref[...])
        outs.append(tc * lax.rsqrt(ms + EPS))
    return jnp.concatenate(outs, axis=-1)


def _partner128(t):
    first = (lax.broadcasted_iota(jnp.int32, t.shape, 1) % 32) < 16
    return jnp.where(first, pltpu.roll(t, LANES - 16, 1), pltpu.roll(t, 16, 1))


def _rope128(t, cos, sin_signed):
    return t * cos + _partner128(t) * sin_signed


def _rope_wide(t, cos, sin_signed):
    return jnp.concatenate(
        [_rope128(t[:, c * LANES:(c + 1) * LANES], cos, sin_signed) for c in range(t.shape[1] // LANES)], axis=-1)


MOD_TN = 1024
MOD_LHS_ROWS = 16


def _split_bf16(t):
    hi = t.astype(BF16)
    return hi, (t - hi.astype(F32)).astype(BF16)


def _mod_kernel(c_ref, w_ref, b_ref, o_ref):
    c = c_ref[...]
    x_hi, x_lo = _split_bf16(c / (1.0 + jnp.exp(-c)))
    w_hi, w_lo = _split_bf16(w_ref[...])
    acc = _dot(x_hi, w_hi) + _dot(x_lo, w_hi) + _dot(x_hi, w_lo)
    o_ref[...] = acc[:MOD_ROWS] + b_ref[...]


def _modulation(cvecs, w_mod, b_mod):
    nvec = cvecs.shape[0]
    c = jnp.zeros((MOD_LHS_ROWS, D_MODEL), F32).at[:nvec].set(cvecs)
    ncol = w_mod.shape[2]
    out = pl.pallas_call(
        _mod_kernel,
        grid=(DEPTH, ncol // MOD_TN),
        in_specs=[
            _const_spec((MOD_LHS_ROWS, D_MODEL)),
            pl.BlockSpec((None, D_MODEL, MOD_TN), lambda l, j: (l, 0, j)),
            pl.BlockSpec((None, 1, MOD_TN), lambda l, j: (l, 0, j)),
        ],
        out_specs=pl.BlockSpec((None, MOD_ROWS, MOD_TN), lambda l, j: (l, 0, j)),
        out_shape=jax.ShapeDtypeStruct((DEPTH, MOD_ROWS, ncol), F32),
        compiler_params=_cp(("arbitrary", "arbitrary")),
        name="modulation",
    )(c, w_mod, b_mod.reshape(DEPTH, 1, ncol))
    return out.reshape(DEPTH * MOD_ROWS, MOD_CHUNKS, D_MODEL)


def _mla_kv_heads(ckv_n, kr, wkv_ref, gk_ref, gkp_ref, rope):
    kv = _dot(ckv_n.astype(BF16), wkv_ref[...])
    ss_r = jnp.sum(kr * kr, axis=-1, keepdims=True)
    pk = _partner128(kr) if rope is not None else None
    ks = []
    for h in range(MLA_HEADS):
        kn = kv[:, h * MLA_NOPE:(h + 1) * MLA_NOPE]
        ms = (jnp.sum(kn * kn, axis=-1, keepdims=True) + ss_r) * (1.0 / MLA_QK)
        r = lax.rsqrt(ms + EPS)
        tail = kr * r * gk_ref[:, h * 256 + 128:(h + 1) * 256]
        if rope is not None:
            tail = tail * rope[0] + (pk * r * gkp_ref[:, h * LANES:(h + 1) * LANES]) * rope[1]
        ks.append(kn * r * gk_ref[:, h * 256:h * 256 + 128])
        ks.append(tail)
    return jnp.concatenate(ks, axis=-1), kv[:, MLA_HEADS * MLA_NOPE:]


def _proj_kernel(*refs, latent):
    (x_ref, mod_ref, gmix_ref, win_ref, wq_ref, wkv_ref, gqna_ref, gkna_ref, gql_ref, gkvl_ref,
     gqm_ref, gkm_ref, gqw_ref, gkw_ref, gqp_ref, gkp_ref, seg_ref) = refs[:17]
    pos = 17
    rope = None
    if latent:
        rope = (refs[17][...], refs[18][...])
        pos = 19
    (qna_o, kna_o, vna_o, qm_o, km_o, vm_o, qw_o, kw_o, vw_o, fv_o) = refs[pos:pos + 10]
    cache_o = refs[pos + 10:]

    x = x_ref[...]
    rinv = lax.rsqrt(jnp.mean(x * x, axis=-1, keepdims=True) + EPS)
    h = (x * rinv) * (gmix_ref[...] * (1.0 + mod_ref[0, 1:2, :])) + mod_ref[0, 0:1, :]
    hb = h.astype(BF16)

    projected = _dot(hb, win_ref[...])

    def proj(c0, c1):
        return projected[:, c0:c1]

    qna_o[...] = (_seg_rms64(proj(C_AQ, C_AK), seg_ref) * gqna_ref[...]).astype(BF16)
    k_na = _seg_rms64(proj(C_AK, C_AV), seg_ref) * gkna_ref[...]
    kna_o[...] = k_na.astype(BF16)
    v_na = proj(C_AV, C_MQ)
    vna_o[...] = v_na.astype(BF16)

    cq = (_rms(proj(C_MQ, C_CKV), MLA_Q_LORA) * gql_ref[...]).astype(BF16)
    qu = _dot(cq, wq_ref[...] if latent else wq_ref[:, :MLA_HEADS * 256])
    qs = []
    for hd in range(MLA_HEADS):
        blk = qu[:, hd * 256:(hd + 1) * 256]
        rinv = lax.rsqrt(jnp.sum(blk * blk, axis=-1, keepdims=True) * (1.0 / MLA_QK) + EPS)
        blk = blk * rinv * gqm_ref[:, hd * 256:(hd + 1) * 256]
        if latent:
            ps = slice(MLA_HEADS * 256 + hd * LANES, MLA_HEADS * 256 + (hd + 1) * LANES)
            partner = qu[:, ps] * rinv * gqp_ref[:, hd * LANES:(hd + 1) * LANES]
            blk = jnp.concatenate([blk[:, :LANES], blk[:, LANES:] * rope[0] + partner * rope[1]], axis=-1)
        qs.append(blk)
    qm_o[...] = jnp.concatenate(qs, axis=-1).astype(BF16)
    ckv_n = _rms(proj(C_CKV, C_KR), MLA_KV_LORA) * gkvl_ref[...]

    tail = proj(C_KR, IN_COLS_P)
    lo = lax.broadcasted_iota(jnp.int32, (x.shape[0], LANES), 1) < HEAD_DIM
    nslab = (IN_COLS_P - C_KR) // LANES
    slabs = [tail[:, c * LANES:(c + 1) * LANES] for c in range(nslab)]
    swapped = [pltpu.roll(t, HEAD_DIM, 1) for t in slabs]
    al = [jnp.where(lo, swapped[c], swapped[c + 1]) for c in range(nslab - 1)]
    kr = jnp.where(lo, slabs[0], 0.0)
    k_m, v_m = _mla_kv_heads(ckv_n, kr, wkv_ref, gkm_ref, gkp_ref, rope)
    km_o[...] = k_m.astype(BF16)
    vm_o[...] = v_m.astype(BF16)

    def dup(t):
        sw = pltpu.roll(t, HEAD_DIM, 1)
        return jnp.concatenate([jnp.where(lo, t, sw), jnp.where(lo, sw, t)], axis=-1)

    q_w = _seg_rms64(jnp.concatenate(al[0:4], axis=-1), seg_ref) * gqw_ref[...]
    k_w = _seg_rms64(al[4], seg_ref) * gkw_ref[...]
    v_w = al[5]
    if latent:
        q_w = _rope_wide(q_w, *rope)
        k_w = _rope128(k_w, *rope)
    qw_o[...] = q_w.astype(BF16)
    kw_o[...] = dup(k_w).astype(BF16)
    vw_o[...] = dup(v_w).astype(BF16)

    fv_o[...] = jnp.concatenate(al[6:10], axis=-1).astype(BF16)

    if not latent:
        kna32_o, vna32_o, ckv32_o, kr32_o, kw32_o, vw32_o = cache_o
        kna32_o[...] = k_na
        vna32_o[...] = v_na
        ckv32_o[...] = ckv_n
        kr32_o[...] = kr[:, :MLA_ROPE]
        kw32_o[...] = k_w
        vw32_o[...] = v_w


def _project(x, mod, layer, p, latent, rope_tabs):
    n = x.shape[0]
    tm = TM_PROJ
    row = lambda i: (i, 0)
    if latent:
        mod_map = lambda i: (layer * MOD_ROWS + 1 + (i * tm) // DEC_SEQ, 0, 0)
    else:
        mod_map = lambda i: (layer * MOD_ROWS, 0, 0)
    in_specs = [
        pl.BlockSpec((tm, D_MODEL), row),
        pl.BlockSpec((1, MOD_CHUNKS, D_MODEL), mod_map),
        _layer_spec((1, D_MODEL), layer),
        _layer_spec((D_MODEL, IN_COLS_P), layer, resident=True),
        _layer_spec((MLA_Q_LORA, MLA_HEADS * (256 + LANES)), layer, resident=True),
        _layer_spec((MLA_KV_LORA, 1024), layer, resident=True),
    ] + [_layer_spec((1, wd), layer) for wd in (512, 512, MLA_Q_LORA, MLA_KV_LORA, 1024, 1024, 512, 128, 512, 512)]
    in_specs.append(_const_spec((LANES, LANES)))
    args = [x, mod, p['g_mix'], p['w_in'], p['w_q_up'], p['w_kv_up'], p['gq_na'], p['gk_na'], p['g_q_lora'],
            p['g_kv_lora'], p['gq_mla'], p['gk_mla'], p['gq_win'], p['gk_win'], p['gqp_mla'], p['gkp_mla'],
            p['seg']]
    if latent:
        nt = DEC_SEQ // tm
        in_specs += [pl.BlockSpec((tm, LANES), lambda i: (i % nt, 0))] * 2
        args += list(rope_tabs)
    widths = [512, 512, 512, 1024, 1024, 512, 512, 256, 256, 512]
    out_specs = [pl.BlockSpec((tm, w), row) for w in widths]
    out_shape = [jax.ShapeDtypeStruct((n, w), BF16) for w in widths]
    if not latent:
        cache_w = [512, 512, MLA_KV_LORA, MLA_ROPE, 128, 128]
        out_specs += [pl.BlockSpec((tm, w), row) for w in cache_w]
        out_shape += [jax.ShapeDtypeStruct((n, w), F32) for w in cache_w]
    return pl.pallas_call(
        functools.partial(_proj_kernel, latent=latent),
        grid=(n // tm,),
        in_specs=in_specs,
        out_specs=out_specs,
        out_shape=out_shape,
        compiler_params=_cp(("arbitrary",)),
        name="proj_latent" if latent else "proj_context",
    )(*args)


def _mla_ctx_kernel(ckv_ref, kr_ref, wkv_ref, gk_ref, k_o, v_o):
    k_m, v_m = _mla_kv_heads(ckv_ref[...], kr_ref[...], wkv_ref, gk_ref, None, None)
    k_o[...] = k_m.astype(BF16)
    v_o[...] = v_m.astype(BF16)


def _mla_ctx(ckv, kr128, p, layer, nb):
    n = nb * PAST_LEN
    src = lambda b: (b * DEPTH + layer, 0)
    row = lambda b: (b, 0)
    return pl.pallas_call(
        _mla_ctx_kernel,
        grid=(nb,),
        in_specs=[pl.BlockSpec((PAST_LEN, MLA_KV_LORA), src), pl.BlockSpec((PAST_LEN, LANES), src),
                  _layer_spec((MLA_KV_LORA, 1024), layer), _layer_spec((1, 1024), layer)],
        out_specs=[pl.BlockSpec((PAST_LEN, 1024), row), pl.BlockSpec((PAST_LEN, 512), row)],
        out_shape=[jax.ShapeDtypeStruct((n, 1024), BF16), jax.ShapeDtypeStruct((n, 512), BF16)],
        compiler_params=_cp(("arbitrary",)),
        name="mla_cached_kv",
    )(ckv, kr128, p['w_kv_up'], p['gk_mla'])


def _softmax_pv(scores, values, sink=None):
    m = functools.reduce(jnp.maximum, [jnp.max(s, axis=-1, keepdims=True) for s in scores])
    if sink is not None:
        m = jnp.maximum(m, sink)
    ps = [jnp.exp2(s - m) for s in scores]
    l = functools.reduce(lambda a, b: a + b, [jnp.sum(p, axis=-1, keepdims=True) for p in ps])
    if sink is not None:
        l = l + jnp.exp2(sink - m)
    o = functools.reduce(lambda a, b: a + b, [_dot(p.astype(BF16), v) for p, v in zip(ps, values)])
    return o / l


def _half_masks():
    lane = lax.broadcasted_iota(jnp.int32, (1, LANES), 1)
    lo = lane < HEAD_DIM
    return jnp.where(lo, 1.0, 0.0).astype(BF16), jnp.where(lo, 0.0, 1.0).astype(BF16)


def _pair_attention(q2, keys, values, extra=None, sinks=None):
    rows = q2.shape[0]
    lo = lax.broadcasted_iota(jnp.int32, (rows, LANES), 1) < HEAD_DIM
    outs = []
    for e, msk in enumerate(_half_masks()):
        qm = q2 * msk
        scores = []
        for idx, k2 in enumerate(keys):
            s = _dot_nt(qm, k2)
            if extra is not None:
                s = extra(e, idx, s)
            scores.append(s)
        sink = None
        if sinks is not None:
            sink = jnp.full((rows, 1), sinks[e], F32)
        outs.append(_softmax_pv(scores, values, sink))
    return jnp.where(lo, outs[0], outs[1])


def _dft_real(x, cc_ref, sc_ref, cs, ss, scale):
    outs = []
    for g in range(FN_GROUPS):
        xg = x[:, g * FN_CH:(g + 1) * FN_CH]
        xc = _dot(xg, cc_ref[...]).astype(BF16)
        xs = _dot(xg, sc_ref[...]).astype(BF16)
        outs.append(_dot(cs, xc) - _dot(ss, xs))
    return jnp.concatenate(outs, axis=-1) * scale


def _ctx_attn_kernel(sink_ref, qna, kna, vna, qm, km, vm, qw, kw, vw, fv, cs_ref, ss_ref, cc_ref, sc_ref,
                     oa, ob, oc, od, *, layer):
    for hp in range(4):
        cs = slice(hp * LANES, (hp + 1) * LANES)
        oa[:, cs] = _pair_attention(qna[:, cs], [kna[:, cs]], [vna[:, cs]]).astype(BF16)
    for hd in range(MLA_HEADS):
        s = _dot_nt(qm[:, hd * 256:(hd + 1) * 256], km[:, hd * 256:(hd + 1) * 256])
        ob[:, hd * 128:(hd + 1) * 128] = _softmax_pv([s], [vm[:, hd * 128:(hd + 1) * 128]]).astype(BF16)
    for hp in range(4):
        cs = slice(hp * LANES, (hp + 1) * LANES)
        kcs = slice((hp // 2) * LANES, (hp // 2 + 1) * LANES)
        oc[:, cs] = _pair_attention(qw[:, cs], [kw[:, kcs]], [vw[:, kcs]],
                                    sinks=(sink_ref[layer, 2 * hp], sink_ref[layer, 2 * hp + 1])).astype(BF16)
    od[...] = _dft_real(fv[...], cc_ref, sc_ref, cs_ref[...], ss_ref[...],
                        1.0 / math.sqrt(SEQ * FN_CH)).astype(BF16)


def _ctx_attention(pr, sink, dft, layer):
    qna, kna, vna, qm, km, vm, qw, kw, vw, fv = pr[:10]
    n = qna.shape[0]
    row = lambda b: (b, 0)
    widths = [512, 512, 512, 1024, 1024, 512, 512, 256, 256, 512]
    in_specs = [pl.BlockSpec(memory_space=pltpu.SMEM)]
    in_specs += [pl.BlockSpec((SEQ, w), row) for w in widths]
    in_specs += [_const_spec((SEQ, SEQ)), _const_spec((SEQ, SEQ)), _const_spec((FN_CH, FN_CH)),
                 _const_spec((FN_CH, FN_CH))]
    return pl.pallas_call(
        functools.partial(_ctx_attn_kernel, layer=layer),
        grid=(n // SEQ,),
        in_specs=in_specs,
        out_specs=[pl.BlockSpec((SEQ, 512), row)] * 4,
        out_shape=[jax.ShapeDtypeStruct((n, 512), BF16)] * 4,
        compiler_params=_cp(("arbitrary",)),
        name="context_attention",
    )(sink, qna, kna, vna, qm, km, vm, qw, kw, vw, fv, dft['cs_ctx'], dft['ss_ctx'], dft['cc'], dft['sc'])


def _na_kernel(q_ref, k_ref, v_ref, kc_ref, vc_ref, ext_ref, o_ref, tl_ref, tr_ref):
    qt = pl.program_id(1)

    @pl.when((pl.program_id(0) == 0) & (qt == 0))
    def _():
        col = lax.broadcasted_iota(jnp.int32, (GRID_W, LANES), 0)
        lane = lax.broadcasted_iota(jnp.int32, (GRID_W, LANES), 1)
        kcol = lane % GRID_W
        first = jnp.clip(col - NA_KW // 2, 0, GRID_W - NA_KW)
        in_win = (kcol >= first) & (kcol < first + NA_KW)
        left = lane < GRID_W
        for h in range(tl_ref.shape[0]):
            for d in range(tl_ref.shape[1]):
                row = jnp.broadcast_to(ext_ref[h, d:d + 1, :], (GRID_W, LANES))
                blk_l = pltpu.roll(row, GRID_W + 1, 1, stride=1, stride_axis=0)
                blk_r = pltpu.roll(row, 1, 1, stride=1, stride_axis=0)
                tl_ref[h, d] = jnp.where(left, jnp.where(in_win, blk_l, NEG), 0.0)
                tr_ref[h, d] = jnp.where(left, 0.0, jnp.where(in_win, blk_r, NEG))

    rows = DEC_SEQ // GRID_W
    qrows = TQ // GRID_W
    kstart = jnp.clip(qt * qrows - NA_KH // 2, 0, rows - NA_KROWS)
    ks = pl.multiple_of(kstart * GRID_W, GRID_W)
    nk = NA_KROWS * GRID_W

    blk_idx = []
    for i in range(qrows):
        r = qt * qrows + i
        rs = jnp.clip(r - NA_KH // 2, 0, rows - NA_KH)
        blk_idx.append([jnp.where((kstart + j >= rs) & (kstart + j < rs + NA_KH), kstart + j - r + NA_KH - 1,
                                  NA_MASKED) for j in range(NA_KROWS)])

    def bias(h):
        return jnp.concatenate([
            jnp.concatenate([tl_ref[h, blk_idx[i][2 * jp]] + tr_ref[h, blk_idx[i][2 * jp + 1]]
                             for jp in range(NA_KROWS // 2)], axis=-1)
            for i in range(qrows)], axis=0)

    for hp in range(4):
        cs = slice(hp * LANES, (hp + 1) * LANES)

        def extra(e, idx, s, hp=hp):
            return s + bias(2 * hp + e) if idx == 0 else s

        o = _pair_attention(q_ref[:, cs], [k_ref[pl.ds(ks, nk), cs], kc_ref[:, cs]],
                            [v_ref[pl.ds(ks, nk), cs], vc_ref[:, cs]], extra=extra)
        o_ref[:, cs] = o.astype(BF16)


def _lat_specs(width_q, width_k, width_v, ctx_map, tq=TQ):
    nq = DEC_SEQ // tq
    ctx_nd = len(ctx_map(0, 0))
    lead = (None,) * (ctx_nd - 2)
    return [
        pl.BlockSpec((tq, width_q), lambda b, t: (b * nq + t, 0)),
        pl.BlockSpec((DEC_SEQ, width_k), lambda b, t: (b, 0)),
        pl.BlockSpec((DEC_SEQ, width_v), lambda b, t: (b, 0)),
        pl.BlockSpec(lead + (PAST_LEN, width_k), ctx_map),
        pl.BlockSpec(lead + (PAST_LEN, width_v), ctx_map),
    ]


def _lat_call(kernel, name, args, in_specs, nb, tq=TQ, scratch=()):
    nq = DEC_SEQ // tq
    return pl.pallas_call(
        kernel,
        grid=(nb, nq),
        in_specs=in_specs,
        out_specs=pl.BlockSpec((tq, 512), lambda b, t: (b * nq + t, 0)),
        out_shape=jax.ShapeDtypeStruct((nb * DEC_SEQ, 512), BF16),
        scratch_shapes=list(scratch),
        compiler_params=_cp(("arbitrary", "arbitrary")),
        name=name,
    )(*args)


def _na_attention(q, k, v, kc, vc, bias_ext, nb, layer):
    specs = _lat_specs(512, 512, 512, lambda b, t: (b, layer, 0, 0))
    specs.append(_layer_spec(bias_ext.shape[1:], layer, resident=True))
    tab = pltpu.VMEM(bias_ext.shape[1:3] + (GRID_W, LANES), F32)
    return _lat_call(_na_kernel, "neighbourhood_attention", (q, k, v, kc, vc, bias_ext), specs, nb,
                     scratch=(tab, tab))


def _mla_kernel(q_ref, k_ref, v_ref, kc_ref, vc_ref, o_ref):
    for hd in range(MLA_HEADS):
        ks = slice(hd * 256, (hd + 1) * 256)
        vs = slice(hd * 128, (hd + 1) * 128)
        q = q_ref[:, ks]
        scores = [_dot_nt(q, k_ref[:, ks]), _dot_nt(q, kc_ref[:, ks])]
        o_ref[:, vs] = _softmax_pv(scores, [v_ref[:, vs], vc_ref[:, vs]]).astype(BF16)


def _mla_attention(q, k, v, kc, vc, nb):
    specs = _lat_specs(1024, 1024, 512, lambda b, t: (b, 0), tq=TQ_MLA)
    return _lat_call(_mla_kernel, "latent_attention", (q, k, v, kc, vc), specs, nb, tq=TQ_MLA)


def _win_kernel(sink_ref, q_ref, k_ref, v_ref, kc_ref, vc_ref, o_ref, *, layer):
    qt = pl.program_id(1)
    nk = 4 * WINDOW
    ks = pl.multiple_of(jnp.clip(qt * TQ - WINDOW, 0, DEC_SEQ - nk), WINDOW)
    qpos = qt * TQ + lax.broadcasted_iota(jnp.int32, (TQ, nk), 0)
    kpos = ks + lax.broadcasted_iota(jnp.int32, (TQ, nk), 1)
    ok = jnp.abs(kpos - qpos) <= WINDOW

    def extra(e, idx, s):
        return jnp.where(ok, s, NEG) if idx == 0 else s

    for hp in range(4):
        cs = slice(hp * LANES, (hp + 1) * LANES)
        kcs = slice((hp // 2) * LANES, (hp // 2 + 1) * LANES)
        o = _pair_attention(q_ref[:, cs], [k_ref[pl.ds(ks, nk), kcs], kc_ref[:, kcs]],
                            [v_ref[pl.ds(ks, nk), kcs], vc_ref[:, kcs]], extra=extra,
                            sinks=(sink_ref[layer, 2 * hp], sink_ref[layer, 2 * hp + 1]))
        o_ref[:, cs] = o.astype(BF16)


def _win_attention(sink, q, k, v, kc, vc, nb, layer):
    specs = [pl.BlockSpec(memory_space=pltpu.SMEM)]
    specs += _lat_specs(512, 256, 256, lambda b, t: (b, layer, 0, 0))
    return _lat_call(functools.partial(_win_kernel, layer=layer), "window_attention", (sink, q, k, v, kc, vc),
                     specs, nb)


def _fourier_kernel(x_ref, cc_ref, sc_ref, cs_ref, ss_ref, o_ref, xc_ref, xs_ref):
    @pl.when(pl.program_id(1) == 0)
    def _():
        for g in range(FN_GROUPS):
            gs = slice(g * FN_CH, (g + 1) * FN_CH)
            xg = x_ref[:, gs]
            xc_ref[:, gs] = _dot(xg, cc_ref[...]).astype(BF16)
            xs_ref[:, gs] = _dot(xg, sc_ref[...]).astype(BF16)

    y = _dot(cs_ref[...], xc_ref[...]) - _dot(ss_ref[...], xs_ref[...])
    o_ref[...] = (y * (1.0 / math.sqrt(DEC_SEQ * FN_CH))).astype(BF16)


def _fourier(x, dft, nb):
    nq = DEC_SEQ // TQ
    return pl.pallas_call(
        _fourier_kernel,
        grid=(nb, nq),
        in_specs=[
            pl.BlockSpec((DEC_SEQ, 512), lambda b, t: (b, 0)),
            _const_spec((FN_CH, FN_CH)), _const_spec((FN_CH, FN_CH)),
            pl.BlockSpec((TQ, DEC_SEQ), lambda b, t: (t, 0)),
            pl.BlockSpec((TQ, DEC_SEQ), lambda b, t: (t, 0)),
        ],
        out_specs=pl.BlockSpec((TQ, 512), lambda b, t: (b * nq + t, 0)),
        out_shape=jax.ShapeDtypeStruct((nb * DEC_SEQ, 512), BF16),
        scratch_shapes=[pltpu.VMEM((DEC_SEQ, 512), BF16), pltpu.VMEM((DEC_SEQ, 512), BF16)],
        compiler_params=_cp(("arbitrary", "arbitrary")),
        name="fourier_mix",
    )(x, dft['cc'], dft['sc'], dft['cs_lat'], dft['ss_lat'])


def _out_kernel(oa, ob, oc, od, x_ref, mod_ref, gffn_ref, w_ref, x1_o, h2_o):
    o = jnp.concatenate([oa[...], ob[...], oc[...], od[...]], axis=-1)
    acc = _dot(o, w_ref[...])
    x1 = x_ref[...] + mod_ref[0, 2:3, :] * acc
    x1_o[...] = x1
    rinv = lax.rsqrt(jnp.mean(x1 * x1, axis=-1, keepdims=True) + EPS)
    h2 = (x1 * rinv) * (gffn_ref[...] * (1.0 + mod_ref[0, 4:5, :])) + mod_ref[0, 3:4, :]
    h2_o[...] = h2.astype(BF16)


def _mod_map(layer, tm, latent):
    if latent:
        return lambda i, *_: (layer * MOD_ROWS + 1 + (i * tm) // DEC_SEQ, 0, 0)
    return lambda i, *_: (layer * MOD_ROWS, 0, 0)


def _out_project(o4, x, mod, layer, p, latent):
    n = x.shape[0]
    tm = TM_OUT
    row = lambda i: (i, 0)
    return pl.pallas_call(
        _out_kernel,
        grid=(n // tm,),
        in_specs=[pl.BlockSpec((tm, 512), row)] * 4 + [
            pl.BlockSpec((tm, D_MODEL), row),
            pl.BlockSpec((1, MOD_CHUNKS, D_MODEL), _mod_map(layer, tm, latent)),
            _layer_spec((1, D_MODEL), layer),
            _layer_spec((D_MODEL, D_MODEL), layer, resident=True),
        ],
        out_specs=[pl.BlockSpec((tm, D_MODEL), row), pl.BlockSpec((tm, D_MODEL), row)],
        out_shape=[jax.ShapeDtypeStruct((n, D_MODEL), F32), jax.ShapeDtypeStruct((n, D_MODEL), BF16)],
        compiler_params=_cp(("arbitrary",)),
        name="out_proj_latent" if latent else "out_proj_context",
    )(*o4, x, mod, p['g_ffn'], p['w_out'])


def _ffn_kernel(h_ref, hp_ref, hn_ref, x1_ref, mod_ref, wg_ref, wu_ref, wc_ref, wd_ref, o_ref, hext_ref, acc_ref,
                *, seq_len):
    i = pl.program_id(0)
    j = pl.program_id(1)
    tm = h_ref.shape[0]

    @pl.when(j == 0)
    def _():
        hext_ref[0:HALO, :] = hp_ref[...]
        hext_ref[HALO:HALO + tm, :] = h_ref[...]
        hext_ref[HALO + tm:, :] = hn_ref[...]
        acc_ref[...] = jnp.zeros_like(acc_ref)

    g_ext = _dot(hext_ref[...], wg_ref[...])
    rows_ext = g_ext.shape[0]
    pos = (i * tm + lax.broadcasted_iota(jnp.int32, (tm, 1), 0)) % seq_len
    g_mid = g_ext[HALO:HALO + tm]
    g_prev = pltpu.roll(g_ext, 1, 0)[HALO:HALO + tm]
    g_next = pltpu.roll(g_ext, rows_ext - 1, 0)[HALO:HALO + tm]
    g_prev = jnp.where(pos == 0, 0.0, g_prev)
    g_next = jnp.where(pos == seq_len - 1, 0.0, g_next)
    g = g_prev * wc_ref[0:1, :] + g_mid * wc_ref[1:2, :] + g_next * wc_ref[2:3, :]
    u = _dot(h_ref[...], wu_ref[...])
    act = (g / (1.0 + jnp.exp(-g))) * u
    acc_ref[...] += _dot(act.astype(BF16), wd_ref[...])

    @pl.when(j == pl.num_programs(1) - 1)
    def _():
        o_ref[...] = x1_ref[...] + mod_ref[0, 5:6, :] * acc_ref[...]


def _ffn(h2, x1, mod, layer, p, latent):
    n = x1.shape[0]
    tm, tf = TM_FFN, TF_FFN
    seq_len = DEC_SEQ if latent else SEQ
    assert n % tm == 0 and tm % HALO == 0 and D_FF % tf == 0
    hb = tm // HALO
    nhalo = n // HALO
    return pl.pallas_call(
        functools.partial(_ffn_kernel, seq_len=seq_len),
        grid=(n // tm, D_FF // tf),
        in_specs=[
            pl.BlockSpec((tm, D_MODEL), lambda i, j: (i, 0)),
            pl.BlockSpec((HALO, D_MODEL), lambda i, j: (jnp.maximum(i * hb - 1, 0), 0)),
            pl.BlockSpec((HALO, D_MODEL), lambda i, j: (jnp.minimum((i + 1) * hb, nhalo - 1), 0)),
            pl.BlockSpec((tm, D_MODEL), lambda i, j: (i, 0)),
            pl.BlockSpec((1, MOD_CHUNKS, D_MODEL), _mod_map(layer, tm, latent)),
            pl.BlockSpec((None, D_MODEL, tf), lambda i, j: (layer, 0, j)),
            pl.BlockSpec((None, D_MODEL, tf), lambda i, j: (layer, 0, j)),
            pl.BlockSpec((None, 8, tf), lambda i, j: (layer, 0, j)),
            pl.BlockSpec((None, tf, D_MODEL), lambda i, j: (layer, j, 0)),
        ],
        out_specs=pl.BlockSpec((tm, D_MODEL), lambda i, j: (i, 0)),
        out_shape=jax.ShapeDtypeStruct((n, D_MODEL), F32),
        scratch_shapes=[pltpu.VMEM((tm + 2 * HALO, D_MODEL), BF16), pltpu.VMEM((tm, D_MODEL), F32)],
        compiler_params=_cp(("arbitrary", "arbitrary")),
        name="ffn_latent" if latent else "ffn_context",
    )(h2, h2, h2, x1, mod, p['w_gate'], p['w_up'], p['w_conv'], p['w_down'])


def _tile_row(g, reps, scale=1.0):
    return (jnp.tile(g, (1, reps)) * scale)[:, None, :]


def _prep(w):
    win_r = jnp.pad(w['w_in'].astype(BF16), ((0, 0), (0, 0), (0, IN_COLS_P - IN_COLS)))

    perm = np.array([j + 16 if j % 32 < 16 else j - 16 for j in range(MLA_ROPE)])
    half_pad = lambda t: jnp.concatenate([t, jnp.zeros(t.shape[:-1] + (LANES - MLA_ROPE,), F32)], axis=-1)
    wq = w['w_q_up'].reshape(DEPTH, MLA_Q_LORA, MLA_HEADS, MLA_QK)
    wq_r = jnp.concatenate([wq, jnp.zeros((DEPTH, MLA_Q_LORA, MLA_HEADS, 256 - MLA_QK), F32)], axis=-1)
    wq_p = half_pad(wq[..., MLA_NOPE:][..., perm])
    wq_r = jnp.concatenate([wq_r.reshape(DEPTH, MLA_Q_LORA, -1), wq_p.reshape(DEPTH, MLA_Q_LORA, -1)],
                           axis=-1).astype(BF16)
    wkv = w['w_kv_up'].reshape(DEPTH, MLA_KV_LORA, MLA_HEADS, 2 * MLA_NOPE)
    wkv_r = jnp.concatenate([wkv[..., :MLA_NOPE].reshape(DEPTH, MLA_KV_LORA, -1),
                             wkv[..., MLA_NOPE:].reshape(DEPTH, MLA_KV_LORA, -1)], axis=-1).astype(BF16)

    def pad_head(g, s):
        return _tile_row(jnp.concatenate([g, jnp.zeros((DEPTH, 256 - MLA_QK), F32)], axis=-1), MLA_HEADS, s)

    def partner_gain(g, s):
        return _tile_row(half_pad(g[:, MLA_NOPE:][:, perm]), MLA_HEADS, s)

    seg = np.kron(np.eye(LANES // HEAD_DIM), np.full((HEAD_DIM, HEAD_DIM), 1.0 / HEAD_DIM)).astype(np.float32)
    wc = jnp.concatenate([w['w_conv'], jnp.zeros((DEPTH, 5, D_FF), F32)], axis=1)
    return dict(
        g_mix=w['g_mix'][:, None, :], g_ffn=w['g_ffn'][:, None, :],
        w_in=win_r, w_q_up=wq_r, w_kv_up=wkv_r,
        gq_na=_tile_row(w['g_qn_na'], 8, HEAD_DIM ** -0.5 * LOG2E), gk_na=_tile_row(w['g_kn_na'], 8),
        g_q_lora=w['g_q_lora'][:, None, :], g_kv_lora=w['g_kv_lora'][:, None, :],
        gq_mla=pad_head(w['g_qn_mla'], MLA_QK ** -0.5 * LOG2E), gk_mla=pad_head(w['g_kn_mla'], 1.0),
        gqp_mla=partner_gain(w['g_qn_mla'], MLA_QK ** -0.5 * LOG2E), gkp_mla=partner_gain(w['g_kn_mla'], 1.0),
        seg=jnp.asarray(seg).astype(BF16),
        gq_win=_tile_row(w['g_qn_win'], 8, HEAD_DIM ** -0.5 * LOG2E), gk_win=_tile_row(w['g_kn_win'], 2),
        sink=w['sink_win'] * LOG2E,
        w_out=w['w_out'].astype(BF16), w_gate=w['w_gate'].astype(BF16), w_up=w['w_up'].astype(BF16),
        w_conv=wc, w_down=w['w_down'].astype(BF16),
    )


def _na_bias_rows(rpb):
    lead = rpb.shape[:-2]
    ext = jnp.concatenate([jnp.repeat(rpb[..., :1], 48, axis=-1), rpb, jnp.repeat(rpb[..., -1:], 49, axis=-1)], axis=-1)
    return jnp.concatenate([ext, jnp.full(lead + (1, LANES), NEG, F32)], axis=-2)


@functools.lru_cache(maxsize=None)
def _dft_tables():
    def cs(n):
        k = (np.arange(n)[:, None] * np.arange(n)[None, :]) % n
        ang = 2.0 * np.pi * k.astype(np.float64) / n
        return np.cos(ang), np.sin(ang)

    out = {}
    for name, n in (('ctx', SEQ), ('lat', DEC_SEQ)):
        c, s = cs(n)
        out['cs_' + name], out['ss_' + name] = c, s
    out['cc'], out['sc'] = cs(FN_CH)
    return out


@functools.lru_cache(maxsize=None)
def _rope_tables():
    t = np.arange(DEC_SEQ)
    quarter = MLA_ROPE // 4
    inv = ROPE_BASE ** (-np.arange(quarter, dtype=np.float64) / quarter)
    j = np.arange(MLA_ROPE)
    pos = np.where(j[None, :] < MLA_ROPE // 2, (t // GRID_W)[:, None], (t % GRID_W)[:, None]).astype(np.float64)
    ang = pos * inv[j % quarter][None, :]
    sign = np.where((j % 32) < 16, -1.0, 1.0)
    cos = np.cos(ang)
    sin = np.sin(ang) * sign[None, :]
    return np.tile(cos, (1, 2)).astype(np.float32), np.tile(sin, (1, 2)).astype(np.float32)


def kernel(x_prompt, x_sample, cache_na_k, cache_na_v, cache_mla_ckv, cache_mla_krope, cache_win_k, cache_win_v,
           c, c_ctx, w_mod, b_mod, g_mix, g_ffn, w_in, g_qn_na, g_kn_na, rpb_na, g_q_lora, w_q_up, g_kv_lora,
           w_kv_up, g_qn_mla, g_kn_mla, g_qn_win, g_kn_win, sink_win, w_out, w_gate, w_up, w_conv, w_down):
    w = dict(g_mix=g_mix, g_ffn=g_ffn, w_in=w_in, g_qn_na=g_qn_na, g_kn_na=g_kn_na, g_q_lora=g_q_lora,
             w_q_up=w_q_up, g_kv_lora=g_kv_lora, w_kv_up=w_kv_up, g_qn_mla=g_qn_mla, g_kn_mla=g_kn_mla,
             g_qn_win=g_qn_win, g_kn_win=g_kn_win, sink_win=sink_win, w_out=w_out, w_gate=w_gate, w_up=w_up,
             w_conv=w_conv, w_down=w_down)
    nb_ctx, nb_lat = x_prompt.shape[0], x_sample.shape[0]
    dft = {k: jnp.asarray(v, F32).astype(BF16) for k, v in _dft_tables().items()}
    rope_tabs = tuple(jnp.asarray(t) for t in _rope_tables())

    mod = _modulation(jnp.concatenate([c_ctx[None, :], c], axis=0), w_mod, b_mod)

    xp = x_prompt.reshape(nb_ctx * SEQ, D_MODEL)
    xs = x_sample.reshape(nb_lat * DEC_SEQ, D_MODEL)
    caches = []
    p = _prep(w)
    nctx = nb_lat * DEPTH * PAST_LEN
    dup = lambda t: jnp.concatenate([t[..., :64], t[..., :64], t[..., 64:], t[..., 64:]], axis=-1).astype(BF16)
    kna_c = cache_na_k.reshape(nb_lat, DEPTH, PAST_LEN, 512).astype(BF16)
    vna_c = cache_na_v.reshape(nb_lat, DEPTH, PAST_LEN, 512).astype(BF16)
    kw_c = dup(cache_win_k.reshape(nb_lat, DEPTH, PAST_LEN, 128))
    vw_c = dup(cache_win_v.reshape(nb_lat, DEPTH, PAST_LEN, 128))
    ckv_c = cache_mla_ckv.reshape(nctx, MLA_KV_LORA)
    kr_c = jnp.pad(cache_mla_krope.reshape(nctx, MLA_ROPE), ((0, 0), (0, LANES - MLA_ROPE)))
    bias_ext = _na_bias_rows(rpb_na * LOG2E)
    for l in range(DEPTH):

        pr = _project(xp, mod, l, p, False, None)
        caches.append(pr[10:])
        o4 = _ctx_attention(pr, p['sink'], dft, l)
        x1, h2 = _out_project(o4, xp, mod, l, p, False)
        xp = _ffn(h2, x1, mod, l, p, False)

        qna, kna, vna, qm, km, vm, qw, kw, vw, fv = _project(xs, mod, l, p, True, rope_tabs)
        km_c, vm_c = _mla_ctx(ckv_c, kr_c, p, l, nb_lat)
        o_a = _na_attention(qna, kna, vna, kna_c, vna_c, bias_ext, nb_lat, l)
        o_b = _mla_attention(qm, km, vm, km_c, vm_c, nb_lat)
        o_c = _win_attention(p['sink'], qw, kw, vw, kw_c, vw_c, nb_lat, l)
        o_d = _fourier(fv, dft, nb_lat)
        x1, h2 = _out_project((o_a, o_b, o_c, o_d), xs, mod, l, p, True)
        xs = _ffn(h2, x1, mod, l, p, True)

    def stack(idx, shape):
        return jnp.stack([caches[l][idx].reshape((nb_ctx, SEQ) + shape) for l in range(DEPTH)], axis=1)

    return (xp.reshape(nb_ctx, SEQ, D_MODEL), xs.reshape(nb_lat, DEC_SEQ, D_MODEL),
            stack(0, (8, HEAD_DIM)), stack(1, (8, HEAD_DIM)), stack(2, (MLA_KV_LORA,)), stack(3, (MLA_ROPE,)),
            stack(4, (2, HEAD_DIM)), stack(5, (2, HEAD_DIM)))
```

```python
import functools
import math

import numpy as np
import jax
import jax.numpy as jnp
from jax import lax
from jax.experimental import pallas as pl
from jax.experimental.pallas import tpu as pltpu

F32 = jnp.float32
BF16 = jnp.bfloat16

D_MODEL = 2048
DEPTH = 2
SEQ = 256
DEC_SEQ = 2048
PAST_LEN = 512
GRID_W = 64
HEAD_DIM = 64
GROUP_W = 512
MLA_HEADS = 4
MLA_NOPE = 128
MLA_ROPE = 64
MLA_QK = MLA_NOPE + MLA_ROPE
MLA_Q_LORA = 384
MLA_KV_LORA = 128
WINDOW = 128
FN_GROUPS = 4
FN_CH = 128
D_FF = 5632
MOD_CHUNKS = 6
ROPE_BASE = 10000.0
EPS = 1e-6
NEG = -1e30
LOG2E = math.log2(math.e)

LANES = 128
MOD_ROWS = 8
VMEM_LIMIT = 56 * 1024 * 1024

C_AQ, C_AK, C_AV = 0, 512, 1024
C_MQ = 1536
C_CKV = 1920
C_KR = 2048
IN_COLS = 3392
IN_COLS_P = 3456

TM_PROJ = 512
TM_OUT = 512
TM_FFN = 512
TF_FFN = 512
HALO = 16
TQ = 256
TQ_MLA = 512
NA_KH = 8
NA_KW = 16
NA_KROWS = 12
NA_MASKED = 15


def _cp(sem):
    return pltpu.CompilerParams(dimension_semantics=sem, vmem_limit_bytes=VMEM_LIMIT)


def _const_spec(shape, resident=False):
    n = len(shape)
    mode = dict(pipeline_mode=pl.Buffered(1)) if resident else {}
    return pl.BlockSpec(shape, lambda *_: (0,) * n, **mode)


def _layer_spec(shape, layer, resident=False):
    n = len(shape)
    mode = dict(pipeline_mode=pl.Buffered(1)) if resident else {}
    return pl.BlockSpec((None,) + tuple(shape), lambda *_: (layer,) + (0,) * n, **mode)


def _dot(a, b):
    return jnp.dot(a, b, preferred_element_type=F32)


def _dot_nt(a, b):
    return lax.dot_general(a, b, (((1,), (1,)), ((), ())), preferred_element_type=F32)


def _rms(t, n):
    return t * lax.rsqrt(jnp.sum(t * t, axis=-1, keepdims=True) * (1.0 / n) + EPS)


def _seg_rms64(t, seg_ref):
    outs = []
    for c in range(t.shape[1] // LANES):
        tc = t[:, c * LANES:(c + 1) * LANES]
        ms = _dot((tc * tc).astype(BF16), seg_ref[...])
        outs.append(tc * lax.rsqrt(ms + EPS))
    return jnp.concatenate(outs, axis=-1)


def _partner128(t):
    first = (lax.broadcasted_iota(jnp.int32, t.shape, 1) % 32) < 16
    return jnp.where(first, pltpu.roll(t, LANES - 16, 1), pltpu.roll(t, 16, 1))


def _rope128(t, cos, sin_signed):
    return t * cos + _partner128(t) * sin_signed


def _rope_wide(t, cos, sin_signed):
    return jnp.concatenate(
        [_rope128(t[:, c * LANES:(c + 1) * LANES], cos, sin_signed) for c in range(t.shape[1] // LANES)], axis=-1)


MOD_TK = 128
MOD_LHS_ROWS = 16


def _split_bf16(t):
    hi = t.astype(BF16)
    return hi, (t - hi.astype(F32)).astype(BF16)


def _mod_kernel(c_ref, w_ref, b_ref, o_ref):
    @pl.when(pl.program_id(1) == 0)
    def _():
        o_ref[...] = jnp.broadcast_to(b_ref[...], o_ref.shape)

    c = c_ref[...]
    x_hi, x_lo = _split_bf16(c / (1.0 + jnp.exp(-c)))
    w_hi, w_lo = _split_bf16(w_ref[...])
    acc = _dot(x_hi, w_hi) + _dot(x_lo, w_hi) + _dot(x_hi, w_lo)
    o_ref[...] += acc[:MOD_ROWS]


def _modulation(cvecs, w_mod, b_mod):
    nvec = cvecs.shape[0]
    c = jnp.zeros((MOD_LHS_ROWS, D_MODEL), F32).at[:nvec].set(cvecs)
    ncol = w_mod.shape[2]
    out = pl.pallas_call(
        _mod_kernel,
        grid=(DEPTH, D_MODEL // MOD_TK),
        in_specs=[
            pl.BlockSpec((MOD_LHS_ROWS, MOD_TK), lambda l, k: (0, k)),
            pl.BlockSpec((None, MOD_TK, ncol), lambda l, k: (l, k, 0)),
            pl.BlockSpec((None, 1, ncol), lambda l, k: (l, 0, 0)),
        ],
        out_specs=pl.BlockSpec((None, MOD_ROWS, ncol), lambda l, k: (l, 0, 0)),
        out_shape=jax.ShapeDtypeStruct((DEPTH, MOD_ROWS, ncol), F32),
        compiler_params=_cp(("arbitrary", "arbitrary")),
        name="modulation",
    )(c, w_mod, b_mod.reshape(DEPTH, 1, ncol))
    return out.reshape(DEPTH * MOD_ROWS, MOD_CHUNKS, D_MODEL)


def _mla_kv_heads(ckv_n, kr, wkv_ref, gk_ref, gkp_ref, rope):
    kv = _dot(ckv_n.astype(BF16), wkv_ref[...])
    ss_r = jnp.sum(kr * kr, axis=-1, keepdims=True)
    pk = _partner128(kr) if rope is not None else None
    ks = []
    for h in range(MLA_HEADS):
        kn = kv[:, h * MLA_NOPE:(h + 1) * MLA_NOPE]
        ms = (jnp.sum(kn * kn, axis=-1, keepdims=True) + ss_r) * (1.0 / MLA_QK)
        r = lax.rsqrt(ms + EPS)
        tail = kr * r * gk_ref[:, h * 256 + 128:(h + 1) * 256]
        if rope is not None:
            tail = tail * rope[0] + (pk * r * gkp_ref[:, h * LANES:(h + 1) * LANES]) * rope[1]
        ks.append(kn * r * gk_ref[:, h * 256:h * 256 + 128])
        ks.append(tail)
    return jnp.concatenate(ks, axis=-1), kv[:, MLA_HEADS * MLA_NOPE:]


def _proj_kernel(*refs, latent):
    (x_ref, mod_ref, gmix_ref, win_ref, wq_ref, wkv_ref, gqna_ref, gkna_ref, gql_ref, gkvl_ref,
     gqm_ref, gkm_ref, gqw_ref, gkw_ref, gqp_ref, gkp_ref, seg_ref) = refs[:17]
    pos = 17
    rope = None
    if latent:
        rope = (refs[17][...], refs[18][...])
        pos = 19
    (qna_o, kna_o, vna_o, qm_o, km_o, vm_o, qw_o, kw_o, vw_o, fv_o) = refs[pos:pos + 10]
    cache_o = refs[pos + 10:]

    x = x_ref[...]
    rinv = lax.rsqrt(jnp.mean(x * x, axis=-1, keepdims=True) + EPS)
    h = (x * rinv) * (gmix_ref[...] * (1.0 + mod_ref[0, 1:2, :])) + mod_ref[0, 0:1, :]
    hb = h.astype(BF16)

    projected = _dot(hb, win_ref[...])

    def proj(c0, c1):
        return projected[:, c0:c1]

    qna_o[...] = (_seg_rms64(proj(C_AQ, C_AK), seg_ref) * gqna_ref[...]).astype(BF16)
    k_na = _seg_rms64(proj(C_AK, C_AV), seg_ref) * gkna_ref[...]
    kna_o[...] = k_na.astype(BF16)
    v_na = proj(C_AV, C_MQ)
    vna_o[...] = v_na.astype(BF16)

    cq = (_rms(proj(C_MQ, C_CKV), MLA_Q_LORA) * gql_ref[...]).astype(BF16)
    qu = _dot(cq, wq_ref[...] if latent else wq_ref[:, :MLA_HEADS * 256])
    qs = []
    for hd in range(MLA_HEADS):
        blk = qu[:, hd * 256:(hd + 1) * 256]
        rinv = lax.rsqrt(jnp.sum(blk * blk, axis=-1, keepdims=True) * (1.0 / MLA_QK) + EPS)
        blk = blk * rinv * gqm_ref[:, hd * 256:(hd + 1) * 256]
        if latent:
            ps = slice(MLA_HEADS * 256 + hd * LANES, MLA_HEADS * 256 + (hd + 1) * LANES)
            partner = qu[:, ps] * rinv * gqp_ref[:, hd * LANES:(hd + 1) * LANES]
            blk = jnp.concatenate([blk[:, :LANES], blk[:, LANES:] * rope[0] + partner * rope[1]], axis=-1)
        qs.append(blk)
    qm_o[...] = jnp.concatenate(qs, axis=-1).astype(BF16)
    ckv_n = _rms(proj(C_CKV, C_KR), MLA_KV_LORA) * gkvl_ref[...]

    tail = proj(C_KR, IN_COLS_P)
    lo = lax.broadcasted_iota(jnp.int32, (x.shape[0], LANES), 1) < HEAD_DIM
    nslab = (IN_COLS_P - C_KR) // LANES
    slabs = [tail[:, c * LANES:(c + 1) * LANES] for c in range(nslab)]
    swapped = [pltpu.roll(t, HEAD_DIM, 1) for t in slabs]
    al = [jnp.where(lo, swapped[c], swapped[c + 1]) for c in range(nslab - 1)]
    kr = jnp.where(lo, slabs[0], 0.0)
    k_m, v_m = _mla_kv_heads(ckv_n, kr, wkv_ref, gkm_ref, gkp_ref, rope)
    km_o[...] = k_m.astype(BF16)
    vm_o[...] = v_m.astype(BF16)

    def dup(t):
        sw = pltpu.roll(t, HEAD_DIM, 1)
        return jnp.concatenate([jnp.where(lo, t, sw), jnp.where(lo, sw, t)], axis=-1)

    q_w = _seg_rms64(jnp.concatenate(al[0:4], axis=-1), seg_ref) * gqw_ref[...]
    k_w = _seg_rms64(al[4], seg_ref) * gkw_ref[...]
    v_w = al[5]
    if latent:
        q_w = _rope_wide(q_w, *rope)
        k_w = _rope128(k_w, *rope)
    qw_o[...] = q_w.astype(BF16)
    kw_o[...] = dup(k_w).astype(BF16)
    vw_o[...] = dup(v_w).astype(BF16)

    fv_o[...] = jnp.concatenate(al[6:10], axis=-1).astype(BF16)

    if not latent:
        kna32_o, vna32_o, ckv32_o, kr32_o, kw32_o, vw32_o = cache_o
        kna32_o[...] = k_na
        vna32_o[...] = v_na
        ckv32_o[...] = ckv_n
        kr32_o[...] = kr[:, :MLA_ROPE]
        kw32_o[...] = k_w
        vw32_o[...] = v_w


def _project(x, mod, layer, p, latent, rope_tabs):
    n = x.shape[0]
    tm = TM_PROJ
    row = lambda i: (i, 0)
    if latent:
        mod_map = lambda i: (layer * MOD_ROWS + 1 + (i * tm) // DEC_SEQ, 0, 0)
    else:
        mod_map = lambda i: (layer * MOD_ROWS, 0, 0)
    in_specs = [
        pl.BlockSpec((tm, D_MODEL), row),
        pl.BlockSpec((1, MOD_CHUNKS, D_MODEL), mod_map),
        _layer_spec((1, D_MODEL), layer),
        _layer_spec((D_MODEL, IN_COLS_P), layer, resident=True),
        _layer_spec((MLA_Q_LORA, MLA_HEADS * (256 + LANES)), layer, resident=True),
        _layer_spec((MLA_KV_LORA, 1024), layer, resident=True),
    ] + [_layer_spec((1, wd), layer) for wd in (512, 512, MLA_Q_LORA, MLA_KV_LORA, 1024, 1024, 512, 128, 512, 512)]
    in_specs.append(_const_spec((LANES, LANES)))
    args = [x, mod, p['g_mix'], p['w_in'], p['w_q_up'], p['w_kv_up'], p['gq_na'], p['gk_na'], p['g_q_lora'],
            p['g_kv_lora'], p['gq_mla'], p['gk_mla'], p['gq_win'], p['gk_win'], p['gqp_mla'], p['gkp_mla'],
            p['seg']]
    if latent:
        nt = DEC_SEQ // tm
        in_specs += [pl.BlockSpec((tm, LANES), lambda i: (i % nt, 0))] * 2
        args += list(rope_tabs)
    widths = [512, 512, 512, 1024, 1024, 512, 512, 256, 256, 512]
    out_specs = [pl.BlockSpec((tm, w), row) for w in widths]
    out_shape = [jax.ShapeDtypeStruct((n, w), BF16) for w in widths]
    if not latent:
        cache_w = [512, 512, MLA_KV_LORA, MLA_ROPE, 128, 128]
        out_specs += [pl.BlockSpec((tm, w), row) for w in cache_w]
        out_shape += [jax.ShapeDtypeStruct((n, w), F32) for w in cache_w]
    return pl.pallas_call(
        functools.partial(_proj_kernel, latent=latent),
        grid=(n // tm,),
        in_specs=in_specs,
        out_specs=out_specs,
        out_shape=out_shape,
        compiler_params=_cp(("arbitrary",)),
        name="proj_latent" if latent else "proj_context",
    )(*args)


def _mla_ctx_kernel(ckv_ref, kr_ref, wkv_ref, gk_ref, k_o, v_o):
    k_m, v_m = _mla_kv_heads(ckv_ref[...], kr_ref[...], wkv_ref, gk_ref, None, None)
    k_o[...] = k_m.astype(BF16)
    v_o[...] = v_m.astype(BF16)


def _mla_ctx(ckv, kr128, p, layer, nb):
    n = nb * PAST_LEN
    src = lambda b: (b * DEPTH + layer, 0)
    row = lambda b: (b, 0)
    return pl.pallas_call(
        _mla_ctx_kernel,
        grid=(nb,),
        in_specs=[pl.BlockSpec((PAST_LEN, MLA_KV_LORA), src), pl.BlockSpec((PAST_LEN, LANES), src),
                  _layer_spec((MLA_KV_LORA, 1024), layer), _layer_spec((1, 1024), layer)],
        out_specs=[pl.BlockSpec((PAST_LEN, 1024), row), pl.BlockSpec((PAST_LEN, 512), row)],
        out_shape=[jax.ShapeDtypeStruct((n, 1024), BF16), jax.ShapeDtypeStruct((n, 512), BF16)],
        compiler_params=_cp(("arbitrary",)),
        name="mla_cached_kv",
    )(ckv, kr128, p['w_kv_up'], p['gk_mla'])


def _softmax_pv(scores, values, sink=None):
    m = functools.reduce(jnp.maximum, [jnp.max(s, axis=-1, keepdims=True) for s in scores])
    if sink is not None:
        m = jnp.maximum(m, sink)
    ps = [jnp.exp2(s - m) for s in scores]
    l = functools.reduce(lambda a, b: a + b, [jnp.sum(p, axis=-1, keepdims=True) for p in ps])
    if sink is not None:
        l = l + jnp.exp2(sink - m)
    o = functools.reduce(lambda a, b: a + b, [_dot(p.astype(BF16), v) for p, v in zip(ps, values)])
    return o / l


def _half_masks():
    lane = lax.broadcasted_iota(jnp.int32, (1, LANES), 1)
    lo = lane < HEAD_DIM
    return jnp.where(lo, 1.0, 0.0).astype(BF16), jnp.where(lo, 0.0, 1.0).astype(BF16)


def _pair_attention(q2, keys, values, extra=None, sinks=None):
    rows = q2.shape[0]
    lo = lax.broadcasted_iota(jnp.int32, (rows, LANES), 1) < HEAD_DIM
    outs = []
    for e, msk in enumerate(_half_masks()):
        qm = q2 * msk
        scores = []
        for idx, k2 in enumerate(keys):
            s = _dot_nt(qm, k2)
            if extra is not None:
                s = extra(e, idx, s)
            scores.append(s)
        sink = None
        if sinks is not None:
            sink = jnp.full((rows, 1), sinks[e], F32)
        outs.append(_softmax_pv(scores, values, sink))
    return jnp.where(lo, outs[0], outs[1])


def _dft_real(x, cc_ref, sc_ref, cs, ss, scale):
    outs = []
    for g in range(FN_GROUPS):
        xg = x[:, g * FN_CH:(g + 1) * FN_CH]
        xc = _dot(xg, cc_ref[...]).astype(BF16)
        xs = _dot(xg, sc_ref[...]).astype(BF16)
        outs.append(_dot(cs, xc) - _dot(ss, xs))
    return jnp.concatenate(outs, axis=-1) * scale


def _ctx_attn_kernel(sink_ref, qna, kna, vna, qm, km, vm, qw, kw, vw, fv, cs_ref, ss_ref, cc_ref, sc_ref,
                     oa, ob, oc, od, *, layer):
    for hp in range(4):
        cs = slice(hp * LANES, (hp + 1) * LANES)
        oa[:, cs] = _pair_attention(qna[:, cs], [kna[:, cs]], [vna[:, cs]]).astype(BF16)
    for hd in range(MLA_HEADS):
        s = _dot_nt(qm[:, hd * 256:(hd + 1) * 256], km[:, hd * 256:(hd + 1) * 256])
        ob[:, hd * 128:(hd + 1) * 128] = _softmax_pv([s], [vm[:, hd * 128:(hd + 1) * 128]]).astype(BF16)
    for hp in range(4):
        cs = slice(hp * LANES, (hp + 1) * LANES)
        kcs = slice((hp // 2) * LANES, (hp // 2 + 1) * LANES)
        oc[:, cs] = _pair_attention(qw[:, cs], [kw[:, kcs]], [vw[:, kcs]],
                                    sinks=(sink_ref[layer, 2 * hp], sink_ref[layer, 2 * hp + 1])).astype(BF16)
    od[...] = _dft_real(fv[...], cc_ref, sc_ref, cs_ref[...], ss_ref[...],
                        1.0 / math.sqrt(SEQ * FN_CH)).astype(BF16)


def _ctx_attention(pr, sink, dft, layer):
    qna, kna, vna, qm, km, vm, qw, kw, vw, fv = pr[:10]
    n = qna.shape[0]
    row = lambda b: (b, 0)
    widths = [512, 512, 512, 1024, 1024, 512, 512, 256, 256, 512]
    in_specs = [pl.BlockSpec(memory_space=pltpu.SMEM)]
    in_specs += [pl.BlockSpec((SEQ, w), row) for w in widths]
    in_specs += [_const_spec((SEQ, SEQ)), _const_spec((SEQ, SEQ)), _const_spec((FN_CH, FN_CH)),
                 _const_spec((FN_CH, FN_CH))]
    return pl.pallas_call(
        functools.partial(_ctx_attn_kernel, layer=layer),
        grid=(n // SEQ,),
        in_specs=in_specs,
        out_specs=[pl.BlockSpec((SEQ, 512), row)] * 4,
        out_shape=[jax.ShapeDtypeStruct((n, 512), BF16)] * 4,
        compiler_params=_cp(("arbitrary",)),
        name="context_attention",
    )(sink, qna, kna, vna, qm, km, vm, qw, kw, vw, fv, dft['cs_ctx'], dft['ss_ctx'], dft['cc'], dft['sc'])


def _na_kernel(q_ref, k_ref, v_ref, kc_ref, vc_ref, ext_ref, o_ref, tl_ref, tr_ref):
    qt = pl.program_id(1)

    @pl.when((pl.program_id(0) == 0) & (qt == 0))
    def _():
        col = lax.broadcasted_iota(jnp.int32, (GRID_W, LANES), 0)
        lane = lax.broadcasted_iota(jnp.int32, (GRID_W, LANES), 1)
        kcol = lane % GRID_W
        first = jnp.clip(col - NA_KW // 2, 0, GRID_W - NA_KW)
        in_win = (kcol >= first) & (kcol < first + NA_KW)
        left = lane < GRID_W
        for h in range(tl_ref.shape[0]):
            for d in range(tl_ref.shape[1]):
                row = jnp.broadcast_to(ext_ref[h, d:d + 1, :], (GRID_W, LANES))
                blk_l = pltpu.roll(row, GRID_W + 1, 1, stride=1, stride_axis=0)
                blk_r = pltpu.roll(row, 1, 1, stride=1, stride_axis=0)
                tl_ref[h, d] = jnp.where(left, jnp.where(in_win, blk_l, NEG), 0.0)
                tr_ref[h, d] = jnp.where(left, 0.0, jnp.where(in_win, blk_r, NEG))

    rows = DEC_SEQ // GRID_W
    qrows = TQ // GRID_W
    kstart = jnp.clip(qt * qrows - NA_KH // 2, 0, rows - NA_KROWS)
    ks = pl.multiple_of(kstart * GRID_W, GRID_W)
    nk = NA_KROWS * GRID_W

    blk_idx = []
    for i in range(qrows):
        r = qt * qrows + i
        rs = jnp.clip(r - NA_KH // 2, 0, rows - NA_KH)
        blk_idx.append([jnp.where((kstart + j >= rs) & (kstart + j < rs + NA_KH), kstart + j - r + NA_KH - 1,
                                  NA_MASKED) for j in range(NA_KROWS)])

    def bias(h):
        return jnp.concatenate([
            jnp.concatenate([tl_ref[h, blk_idx[i][2 * jp]] + tr_ref[h, blk_idx[i][2 * jp + 1]]
                             for jp in range(NA_KROWS // 2)], axis=-1)
            for i in range(qrows)], axis=0)

    for hp in range(4):
        cs = slice(hp * LANES, (hp + 1) * LANES)

        def extra(e, idx, s, hp=hp):
            return s + bias(2 * hp + e) if idx == 0 else s

        o = _pair_attention(q_ref[:, cs], [k_ref[pl.ds(ks, nk), cs], kc_ref[:, cs]],
                            [v_ref[pl.ds(ks, nk), cs], vc_ref[:, cs]], extra=extra)
        o_ref[:, cs] = o.astype(BF16)


def _lat_specs(width_q, width_k, width_v, ctx_map, tq=TQ):
    nq = DEC_SEQ // tq
    ctx_nd = len(ctx_map(0, 0))
    lead = (None,) * (ctx_nd - 2)
    return [
        pl.BlockSpec((tq, width_q), lambda b, t: (b * nq + t, 0)),
        pl.BlockSpec((DEC_SEQ, width_k), lambda b, t: (b, 0)),
        pl.BlockSpec((DEC_SEQ, width_v), lambda b, t: (b, 0)),
        pl.BlockSpec(lead + (PAST_LEN, width_k), ctx_map),
        pl.BlockSpec(lead + (PAST_LEN, width_v), ctx_map),
    ]


def _lat_call(kernel, name, args, in_specs, nb, tq=TQ, scratch=()):
    nq = DEC_SEQ // tq
    return pl.pallas_call(
        kernel,
        grid=(nb, nq),
        in_specs=in_specs,
        out_specs=pl.BlockSpec((tq, 512), lambda b, t: (b * nq + t, 0)),
        out_shape=jax.ShapeDtypeStruct((nb * DEC_SEQ, 512), BF16),
        scratch_shapes=list(scratch),
        compiler_params=_cp(("arbitrary", "arbitrary")),
        name=name,
    )(*args)


def _na_attention(q, k, v, kc, vc, bias_ext, nb, layer):
    specs = _lat_specs(512, 512, 512, lambda b, t: (b, layer, 0, 0))
    specs.append(_layer_spec(bias_ext.shape[1:], layer, resident=True))
    tab = pltpu.VMEM(bias_ext.shape[1:3] + (GRID_W, LANES), F32)
    return _lat_call(_na_kernel, "neighbourhood_attention", (q, k, v, kc, vc, bias_ext), specs, nb,
                     scratch=(tab, tab))


def _mla_kernel(q_ref, k_ref, v_ref, kc_ref, vc_ref, o_ref):
    for hd in range(MLA_HEADS):
        ks = slice(hd * 256, (hd + 1) * 256)
        vs = slice(hd * 128, (hd + 1) * 128)
        q = q_ref[:, ks]
        scores = [_dot_nt(q, k_ref[:, ks]), _dot_nt(q, kc_ref[:, ks])]
        o_ref[:, vs] = _softmax_pv(scores, [v_ref[:, vs], vc_ref[:, vs]]).astype(BF16)


def _mla_attention(q, k, v, kc, vc, nb):
    specs = _lat_specs(1024, 1024, 512, lambda b, t: (b, 0), tq=TQ_MLA)
    return _lat_call(_mla_kernel, "latent_attention", (q, k, v, kc, vc), specs, nb, tq=TQ_MLA)


def _win_kernel(sink_ref, q_ref, k_ref, v_ref, kc_ref, vc_ref, o_ref, *, layer):
    qt = pl.program_id(1)
    nk = 4 * WINDOW
    ks = pl.multiple_of(jnp.clip(qt * TQ - WINDOW, 0, DEC_SEQ - nk), WINDOW)
    qpos = qt * TQ + lax.broadcasted_iota(jnp.int32, (TQ, nk), 0)
    kpos = ks + lax.broadcasted_iota(jnp.int32, (TQ, nk), 1)
    ok = jnp.abs(kpos - qpos) <= WINDOW

    def extra(e, idx, s):
        return jnp.where(ok, s, NEG) if idx == 0 else s

    for hp in range(4):
        cs = slice(hp * LANES, (hp + 1) * LANES)
        kcs = slice((hp // 2) * LANES, (hp // 2 + 1) * LANES)
        o = _pair_attention(q_ref[:, cs], [k_ref[pl.ds(ks, nk), kcs], kc_ref[:, kcs]],
                            [v_ref[pl.ds(ks, nk), kcs], vc_ref[:, kcs]], extra=extra,
                            sinks=(sink_ref[layer, 2 * hp], sink_ref[layer, 2 * hp + 1]))
        o_ref[:, cs] = o.astype(BF16)


def _win_attention(sink, q, k, v, kc, vc, nb, layer):
    specs = [pl.BlockSpec(memory_space=pltpu.SMEM)]
    specs += _lat_specs(512, 256, 256, lambda b, t: (b, layer, 0, 0))
    return _lat_call(functools.partial(_win_kernel, layer=layer), "window_attention", (sink, q, k, v, kc, vc),
                     specs, nb)


def _fourier_kernel(x_ref, cc_ref, sc_ref, cs_ref, ss_ref, o_ref, xc_ref, xs_ref):
    @pl.when(pl.program_id(1) == 0)
    def _():
        for g in range(FN_GROUPS):
            gs = slice(g * FN_CH, (g + 1) * FN_CH)
            xg = x_ref[:, gs]
            xc_ref[:, gs] = _dot(xg, cc_ref[...]).astype(BF16)
            xs_ref[:, gs] = _dot(xg, sc_ref[...]).astype(BF16)

    y = _dot(cs_ref[...], xc_ref[...]) - _dot(ss_ref[...], xs_ref[...])
    o_ref[...] = (y * (1.0 / math.sqrt(DEC_SEQ * FN_CH))).astype(BF16)


def _fourier(x, dft, nb):
    nq = DEC_SEQ // TQ
    return pl.pallas_call(
        _fourier_kernel,
        grid=(nb, nq),
        in_specs=[
            pl.BlockSpec((DEC_SEQ, 512), lambda b, t: (b, 0)),
            _const_spec((FN_CH, FN_CH)), _const_spec((FN_CH, FN_CH)),
            pl.BlockSpec((TQ, DEC_SEQ), lambda b, t: (t, 0)),
            pl.BlockSpec((TQ, DEC_SEQ), lambda b, t: (t, 0)),
        ],
        out_specs=pl.BlockSpec((TQ, 512), lambda b, t: (b * nq + t, 0)),
        out_shape=jax.ShapeDtypeStruct((nb * DEC_SEQ, 512), BF16),
        scratch_shapes=[pltpu.VMEM((DEC_SEQ, 512), BF16), pltpu.VMEM((DEC_SEQ, 512), BF16)],
        compiler_params=_cp(("arbitrary", "arbitrary")),
        name="fourier_mix",
    )(x, dft['cc'], dft['sc'], dft['cs_lat'], dft['ss_lat'])


def _out_kernel(oa, ob, oc, od, x_ref, mod_ref, gffn_ref, w_ref, x1_o, h2_o):
    o = jnp.concatenate([oa[...], ob[...], oc[...], od[...]], axis=-1)
    acc = _dot(o, w_ref[...])
    x1 = x_ref[...] + mod_ref[0, 2:3, :] * acc
    x1_o[...] = x1
    rinv = lax.rsqrt(jnp.mean(x1 * x1, axis=-1, keepdims=True) + EPS)
    h2 = (x1 * rinv) * (gffn_ref[...] * (1.0 + mod_ref[0, 4:5, :])) + mod_ref[0, 3:4, :]
    h2_o[...] = h2.astype(BF16)


def _mod_map(layer, tm, latent):
    if latent:
        return lambda i, *_: (layer * MOD_ROWS + 1 + (i * tm) // DEC_SEQ, 0, 0)
    return lambda i, *_: (layer * MOD_ROWS, 0, 0)


def _out_project(o4, x, mod, layer, p, latent):
    n = x.shape[0]
    tm = TM_OUT
    row = lambda i: (i, 0)
    return pl.pallas_call(
        _out_kernel,
        grid=(n // tm,),
        in_specs=[pl.BlockSpec((tm, 512), row)] * 4 + [
            pl.BlockSpec((tm, D_MODEL), row),
            pl.BlockSpec((1, MOD_CHUNKS, D_MODEL), _mod_map(layer, tm, latent)),
            _layer_spec((1, D_MODEL), layer),
            _layer_spec((D_MODEL, D_MODEL), layer, resident=True),
        ],
        out_specs=[pl.BlockSpec((tm, D_MODEL), row), pl.BlockSpec((tm, D_MODEL), row)],
        out_shape=[jax.ShapeDtypeStruct((n, D_MODEL), F32), jax.ShapeDtypeStruct((n, D_MODEL), BF16)],
        compiler_params=_cp(("arbitrary",)),
        name="out_proj_latent" if latent else "out_proj_context",
    )(*o4, x, mod, p['g_ffn'], p['w_out'])


def _ffn_kernel(h_ref, hp_ref, hn_ref, x1_ref, mod_ref, wg_ref, wu_ref, wc_ref, wd_ref, o_ref, hext_ref, acc_ref,
                *, seq_len):
    i = pl.program_id(0)
    j = pl.program_id(1)
    tm = h_ref.shape[0]

    @pl.when(j == 0)
    def _():
        hext_ref[0:HALO, :] = hp_ref[...]
        hext_ref[HALO:HALO + tm, :] = h_ref[...]
        hext_ref[HALO + tm:, :] = hn_ref[...]
        acc_ref[...] = jnp.zeros_like(acc_ref)

    g_ext = _dot(hext_ref[...], wg_ref[...])
    rows_ext = g_ext.shape[0]
    pos = (i * tm + lax.broadcasted_iota(jnp.int32, (tm, 1), 0)) % seq_len
    g_mid = g_ext[HALO:HALO + tm]
    g_prev = pltpu.roll(g_ext, 1, 0)[HALO:HALO + tm]
    g_next = pltpu.roll(g_ext, rows_ext - 1, 0)[HALO:HALO + tm]
    g_prev = jnp.where(pos == 0, 0.0, g_prev)
    g_next = jnp.where(pos == seq_len - 1, 0.0, g_next)
    g = g_prev * wc_ref[0:1, :] + g_mid * wc_ref[1:2, :] + g_next * wc_ref[2:3, :]
    u = _dot(hext_ref[HALO:HALO + tm, :], wu_ref[...])
    act = (g / (1.0 + jnp.exp(-g))) * u
    acc_ref[...] += _dot(act.astype(BF16), wd_ref[...])

    @pl.when(j == pl.num_programs(1) - 1)
    def _():
        o_ref[...] = x1_ref[...] + mod_ref[0, 5:6, :] * acc_ref[...]


def _ffn(h2, x1, mod, layer, p, latent):
    n = x1.shape[0]
    tm, tf = TM_FFN, TF_FFN
    seq_len = DEC_SEQ if latent else SEQ
    assert n % tm == 0 and tm % HALO == 0 and D_FF % tf == 0
    hb = tm // HALO
    nhalo = n // HALO
    nt = n // tm
    switch = (D_FF // tf) // 2
    return pl.pallas_call(
        functools.partial(_ffn_kernel, seq_len=seq_len),
        grid=(nt, D_FF // tf),
        in_specs=[
            pl.BlockSpec((tm, D_MODEL), lambda i, j: (jnp.where(j < switch, i, jnp.minimum(i + 1, nt - 1)), 0)),
            pl.BlockSpec((HALO, D_MODEL), lambda i, j: (jnp.maximum(i * hb - 1, 0), 0)),
            pl.BlockSpec((HALO, D_MODEL), lambda i, j: (jnp.minimum((i + 1) * hb, nhalo - 1), 0)),
            pl.BlockSpec((tm, D_MODEL), lambda i, j: (jnp.where(j < switch, jnp.maximum(i - 1, 0), i), 0)),
            pl.BlockSpec((1, MOD_CHUNKS, D_MODEL), _mod_map(layer, tm, latent)),
            pl.BlockSpec((None, D_MODEL, tf), lambda i, j: (layer, 0, j)),
            pl.BlockSpec((None, D_MODEL, tf), lambda i, j: (layer, 0, j)),
            pl.BlockSpec((None, 8, tf), lambda i, j: (layer, 0, j)),
            pl.BlockSpec((None, tf, D_MODEL), lambda i, j: (layer, j, 0)),
        ],
        out_specs=pl.BlockSpec((tm, D_MODEL), lambda i, j: (i, 0)),
        out_shape=jax.ShapeDtypeStruct((n, D_MODEL), F32),
        scratch_shapes=[pltpu.VMEM((tm + 2 * HALO, D_MODEL), BF16), pltpu.VMEM((tm, D_MODEL), F32)],
        compiler_params=_cp(("arbitrary", "arbitrary")),
        name="ffn_latent" if latent else "ffn_context",
    )(h2, h2, h2, x1, mod, p['w_gate'], p['w_up'], p['w_conv'], p['w_down'])


def _tile_row(g, reps, scale=1.0):
    return (jnp.tile(g, (1, reps)) * scale)[:, None, :]


def _prep(w):
    win_r = jnp.pad(w['w_in'].astype(BF16), ((0, 0), (0, 0), (0, IN_COLS_P - IN_COLS)))

    perm = np.array([j + 16 if j % 32 < 16 else j - 16 for j in range(MLA_ROPE)])
    half_pad = lambda t: jnp.concatenate([t, jnp.zeros(t.shape[:-1] + (LANES - MLA_ROPE,), F32)], axis=-1)
    wq = w['w_q_up'].reshape(DEPTH, MLA_Q_LORA, MLA_HEADS, MLA_QK)
    wq_r = jnp.concatenate([wq, jnp.zeros((DEPTH, MLA_Q_LORA, MLA_HEADS, 256 - MLA_QK), F32)], axis=-1)
    wq_p = half_pad(wq[..., MLA_NOPE:][..., perm])
    wq_r = jnp.concatenate([wq_r.reshape(DEPTH, MLA_Q_LORA, -1), wq_p.reshape(DEPTH, MLA_Q_LORA, -1)],
                           axis=-1).astype(BF16)
    wkv = w['w_kv_up'].reshape(DEPTH, MLA_KV_LORA, MLA_HEADS, 2 * MLA_NOPE)
    wkv_r = jnp.concatenate([wkv[..., :MLA_NOPE].reshape(DEPTH, MLA_KV_LORA, -1),
                             wkv[..., MLA_NOPE:].reshape(DEPTH, MLA_KV_LORA, -1)], axis=-1).astype(BF16)

    def pad_head(g, s):
        return _tile_row(jnp.concatenate([g, jnp.zeros((DEPTH, 256 - MLA_QK), F32)], axis=-1), MLA_HEADS, s)

    def partner_gain(g, s):
        return _tile_row(half_pad(g[:, MLA_NOPE:][:, perm]), MLA_HEADS, s)

    seg = np.kron(np.eye(LANES // HEAD_DIM), np.full((HEAD_DIM, HEAD_DIM), 1.0 / HEAD_DIM)).astype(np.float32)
    wc = jnp.concatenate([w['w_conv'], jnp.zeros((DEPTH, 5, D_FF), F32)], axis=1)
    return dict(
        g_mix=w['g_mix'][:, None, :], g_ffn=w['g_ffn'][:, None, :],
        w_in=win_r, w_q_up=wq_r, w_kv_up=wkv_r,
        gq_na=_tile_row(w['g_qn_na'], 8, HEAD_DIM ** -0.5 * LOG2E), gk_na=_tile_row(w['g_kn_na'], 8),
        g_q_lora=w['g_q_lora'][:, None, :], g_kv_lora=w['g_kv_lora'][:, None, :],
        gq_mla=pad_head(w['g_qn_mla'], MLA_QK ** -0.5 * LOG2E), gk_mla=pad_head(w['g_kn_mla'], 1.0),
        gqp_mla=partner_gain(w['g_qn_mla'], MLA_QK ** -0.5 * LOG2E), gkp_mla=partner_gain(w['g_kn_mla'], 1.0),
        seg=jnp.asarray(seg).astype(BF16),
        gq_win=_tile_row(w['g_qn_win'], 8, HEAD_DIM ** -0.5 * LOG2E), gk_win=_tile_row(w['g_kn_win'], 2),
        sink=w['sink_win'] * LOG2E,
        w_out=w['w_out'].astype(BF16), w_gate=w['w_gate'].astype(BF16), w_up=w['w_up'].astype(BF16),
        w_conv=wc, w_down=w['w_down'].astype(BF16),
    )


def _na_bias_rows(rpb):
    lead = rpb.shape[:-2]
    ext = jnp.concatenate([jnp.repeat(rpb[..., :1], 48, axis=-1), rpb, jnp.repeat(rpb[..., -1:], 49, axis=-1)], axis=-1)
    return jnp.concatenate([ext, jnp.full(lead + (1, LANES), NEG, F32)], axis=-2)


@functools.lru_cache(maxsize=None)
def _dft_tables():
    def cs(n):
        k = (np.arange(n)[:, None] * np.arange(n)[None, :]) % n
        ang = 2.0 * np.pi * k.astype(np.float64) / n
        return np.cos(ang), np.sin(ang)

    out = {}
    for name, n in (('ctx', SEQ), ('lat', DEC_SEQ)):
        c, s = cs(n)
        out['cs_' + name], out['ss_' + name] = c, s
    out['cc'], out['sc'] = cs(FN_CH)
    return out


@functools.lru_cache(maxsize=None)
def _rope_tables():
    t = np.arange(DEC_SEQ)
    quarter = MLA_ROPE // 4
    inv = ROPE_BASE ** (-np.arange(quarter, dtype=np.float64) / quarter)
    j = np.arange(MLA_ROPE)
    pos = np.where(j[None, :] < MLA_ROPE // 2, (t // GRID_W)[:, None], (t % GRID_W)[:, None]).astype(np.float64)
    ang = pos * inv[j % quarter][None, :]
    sign = np.where((j % 32) < 16, -1.0, 1.0)
    cos = np.cos(ang)
    sin = np.sin(ang) * sign[None, :]
    return np.tile(cos, (1, 2)).astype(np.float32), np.tile(sin, (1, 2)).astype(np.float32)


def kernel(x_prompt, x_sample, cache_na_k, cache_na_v, cache_mla_ckv, cache_mla_krope, cache_win_k, cache_win_v,
           c, c_ctx, w_mod, b_mod, g_mix, g_ffn, w_in, g_qn_na, g_kn_na, rpb_na, g_q_lora, w_q_up, g_kv_lora,
           w_kv_up, g_qn_mla, g_kn_mla, g_qn_win, g_kn_win, sink_win, w_out, w_gate, w_up, w_conv, w_down):
    w = dict(g_mix=g_mix, g_ffn=g_ffn, w_in=w_in, g_qn_na=g_qn_na, g_kn_na=g_kn_na, g_q_lora=g_q_lora,
             w_q_up=w_q_up, g_kv_lora=g_kv_lora, w_kv_up=w_kv_up, g_qn_mla=g_qn_mla, g_kn_mla=g_kn_mla,
             g_qn_win=g_qn_win, g_kn_win=g_kn_win, sink_win=sink_win, w_out=w_out, w_gate=w_gate, w_up=w_up,
             w_conv=w_conv, w_down=w_down)
    nb_ctx, nb_lat = x_prompt.shape[0], x_sample.shape[0]
    dft = {k: jnp.asarray(v, F32).astype(BF16) for k, v in _dft_tables().items()}
    rope_tabs = tuple(jnp.asarray(t) for t in _rope_tables())

    mod = _modulation(jnp.concatenate([c_ctx[None, :], c], axis=0), w_mod, b_mod)

    xp = x_prompt.reshape(nb_ctx * SEQ, D_MODEL)
    xs = x_sample.reshape(nb_lat * DEC_SEQ, D_MODEL)
    caches = []
    p = _prep(w)
    nctx = nb_lat * DEPTH * PAST_LEN
    dup = lambda t: jnp.concatenate([t[..., :64], t[..., :64], t[..., 64:], t[..., 64:]], axis=-1).astype(BF16)
    kna_c = cache_na_k.reshape(nb_lat, DEPTH, PAST_LEN, 512).astype(BF16)
    vna_c = cache_na_v.reshape(nb_lat, DEPTH, PAST_LEN, 512).astype(BF16)
    kw_c = dup(cache_win_k.reshape(nb_lat, DEPTH, PAST_LEN, 128))
    vw_c = dup(cache_win_v.reshape(nb_lat, DEPTH, PAST_LEN, 128))
    ckv_c = cache_mla_ckv.reshape(nctx, MLA_KV_LORA)
    kr_c = jnp.pad(cache_mla_krope.reshape(nctx, MLA_ROPE), ((0, 0), (0, LANES - MLA_ROPE)))
    bias_ext = _na_bias_rows(rpb_na * LOG2E)
    for l in range(DEPTH):

        pr = _project(xp, mod, l, p, False, None)
        caches.append(pr[10:])
        o4 = _ctx_attention(pr, p['sink'], dft, l)
        x1, h2 = _out_project(o4, xp, mod, l, p, False)
        xp = _ffn(h2, x1, mod, l, p, False)

        qna, kna, vna, qm, km, vm, qw, kw, vw, fv = _project(xs, mod, l, p, True, rope_tabs)
        km_c, vm_c = _mla_ctx(ckv_c, kr_c, p, l, nb_lat)
        o_a = _na_attention(qna, kna, vna, kna_c, vna_c, bias_ext, nb_lat, l)
        o_b = _mla_attention(qm, km, vm, km_c, vm_c, nb_lat)
        o_c = _win_attention(p['sink'], qw, kw, vw, kw_c, vw_c, nb_lat, l)
        o_d = _fourier(fv, dft, nb_lat)
        x1, h2 = _out_project((o_a, o_b, o_c, o_d), xs, mod, l, p, True)
        xs = _ffn(h2, x1, mod, l, p, True)

    def stack(idx, shape):
        return jnp.stack([caches[l][idx].reshape((nb_ctx, SEQ) + shape) for l in range(DEPTH)], axis=1)

    return (xp.reshape(nb_ctx, SEQ, D_MODEL), xs.reshape(nb_lat, DEC_SEQ, D_MODEL),
            stack(0, (8, HEAD_DIM)), stack(1, (8, HEAD_DIM)), stack(2, (MLA_KV_LORA,)), stack(3, (MLA_ROPE,)),
            stack(4, (2, HEAD_DIM)), stack(5, (2, HEAD_DIM)))
```

```python
import functools
import math

import numpy as np
import jax
import jax.numpy as jnp
from jax import lax
from jax.experimental import pallas as pl
from jax.experimental.pallas import tpu as pltpu

F32 = jnp.float32
BF16 = jnp.bfloat16

D_MODEL = 2048
DEPTH = 2
SEQ = 256
DEC_SEQ = 2048
PAST_LEN = 512
GRID_W = 64
HEAD_DIM = 64
GROUP_W = 512
MLA_HEADS = 4
MLA_NOPE = 128
MLA_ROPE = 64
MLA_QK = MLA_NOPE + MLA_ROPE
MLA_Q_LORA = 384
MLA_KV_LORA = 128
WINDOW = 128
FN_GROUPS = 4
FN_CH = 128
D_FF = 5632
MOD_CHUNKS = 6
ROPE_BASE = 10000.0
EPS = 1e-6
NEG = -1e30
LOG2E = math.log2(math.e)

LANES = 128
MOD_ROWS = 8
VMEM_LIMIT = 56 * 1024 * 1024

C_AQ, C_AK, C_AV = 0, 512, 1024
C_MQ = 1536
C_CKV = 1920
C_KR = 2048
IN_COLS = 3392
IN_COLS_P = 3456

TM_PROJ = 512
TM_OUT = 512
TM_FFN = 512
TF_FFN = 512
HALO = 16
TQ = 256
TQ_MLA = 512
TQ_FN = 1024
NA_KH = 8
NA_KW = 16
NA_KROWS = 12
NA_MASKED = 15


def _cp(sem):
    return pltpu.CompilerParams(dimension_semantics=sem, vmem_limit_bytes=VMEM_LIMIT)


def _const_spec(shape, resident=False):
    n = len(shape)
    mode = dict(pipeline_mode=pl.Buffered(1)) if resident else {}
    return pl.BlockSpec(shape, lambda *_: (0,) * n, **mode)


def _layer_spec(shape, layer, resident=False):
    n = len(shape)
    mode = dict(pipeline_mode=pl.Buffered(1)) if resident else {}
    return pl.BlockSpec((None,) + tuple(shape), lambda *_: (layer,) + (0,) * n, **mode)


def _dot(a, b):
    return jnp.dot(a, b, preferred_element_type=F32)


def _dot_nt(a, b):
    return lax.dot_general(a, b, (((1,), (1,)), ((), ())), preferred_element_type=F32)


def _rms(t, n):
    return t * lax.rsqrt(jnp.sum(t * t, axis=-1, keepdims=True) * (1.0 / n) + EPS)


def _seg_rms64(t, seg_ref):
    outs = []
    for c in range(t.shape[1] // LANES):
        tc = t[:, c * LANES:(c + 1) * LANES]
        ms = _dot((tc * tc).astype(BF16), seg_ref[...])
        outs.append(tc * lax.rsqrt(ms + EPS))
    return jnp.concatenate(outs, axis=-1)


def _partner128(t):
    first = (lax.broadcasted_iota(jnp.int32, t.shape, 1) % 32) < 16
    return jnp.where(first, pltpu.roll(t, LANES - 16, 1), pltpu.roll(t, 16, 1))


def _rope128(t, cos, sin_signed):
    return t * cos + _partner128(t) * sin_signed


def _rope_wide(t, cos, sin_signed):
    return jnp.concatenate(
        [_rope128(t[:, c * LANES:(c + 1) * LANES], cos, sin_signed) for c in range(t.shape[1] // LANES)], axis=-1)


MOD_TK = 128
MOD_LHS_ROWS = 16


def _split_bf16(t):
    hi = t.astype(BF16)
    return hi, (t - hi.astype(F32)).astype(BF16)


def _mod_kernel(c_ref, w_ref, b_ref, o_ref):
    @pl.when(pl.program_id(1) == 0)
    def _():
        o_ref[...] = jnp.broadcast_to(b_ref[...], o_ref.shape)

    c = c_ref[...]
    x_hi, x_lo = _split_bf16(c / (1.0 + jnp.exp(-c)))
    w_hi, w_lo = _split_bf16(w_ref[...])
    acc = _dot(x_hi, w_hi) + _dot(x_lo, w_hi) + _dot(x_hi, w_lo)
    o_ref[...] += acc[:MOD_ROWS]


def _modulation(cvecs, w_mod, b_mod):
    nvec = cvecs.shape[0]
    c = jnp.zeros((MOD_LHS_ROWS, D_MODEL), F32).at[:nvec].set(cvecs)
    ncol = w_mod.shape[2]
    out = pl.pallas_call(
        _mod_kernel,
        grid=(DEPTH, D_MODEL // MOD_TK),
        in_specs=[
            pl.BlockSpec((MOD_LHS_ROWS, MOD_TK), lambda l, k: (0, k)),
            pl.BlockSpec((None, MOD_TK, ncol), lambda l, k: (l, k, 0)),
            pl.BlockSpec((None, 1, ncol), lambda l, k: (l, 0, 0)),
        ],
        out_specs=pl.BlockSpec((None, MOD_ROWS, ncol), lambda l, k: (l, 0, 0)),
        out_shape=jax.ShapeDtypeStruct((DEPTH, MOD_ROWS, ncol), F32),
        compiler_params=_cp(("arbitrary", "arbitrary")),
        name="modulation",
    )(c, w_mod, b_mod.reshape(DEPTH, 1, ncol))
    return out.reshape(DEPTH * MOD_ROWS, MOD_CHUNKS, D_MODEL)


def _mla_kv_heads(ckv_n, kr, wkv_ref, gk_ref, gkp_ref, rope):
    kv = _dot(ckv_n.astype(BF16), wkv_ref[...])
    ss_r = jnp.sum(kr * kr, axis=-1, keepdims=True)
    pk = _partner128(kr) if rope is not None else None
    ks = []
    for h in range(MLA_HEADS):
        kn = kv[:, h * MLA_NOPE:(h + 1) * MLA_NOPE]
        ms = (jnp.sum(kn * kn, axis=-1, keepdims=True) + ss_r) * (1.0 / MLA_QK)
        r = lax.rsqrt(ms + EPS)
        tail = kr * r * gk_ref[:, h * 256 + 128:(h + 1) * 256]
        if rope is not None:
            tail = tail * rope[0] + (pk * r * gkp_ref[:, h * LANES:(h + 1) * LANES]) * rope[1]
        ks.append(kn * r * gk_ref[:, h * 256:h * 256 + 128])
        ks.append(tail)
    return jnp.concatenate(ks, axis=-1), kv[:, MLA_HEADS * MLA_NOPE:]


def _proj_kernel(*refs, latent):
    (x_ref, mod_ref, gmix_ref, win_ref, wq_ref, wkv_ref, gqna_ref, gkna_ref, gql_ref, gkvl_ref,
     gqm_ref, gkm_ref, gqw_ref, gkw_ref, gqp_ref, gkp_ref, seg_ref) = refs[:17]
    pos = 17
    rope = None
    if latent:
        rope = (refs[17][...], refs[18][...])
        pos = 19
    (qna_o, kna_o, vna_o, qm_o, km_o, vm_o, qw_o, kw_o, vw_o, fv_o) = refs[pos:pos + 10]
    cache_o = refs[pos + 10:]

    x = x_ref[...]
    rinv = lax.rsqrt(jnp.mean(x * x, axis=-1, keepdims=True) + EPS)
    h = (x * rinv) * (gmix_ref[...] * (1.0 + mod_ref[0, 1:2, :])) + mod_ref[0, 0:1, :]
    hb = h.astype(BF16)

    projected = _dot(hb, win_ref[...])

    def proj(c0, c1):
        return projected[:, c0:c1]

    qna_o[...] = (_seg_rms64(proj(C_AQ, C_AK), seg_ref) * gqna_ref[...]).astype(BF16)
    k_na = _seg_rms64(proj(C_AK, C_AV), seg_ref) * gkna_ref[...]
    kna_o[...] = k_na.astype(BF16)
    v_na = proj(C_AV, C_MQ)
    vna_o[...] = v_na.astype(BF16)

    cq = (_rms(proj(C_MQ, C_CKV), MLA_Q_LORA) * gql_ref[...]).astype(BF16)
    qu = _dot(cq, wq_ref[...] if latent else wq_ref[:, :MLA_HEADS * 256])
    qs = []
    for hd in range(MLA_HEADS):
        blk = qu[:, hd * 256:(hd + 1) * 256]
        rinv = lax.rsqrt(jnp.sum(blk * blk, axis=-1, keepdims=True) * (1.0 / MLA_QK) + EPS)
        blk = blk * rinv * gqm_ref[:, hd * 256:(hd + 1) * 256]
        if latent:
            ps = slice(MLA_HEADS * 256 + hd * LANES, MLA_HEADS * 256 + (hd + 1) * LANES)
            partner = qu[:, ps] * rinv * gqp_ref[:, hd * LANES:(hd + 1) * LANES]
            blk = jnp.concatenate([blk[:, :LANES], blk[:, LANES:] * rope[0] + partner * rope[1]], axis=-1)
        qs.append(blk)
    qm_o[...] = jnp.concatenate(qs, axis=-1).astype(BF16)
    ckv_n = _rms(proj(C_CKV, C_KR), MLA_KV_LORA) * gkvl_ref[...]

    tail = proj(C_KR, IN_COLS_P)
    lo = lax.broadcasted_iota(jnp.int32, (x.shape[0], LANES), 1) < HEAD_DIM
    nslab = (IN_COLS_P - C_KR) // LANES
    slabs = [tail[:, c * LANES:(c + 1) * LANES] for c in range(nslab)]
    swapped = [pltpu.roll(t, HEAD_DIM, 1) for t in slabs]
    al = [jnp.where(lo, swapped[c], swapped[c + 1]) for c in range(nslab - 1)]
    kr = jnp.where(lo, slabs[0], 0.0)
    k_m, v_m = _mla_kv_heads(ckv_n, kr, wkv_ref, gkm_ref, gkp_ref, rope)
    km_o[...] = k_m.astype(BF16)
    vm_o[...] = v_m.astype(BF16)

    def dup(t):
        sw = pltpu.roll(t, HEAD_DIM, 1)
        return jnp.concatenate([jnp.where(lo, t, sw), jnp.where(lo, sw, t)], axis=-1)

    q_w = _seg_rms64(jnp.concatenate(al[0:4], axis=-1), seg_ref) * gqw_ref[...]
    k_w = _seg_rms64(al[4], seg_ref) * gkw_ref[...]
    v_w = al[5]
    if latent:
        q_w = _rope_wide(q_w, *rope)
        k_w = _rope128(k_w, *rope)
    qw_o[...] = q_w.astype(BF16)
    kw_o[...] = dup(k_w).astype(BF16)
    vw_o[...] = dup(v_w).astype(BF16)

    fv_o[...] = jnp.concatenate(al[6:10], axis=-1).astype(BF16)

    if not latent:
        kna32_o, vna32_o, ckv32_o, kr32_o, kw32_o, vw32_o = cache_o
        kna32_o[...] = k_na
        vna32_o[...] = v_na
        ckv32_o[...] = ckv_n
        kr32_o[...] = kr[:, :MLA_ROPE]
        kw32_o[...] = k_w
        vw32_o[...] = v_w


def _project(x, mod, layer, p, latent, rope_tabs):
    n = x.shape[0]
    tm = TM_PROJ
    row = lambda i: (i, 0)
    if latent:
        mod_map = lambda i: (layer * MOD_ROWS + 1 + (i * tm) // DEC_SEQ, 0, 0)
    else:
        mod_map = lambda i: (layer * MOD_ROWS, 0, 0)
    in_specs = [
        pl.BlockSpec((tm, D_MODEL), row),
        pl.BlockSpec((1, MOD_CHUNKS, D_MODEL), mod_map),
        _layer_spec((1, D_MODEL), layer),
        _layer_spec((D_MODEL, IN_COLS_P), layer, resident=True),
        _layer_spec((MLA_Q_LORA, MLA_HEADS * (256 + LANES)), layer, resident=True),
        _layer_spec((MLA_KV_LORA, 1024), layer, resident=True),
    ] + [_layer_spec((1, wd), layer) for wd in (512, 512, MLA_Q_LORA, MLA_KV_LORA, 1024, 1024, 512, 128, 512, 512)]
    in_specs.append(_const_spec((LANES, LANES)))
    args = [x, mod, p['g_mix'], p['w_in'], p['w_q_up'], p['w_kv_up'], p['gq_na'], p['gk_na'], p['g_q_lora'],
            p['g_kv_lora'], p['gq_mla'], p['gk_mla'], p['gq_win'], p['gk_win'], p['gqp_mla'], p['gkp_mla'],
            p['seg']]
    if latent:
        nt = DEC_SEQ // tm
        in_specs += [pl.BlockSpec((tm, LANES), lambda i: (i % nt, 0))] * 2
        args += list(rope_tabs)
    widths = [512, 512, 512, 1024, 1024, 512, 512, 256, 256, 512]
    out_specs = [pl.BlockSpec((tm, w), row) for w in widths]
    out_shape = [jax.ShapeDtypeStruct((n, w), BF16) for w in widths]
    if not latent:
        cache_w = [512, 512, MLA_KV_LORA, MLA_ROPE, 128, 128]
        out_specs += [pl.BlockSpec((tm, w), row) for w in cache_w]
        out_shape += [jax.ShapeDtypeStruct((n, w), F32) for w in cache_w]
    return pl.pallas_call(
        functools.partial(_proj_kernel, latent=latent),
        grid=(n // tm,),
        in_specs=in_specs,
        out_specs=out_specs,
        out_shape=out_shape,
        compiler_params=_cp(("arbitrary",)),
        name="proj_latent" if latent else "proj_context",
    )(*args)


def _mla_ctx_kernel(ckv_ref, kr_ref, wkv_ref, gk_ref, k_o, v_o):
    k_m, v_m = _mla_kv_heads(ckv_ref[...], kr_ref[...], wkv_ref, gk_ref, None, None)
    k_o[...] = k_m.astype(BF16)
    v_o[...] = v_m.astype(BF16)


def _mla_ctx(ckv, kr128, p, layer, nb):
    n = nb * PAST_LEN
    src = lambda b: (b * DEPTH + layer, 0)
    row = lambda b: (b, 0)
    return pl.pallas_call(
        _mla_ctx_kernel,
        grid=(nb,),
        in_specs=[pl.BlockSpec((PAST_LEN, MLA_KV_LORA), src), pl.BlockSpec((PAST_LEN, LANES), src),
                  _layer_spec((MLA_KV_LORA, 1024), layer), _layer_spec((1, 1024), layer)],
        out_specs=[pl.BlockSpec((PAST_LEN, 1024), row), pl.BlockSpec((PAST_LEN, 512), row)],
        out_shape=[jax.ShapeDtypeStruct((n, 1024), BF16), jax.ShapeDtypeStruct((n, 512), BF16)],
        compiler_params=_cp(("arbitrary",)),
        name="mla_cached_kv",
    )(ckv, kr128, p['w_kv_up'], p['gk_mla'])


def _softmax_pv(scores, values, sink=None):
    m = functools.reduce(jnp.maximum, [jnp.max(s, axis=-1, keepdims=True) for s in scores])
    if sink is not None:
        m = jnp.maximum(m, sink)
    ps = [jnp.exp2(s - m) for s in scores]
    l = functools.reduce(lambda a, b: a + b, [jnp.sum(p, axis=-1, keepdims=True) for p in ps])
    if sink is not None:
        l = l + jnp.exp2(sink - m)
    o = functools.reduce(lambda a, b: a + b, [_dot(p.astype(BF16), v) for p, v in zip(ps, values)])
    return o / l


def _half_masks():
    lane = lax.broadcasted_iota(jnp.int32, (1, LANES), 1)
    lo = lane < HEAD_DIM
    return jnp.where(lo, 1.0, 0.0).astype(BF16), jnp.where(lo, 0.0, 1.0).astype(BF16)


def _pair_attention(q2, keys, values, extra=None, sinks=None):
    rows = q2.shape[0]
    lo = lax.broadcasted_iota(jnp.int32, (rows, LANES), 1) < HEAD_DIM
    outs = []
    for e, msk in enumerate(_half_masks()):
        qm = q2 * msk
        scores = []
        for idx, k2 in enumerate(keys):
            s = _dot_nt(qm, k2)
            if extra is not None:
                s = extra(e, idx, s)
            scores.append(s)
        sink = None
        if sinks is not None:
            sink = jnp.full((rows, 1), sinks[e], F32)
        outs.append(_softmax_pv(scores, values, sink))
    return jnp.where(lo, outs[0], outs[1])


def _dft_real(x, cc_ref, sc_ref, cs, ss, scale):
    outs = []
    for g in range(FN_GROUPS):
        xg = x[:, g * FN_CH:(g + 1) * FN_CH]
        xc = _dot(xg, cc_ref[...]).astype(BF16)
        xs = _dot(xg, sc_ref[...]).astype(BF16)
        outs.append(_dot(cs, xc) - _dot(ss, xs))
    return jnp.concatenate(outs, axis=-1) * scale


def _ctx_attn_kernel(sink_ref, qna, kna, vna, qm, km, vm, qw, kw, vw, fv, cs_ref, ss_ref, cc_ref, sc_ref,
                     oa, ob, oc, od, *, layer):
    for hp in range(4):
        cs = slice(hp * LANES, (hp + 1) * LANES)
        oa[:, cs] = _pair_attention(qna[:, cs], [kna[:, cs]], [vna[:, cs]]).astype(BF16)
    for hd in range(MLA_HEADS):
        s = _dot_nt(qm[:, hd * 256:(hd + 1) * 256], km[:, hd * 256:(hd + 1) * 256])
        ob[:, hd * 128:(hd + 1) * 128] = _softmax_pv([s], [vm[:, hd * 128:(hd + 1) * 128]]).astype(BF16)
    for hp in range(4):
        cs = slice(hp * LANES, (hp + 1) * LANES)
        kcs = slice((hp // 2) * LANES, (hp // 2 + 1) * LANES)
        oc[:, cs] = _pair_attention(qw[:, cs], [kw[:, kcs]], [vw[:, kcs]],
                                    sinks=(sink_ref[layer, 2 * hp], sink_ref[layer, 2 * hp + 1])).astype(BF16)
    od[...] = _dft_real(fv[...], cc_ref, sc_ref, cs_ref[...], ss_ref[...],
                        1.0 / math.sqrt(SEQ * FN_CH)).astype(BF16)


def _ctx_attention(pr, sink, dft, layer):
    qna, kna, vna, qm, km, vm, qw, kw, vw, fv = pr[:10]
    n = qna.shape[0]
    row = lambda b: (b, 0)
    widths = [512, 512, 512, 1024, 1024, 512, 512, 256, 256, 512]
    in_specs = [pl.BlockSpec(memory_space=pltpu.SMEM)]
    in_specs += [pl.BlockSpec((SEQ, w), row) for w in widths]
    in_specs += [_const_spec((SEQ, SEQ)), _const_spec((SEQ, SEQ)), _const_spec((FN_CH, FN_CH)),
                 _const_spec((FN_CH, FN_CH))]
    return pl.pallas_call(
        functools.partial(_ctx_attn_kernel, layer=layer),
        grid=(n // SEQ,),
        in_specs=in_specs,
        out_specs=[pl.BlockSpec((SEQ, 512), row)] * 4,
        out_shape=[jax.ShapeDtypeStruct((n, 512), BF16)] * 4,
        compiler_params=_cp(("arbitrary",)),
        name="context_attention",
    )(sink, qna, kna, vna, qm, km, vm, qw, kw, vw, fv, dft['cs_ctx'], dft['ss_ctx'], dft['cc'], dft['sc'])


def _na_kernel(q_ref, k_ref, v_ref, kc_ref, vc_ref, ext_ref, o_ref, tl_ref, tr_ref):
    qt = pl.program_id(1)

    @pl.when((pl.program_id(0) == 0) & (qt == 0))
    def _():
        col = lax.broadcasted_iota(jnp.int32, (GRID_W, LANES), 0)
        lane = lax.broadcasted_iota(jnp.int32, (GRID_W, LANES), 1)
        kcol = lane % GRID_W
        first = jnp.clip(col - NA_KW // 2, 0, GRID_W - NA_KW)
        in_win = (kcol >= first) & (kcol < first + NA_KW)
        left = lane < GRID_W
        for h in range(tl_ref.shape[0]):
            for d in range(tl_ref.shape[1]):
                row = jnp.broadcast_to(ext_ref[h, d:d + 1, :], (GRID_W, LANES))
                blk_l = pltpu.roll(row, GRID_W + 1, 1, stride=1, stride_axis=0)
                blk_r = pltpu.roll(row, 1, 1, stride=1, stride_axis=0)
                tl_ref[h, d] = jnp.where(left, jnp.where(in_win, blk_l, NEG), 0.0)
                tr_ref[h, d] = jnp.where(left, 0.0, jnp.where(in_win, blk_r, NEG))

    rows = DEC_SEQ // GRID_W
    qrows = TQ // GRID_W
    kstart = jnp.clip(qt * qrows - NA_KH // 2, 0, rows - NA_KROWS)
    ks = pl.multiple_of(kstart * GRID_W, GRID_W)
    nk = NA_KROWS * GRID_W

    blk_idx = []
    for i in range(qrows):
        r = qt * qrows + i
        rs = jnp.clip(r - NA_KH // 2, 0, rows - NA_KH)
        blk_idx.append([jnp.where((kstart + j >= rs) & (kstart + j < rs + NA_KH), kstart + j - r + NA_KH - 1,
                                  NA_MASKED) for j in range(NA_KROWS)])

    def bias(h):
        return jnp.concatenate([
            jnp.concatenate([tl_ref[h, blk_idx[i][2 * jp]] + tr_ref[h, blk_idx[i][2 * jp + 1]]
                             for jp in range(NA_KROWS // 2)], axis=-1)
            for i in range(qrows)], axis=0)

    for hp in range(4):
        cs = slice(hp * LANES, (hp + 1) * LANES)

        def extra(e, idx, s, hp=hp):
            return s + bias(2 * hp + e) if idx == 0 else s

        o = _pair_attention(q_ref[:, cs], [k_ref[pl.ds(ks, nk), cs], kc_ref[:, cs]],
                            [v_ref[pl.ds(ks, nk), cs], vc_ref[:, cs]], extra=extra)
        o_ref[:, cs] = o.astype(BF16)


def _lat_specs(width_q, width_k, width_v, ctx_map, tq=TQ):
    nq = DEC_SEQ // tq
    ctx_nd = len(ctx_map(0, 0))
    lead = (None,) * (ctx_nd - 2)
    return [
        pl.BlockSpec((tq, width_q), lambda b, t: (b * nq + t, 0)),
        pl.BlockSpec((DEC_SEQ, width_k), lambda b, t: (b, 0)),
        pl.BlockSpec((DEC_SEQ, width_v), lambda b, t: (b, 0)),
        pl.BlockSpec(lead + (PAST_LEN, width_k), ctx_map),
        pl.BlockSpec(lead + (PAST_LEN, width_v), ctx_map),
    ]


def _lat_call(kernel, name, args, in_specs, nb, tq=TQ, scratch=()):
    nq = DEC_SEQ // tq
    return pl.pallas_call(
        kernel,
        grid=(nb, nq),
        in_specs=in_specs,
        out_specs=pl.BlockSpec((tq, 512), lambda b, t: (b * nq + t, 0)),
        out_shape=jax.ShapeDtypeStruct((nb * DEC_SEQ, 512), BF16),
        scratch_shapes=list(scratch),
        compiler_params=_cp(("arbitrary", "arbitrary")),
        name=name,
    )(*args)


def _na_attention(q, k, v, kc, vc, bias_ext, nb, layer):
    specs = _lat_specs(512, 512, 512, lambda b, t: (b, layer, 0, 0))
    specs.append(_layer_spec(bias_ext.shape[1:], layer, resident=True))
    tab = pltpu.VMEM(bias_ext.shape[1:3] + (GRID_W, LANES), F32)
    return _lat_call(_na_kernel, "neighbourhood_attention", (q, k, v, kc, vc, bias_ext), specs, nb,
                     scratch=(tab, tab))


def _mla_kernel(q_ref, k_ref, v_ref, kc_ref, vc_ref, o_ref):
    for hd in range(MLA_HEADS):
        ks = slice(hd * 256, (hd + 1) * 256)
        vs = slice(hd * 128, (hd + 1) * 128)
        q = q_ref[:, ks]
        scores = [_dot_nt(q, k_ref[:, ks]), _dot_nt(q, kc_ref[:, ks])]
        o_ref[:, vs] = _softmax_pv(scores, [v_ref[:, vs], vc_ref[:, vs]]).astype(BF16)


def _mla_attention(q, k, v, kc, vc, nb):
    specs = _lat_specs(1024, 1024, 512, lambda b, t: (b, 0), tq=TQ_MLA)
    return _lat_call(_mla_kernel, "latent_attention", (q, k, v, kc, vc), specs, nb, tq=TQ_MLA)


def _win_kernel(sink_ref, q_ref, k_ref, v_ref, kc_ref, vc_ref, o_ref, *, layer):
    qt = pl.program_id(1)
    nk = 4 * WINDOW
    ks = pl.multiple_of(jnp.clip(qt * TQ - WINDOW, 0, DEC_SEQ - nk), WINDOW)
    qpos = qt * TQ + lax.broadcasted_iota(jnp.int32, (TQ, nk), 0)
    kpos = ks + lax.broadcasted_iota(jnp.int32, (TQ, nk), 1)
    band = jnp.where(jnp.abs(kpos - qpos) <= WINDOW, 0.0, NEG)

    def extra(e, idx, s):
        return s + band if idx == 0 else s

    for hp in range(4):
        cs = slice(hp * LANES, (hp + 1) * LANES)
        kcs = slice((hp // 2) * LANES, (hp // 2 + 1) * LANES)
        o = _pair_attention(q_ref[:, cs], [k_ref[pl.ds(ks, nk), kcs], kc_ref[:, kcs]],
                            [v_ref[pl.ds(ks, nk), kcs], vc_ref[:, kcs]], extra=extra,
                            sinks=(sink_ref[layer, 2 * hp], sink_ref[layer, 2 * hp + 1]))
        o_ref[:, cs] = o.astype(BF16)


def _win_attention(sink, q, k, v, kc, vc, nb, layer):
    specs = [pl.BlockSpec(memory_space=pltpu.SMEM)]
    specs += _lat_specs(512, 256, 256, lambda b, t: (b, layer, 0, 0))
    return _lat_call(functools.partial(_win_kernel, layer=layer), "window_attention", (sink, q, k, v, kc, vc),
                     specs, nb)


def _fourier_kernel(x_ref, cc_ref, sc_ref, cs_ref, ss_ref, o_ref, xc_ref, xs_ref):
    @pl.when(pl.program_id(1) == 0)
    def _():
        for g in range(FN_GROUPS):
            gs = slice(g * FN_CH, (g + 1) * FN_CH)
            xg = x_ref[:, gs]
            xc_ref[:, gs] = _dot(xg, cc_ref[...]).astype(BF16)
            xs_ref[:, gs] = _dot(xg, sc_ref[...]).astype(BF16)

    y = _dot(cs_ref[...], xc_ref[...]) - _dot(ss_ref[...], xs_ref[...])
    o_ref[...] = (y * (1.0 / math.sqrt(DEC_SEQ * FN_CH))).astype(BF16)


def _fourier(x, dft, nb):
    nq = DEC_SEQ // TQ_FN
    return pl.pallas_call(
        _fourier_kernel,
        grid=(nb, nq),
        in_specs=[
            pl.BlockSpec((DEC_SEQ, 512), lambda b, t: (b, 0)),
            _const_spec((FN_CH, FN_CH)), _const_spec((FN_CH, FN_CH)),
            pl.BlockSpec((TQ_FN, DEC_SEQ), lambda b, t: (t, 0)),
            pl.BlockSpec((TQ_FN, DEC_SEQ), lambda b, t: (t, 0)),
        ],
        out_specs=pl.BlockSpec((TQ_FN, 512), lambda b, t: (b * nq + t, 0)),
        out_shape=jax.ShapeDtypeStruct((nb * DEC_SEQ, 512), BF16),
        scratch_shapes=[pltpu.VMEM((DEC_SEQ, 512), BF16), pltpu.VMEM((DEC_SEQ, 512), BF16)],
        compiler_params=_cp(("arbitrary", "arbitrary")),
        name="fourier_mix",
    )(x, dft['cc'], dft['sc'], dft['cs_lat'], dft['ss_lat'])


def _out_kernel(oa, ob, oc, od, x_ref, mod_ref, gffn_ref, w_ref, x1_o, h2_o):
    o = jnp.concatenate([oa[...], ob[...], oc[...], od[...]], axis=-1)
    acc = _dot(o, w_ref[...])
    x1 = x_ref[...] + mod_ref[0, 2:3, :] * acc
    x1_o[...] = x1
    rinv = lax.rsqrt(jnp.mean(x1 * x1, axis=-1, keepdims=True) + EPS)
    h2 = (x1 * rinv) * (gffn_ref[...] * (1.0 + mod_ref[0, 4:5, :])) + mod_ref[0, 3:4, :]
    h2_o[...] = h2.astype(BF16)


def _mod_map(layer, tm, latent):
    if latent:
        return lambda i, *_: (layer * MOD_ROWS + 1 + (i * tm) // DEC_SEQ, 0, 0)
    return lambda i, *_: (layer * MOD_ROWS, 0, 0)


def _out_project(o4, x, mod, layer, p, latent):
    n = x.shape[0]
    tm = TM_OUT
    row = lambda i: (i, 0)
    return pl.pallas_call(
        _out_kernel,
        grid=(n // tm,),
        in_specs=[pl.BlockSpec((tm, 512), row)] * 4 + [
            pl.BlockSpec((tm, D_MODEL), row),
            pl.BlockSpec((1, MOD_CHUNKS, D_MODEL), _mod_map(layer, tm, latent)),
            _layer_spec((1, D_MODEL), layer),
            _layer_spec((D_MODEL, D_MODEL), layer, resident=True),
        ],
        out_specs=[pl.BlockSpec((tm, D_MODEL), row), pl.BlockSpec((tm, D_MODEL), row)],
        out_shape=[jax.ShapeDtypeStruct((n, D_MODEL), F32), jax.ShapeDtypeStruct((n, D_MODEL), BF16)],
        compiler_params=_cp(("arbitrary",)),
        name="out_proj_latent" if latent else "out_proj_context",
    )(*o4, x, mod, p['g_ffn'], p['w_out'])


def _ffn_kernel(h_ref, hp_ref, hn_ref, x1_ref, mod_ref, wg_ref, wu_ref, wc_ref, wd_ref, o_ref, hext_ref, *, seq_len):
    i = pl.program_id(0)
    j = pl.program_id(1)
    tm = h_ref.shape[0]

    @pl.when(j == 0)
    def _():
        hext_ref[0:HALO, :] = hp_ref[...]
        hext_ref[HALO:HALO + tm, :] = h_ref[...]
        hext_ref[HALO + tm:, :] = hn_ref[...]
        o_ref[...] = jnp.zeros_like(o_ref)

    g_ext = _dot(hext_ref[...], wg_ref[...])
    rows_ext = g_ext.shape[0]
    pos = (i * tm + lax.broadcasted_iota(jnp.int32, (tm, 1), 0)) % seq_len
    g_mid = g_ext[HALO:HALO + tm]
    g_prev = pltpu.roll(g_ext, 1, 0)[HALO:HALO + tm]
    g_next = pltpu.roll(g_ext, rows_ext - 1, 0)[HALO:HALO + tm]
    g_prev = jnp.where(pos == 0, 0.0, g_prev)
    g_next = jnp.where(pos == seq_len - 1, 0.0, g_next)
    g = g_prev * wc_ref[0:1, :] + g_mid * wc_ref[1:2, :] + g_next * wc_ref[2:3, :]
    u = _dot(h_ref[...], wu_ref[...])
    act = (g / (1.0 + jnp.exp(-g))) * u
    o_ref[...] += _dot(act.astype(BF16), wd_ref[...])

    @pl.when(j == pl.num_programs(1) - 1)
    def _():
        o_ref[...] = x1_ref[...] + mod_ref[0, 5:6, :] * o_ref[...]


def _ffn(h2, x1, mod, layer, p, latent):
    n = x1.shape[0]
    tm, tf = TM_FFN, TF_FFN
    seq_len = DEC_SEQ if latent else SEQ
    assert n % tm == 0 and tm % HALO == 0 and D_FF % tf == 0
    hb = tm // HALO
    nhalo = n // HALO
    return pl.pallas_call(
        functools.partial(_ffn_kernel, seq_len=seq_len),
        grid=(n // tm, D_FF // tf),
        in_specs=[
            pl.BlockSpec((tm, D_MODEL), lambda i, j: (i, 0)),
            pl.BlockSpec((HALO, D_MODEL), lambda i, j: (jnp.maximum(i * hb - 1, 0), 0)),
            pl.BlockSpec((HALO, D_MODEL), lambda i, j: (jnp.minimum((i + 1) * hb, nhalo - 1), 0)),
            pl.BlockSpec((tm, D_MODEL), lambda i, j: (i, 0)),
            pl.BlockSpec((1, MOD_CHUNKS, D_MODEL), _mod_map(layer, tm, latent)),
            pl.BlockSpec((None, D_MODEL, tf), lambda i, j: (layer, 0, j)),
            pl.BlockSpec((None, D_MODEL, tf), lambda i, j: (layer, 0, j)),
            pl.BlockSpec((None, 8, tf), lambda i, j: (layer, 0, j)),
            pl.BlockSpec((None, tf, D_MODEL), lambda i, j: (layer, j, 0)),
        ],
        out_specs=pl.BlockSpec((tm, D_MODEL), lambda i, j: (i, 0)),
        out_shape=jax.ShapeDtypeStruct((n, D_MODEL), F32),
        scratch_shapes=[pltpu.VMEM((tm + 2 * HALO, D_MODEL), BF16)],
        compiler_params=_cp(("arbitrary", "arbitrary")),
        name="ffn_latent" if latent else "ffn_context",
    )(h2, h2, h2, x1, mod, p['w_gate'], p['w_up'], p['w_conv'], p['w_down'])


def _tile_row(g, reps, scale=1.0):
    return (jnp.tile(g, (1, reps)) * scale)[:, None, :]


def _prep(w):
    win_r = jnp.pad(w['w_in'], ((0, 0), (0, 0), (0, IN_COLS_P - IN_COLS))).astype(BF16)

    perm = np.array([j + 16 if j % 32 < 16 else j - 16 for j in range(MLA_ROPE)])
    half_pad = lambda t: jnp.concatenate([t, jnp.zeros(t.shape[:-1] + (LANES - MLA_ROPE,), F32)], axis=-1)
    wq = w['w_q_up'].reshape(DEPTH, MLA_Q_LORA, MLA_HEADS, MLA_QK)
    wq_r = jnp.concatenate([wq, jnp.zeros((DEPTH, MLA_Q_LORA, MLA_HEADS, 256 - MLA_QK), F32)], axis=-1)
    wq_p = half_pad(wq[..., MLA_NOPE:][..., perm])
    wq_r = jnp.concatenate([wq_r.reshape(DEPTH, MLA_Q_LORA, -1), wq_p.reshape(DEPTH, MLA_Q_LORA, -1)],
                           axis=-1).astype(BF16)
    wkv = w['w_kv_up'].reshape(DEPTH, MLA_KV_LORA, MLA_HEADS, 2 * MLA_NOPE)
    wkv_r = jnp.concatenate([wkv[..., :MLA_NOPE].reshape(DEPTH, MLA_KV_LORA, -1),
                             wkv[..., MLA_NOPE:].reshape(DEPTH, MLA_KV_LORA, -1)], axis=-1).astype(BF16)

    def pad_head(g, s):
        return _tile_row(jnp.concatenate([g, jnp.zeros((DEPTH, 256 - MLA_QK), F32)], axis=-1), MLA_HEADS, s)

    def partner_gain(g, s):
        return _tile_row(half_pad(g[:, MLA_NOPE:][:, perm]), MLA_HEADS, s)

    seg = np.kron(np.eye(LANES // HEAD_DIM), np.full((HEAD_DIM, HEAD_DIM), 1.0 / HEAD_DIM)).astype(np.float32)
    wc = jnp.concatenate([w['w_conv'], jnp.zeros((DEPTH, 5, D_FF), F32)], axis=1)
    return dict(
        g_mix=w['g_mix'][:, None, :], g_ffn=w['g_ffn'][:, None, :],
        w_in=win_r, w_q_up=wq_r, w_kv_up=wkv_r,
        gq_na=_tile_row(w['g_qn_na'], 8, HEAD_DIM ** -0.5 * LOG2E), gk_na=_tile_row(w['g_kn_na'], 8),
        g_q_lora=w['g_q_lora'][:, None, :], g_kv_lora=w['g_kv_lora'][:, None, :],
        gq_mla=pad_head(w['g_qn_mla'], MLA_QK ** -0.5 * LOG2E), gk_mla=pad_head(w['g_kn_mla'], 1.0),
        gqp_mla=partner_gain(w['g_qn_mla'], MLA_QK ** -0.5 * LOG2E), gkp_mla=partner_gain(w['g_kn_mla'], 1.0),
        seg=jnp.asarray(seg).astype(BF16),
        gq_win=_tile_row(w['g_qn_win'], 8, HEAD_DIM ** -0.5 * LOG2E), gk_win=_tile_row(w['g_kn_win'], 2),
        sink=w['sink_win'] * LOG2E,
        w_out=w['w_out'].astype(BF16), w_gate=w['w_gate'].astype(BF16), w_up=w['w_up'].astype(BF16),
        w_conv=wc, w_down=w['w_down'].astype(BF16),
    )


def _na_bias_rows(rpb):
    lead = rpb.shape[:-2]
    ext = jnp.concatenate([jnp.repeat(rpb[..., :1], 48, axis=-1), rpb, jnp.repeat(rpb[..., -1:], 49, axis=-1)], axis=-1)
    return jnp.concatenate([ext, jnp.full(lead + (1, LANES), NEG, F32)], axis=-2)


@functools.lru_cache(maxsize=None)
def _dft_tables():
    def cs(n):
        k = (np.arange(n)[:, None] * np.arange(n)[None, :]) % n
        ang = 2.0 * np.pi * k.astype(np.float64) / n
        return np.cos(ang), np.sin(ang)

    out = {}
    for name, n in (('ctx', SEQ), ('lat', DEC_SEQ)):
        c, s = cs(n)
        out['cs_' + name], out['ss_' + name] = c, s
    out['cc'], out['sc'] = cs(FN_CH)
    return out


@functools.lru_cache(maxsize=None)
def _rope_tables():
    t = np.arange(DEC_SEQ)
    quarter = MLA_ROPE // 4
    inv = ROPE_BASE ** (-np.arange(quarter, dtype=np.float64) / quarter)
    j = np.arange(MLA_ROPE)
    pos = np.where(j[None, :] < MLA_ROPE // 2, (t // GRID_W)[:, None], (t % GRID_W)[:, None]).astype(np.float64)
    ang = pos * inv[j % quarter][None, :]
    sign = np.where((j % 32) < 16, -1.0, 1.0)
    cos = np.cos(ang)
    sin = np.sin(ang) * sign[None, :]
    return np.tile(cos, (1, 2)).astype(np.float32), np.tile(sin, (1, 2)).astype(np.float32)


def kernel(x_prompt, x_sample, cache_na_k, cache_na_v, cache_mla_ckv, cache_mla_krope, cache_win_k, cache_win_v,
           c, c_ctx, w_mod, b_mod, g_mix, g_ffn, w_in, g_qn_na, g_kn_na, rpb_na, g_q_lora, w_q_up, g_kv_lora,
           w_kv_up, g_qn_mla, g_kn_mla, g_qn_win, g_kn_win, sink_win, w_out, w_gate, w_up, w_conv, w_down):
    w = dict(g_mix=g_mix, g_ffn=g_ffn, w_in=w_in, g_qn_na=g_qn_na, g_kn_na=g_kn_na, g_q_lora=g_q_lora,
             w_q_up=w_q_up, g_kv_lora=g_kv_lora, w_kv_up=w_kv_up, g_qn_mla=g_qn_mla, g_kn_mla=g_kn_mla,
             g_qn_win=g_qn_win, g_kn_win=g_kn_win, sink_win=sink_win, w_out=w_out, w_gate=w_gate, w_up=w_up,
             w_conv=w_conv, w_down=w_down)
    nb_ctx, nb_lat = x_prompt.shape[0], x_sample.shape[0]
    dft = {k: jnp.asarray(v, F32).astype(BF16) for k, v in _dft_tables().items()}
    rope_tabs = tuple(jnp.asarray(t) for t in _rope_tables())

    mod = _modulation(jnp.concatenate([c_ctx[None, :], c], axis=0), w_mod, b_mod)

    xp = x_prompt.reshape(nb_ctx * SEQ, D_MODEL)
    xs = x_sample.reshape(nb_lat * DEC_SEQ, D_MODEL)
    caches = []
    p = _prep(w)
    nctx = nb_lat * DEPTH * PAST_LEN
    dup = lambda t: jnp.concatenate([t[..., :64], t[..., :64], t[..., 64:], t[..., 64:]], axis=-1).astype(BF16)
    kna_c = cache_na_k.reshape(nb_lat, DEPTH, PAST_LEN, 512).astype(BF16)
    vna_c = cache_na_v.reshape(nb_lat, DEPTH, PAST_LEN, 512).astype(BF16)
    kw_c = dup(cache_win_k.reshape(nb_lat, DEPTH, PAST_LEN, 128))
    vw_c = dup(cache_win_v.reshape(nb_lat, DEPTH, PAST_LEN, 128))
    ckv_c = cache_mla_ckv.reshape(nctx, MLA_KV_LORA)
    kr_c = jnp.pad(cache_mla_krope.reshape(nctx, MLA_ROPE), ((0, 0), (0, LANES - MLA_ROPE)))
    bias_ext = _na_bias_rows(rpb_na * LOG2E)
    for l in range(DEPTH):

        pr = _project(xp, mod, l, p, False, None)
        caches.append(pr[10:])
        o4 = _ctx_attention(pr, p['sink'], dft, l)
        x1, h2 = _out_project(o4, xp, mod, l, p, False)
        xp = _ffn(h2, x1, mod, l, p, False)

        qna, kna, vna, qm, km, vm, qw, kw, vw, fv = _project(xs, mod, l, p, True, rope_tabs)
        km_c, vm_c = _mla_ctx(ckv_c, kr_c, p, l, nb_lat)
        o_a = _na_attention(qna, kna, vna, kna_c, vna_c, bias_ext, nb_lat, l)
        o_b = _mla_attention(qm, km, vm, km_c, vm_c, nb_lat)
        o_c = _win_attention(p['sink'], qw, kw, vw, kw_c, vw_c, nb_lat, l)
        o_d = _fourier(fv, dft, nb_lat)
        x1, h2 = _out_project((o_a, o_b, o_c, o_d), xs, mod, l, p, True)
        xs = _ffn(h2, x1, mod, l, p, True)

    def stack(idx, shape):
        return jnp.stack([caches[l][idx].reshape((nb_ctx, SEQ) + shape) for l in range(DEPTH)], axis=1)

    return (xp.reshape(nb_ctx, SEQ, D_MODEL), xs.reshape(nb_lat, DEC_SEQ, D_MODEL),
            stack(0, (8, HEAD_DIM)), stack(1, (8, HEAD_DIM)), stack(2, (MLA_KV_LORA,)), stack(3, (MLA_ROPE,)),
            stack(4, (2, HEAD_DIM)), stack(5, (2, HEAD_DIM)))
```

```python
import functools
import math

import numpy as np
import jax
import jax.numpy as jnp
from jax import lax
from jax.experimental import pallas as pl
from jax.experimental.pallas import tpu as pltpu

F32 = jnp.float32
BF16 = jnp.bfloat16

D_MODEL = 2048
DEPTH = 2
SEQ = 256
DEC_SEQ = 2048
PAST_LEN = 512
GRID_W = 64
HEAD_DIM = 64
GROUP_W = 512
MLA_HEADS = 4
MLA_NOPE = 128
MLA_ROPE = 64
MLA_QK = MLA_NOPE + MLA_ROPE
MLA_Q_LORA = 384
MLA_KV_LORA = 128
WINDOW = 128
FN_GROUPS = 4
FN_CH = 128
D_FF = 5632
MOD_CHUNKS = 6
ROPE_BASE = 10000.0
EPS = 1e-6
NEG = -1e30
LOG2E = math.log2(math.e)

LANES = 128
MOD_ROWS = 8
VMEM_LIMIT = 56 * 1024 * 1024

C_AQ, C_AK, C_AV = 0, 512, 1024
C_MQ = 1536
C_CKV = 1920
C_KR = 2048
IN_COLS = 3392
IN_COLS_P = 3456

TM_PROJ = 512
TM_OUT = 512
TM_FFN = 512
TF_FFN = 512
HALO = 16
TQ = 256
TQ_MLA = 512
TQ_FN = 1024
NA_KH = 8
NA_KW = 16
NA_KROWS = 12
NA_MASKED = 15


def _cp(sem):
    return pltpu.CompilerParams(dimension_semantics=sem, vmem_limit_bytes=VMEM_LIMIT)


def _const_spec(shape):
    n = len(shape)
    return pl.BlockSpec(shape, lambda *_: (0,) * n)


def _layer_spec(shape, layer, resident=False):
    n = len(shape)
    mode = dict(pipeline_mode=pl.Buffered(1)) if resident else {}
    return pl.BlockSpec((None,) + tuple(shape), lambda *_: (layer,) + (0,) * n, **mode)


def _dot(a, b):
    return jnp.dot(a, b, preferred_element_type=F32)


def _dot_nt(a, b):
    return lax.dot_general(a, b, (((1,), (1,)), ((), ())), preferred_element_type=F32)


def _rms(t, n):
    return t * lax.rsqrt(jnp.sum(t * t, axis=-1, keepdims=True) * (1.0 / n) + EPS)


def _seg_rms64(t, seg_ref):
    outs = []
    for c in range(t.shape[1] // LANES):
        tc = t[:, c * LANES:(c + 1) * LANES]
        ms = _dot((tc * tc).astype(BF16), seg_ref[...])
        outs.append(tc * lax.rsqrt(ms + EPS))
    return jnp.concatenate(outs, axis=-1)


def _partner128(t):
    first = (lax.broadcasted_iota(jnp.int32, t.shape, 1) % 32) < 16
    return jnp.where(first, pltpu.roll(t, LANES - 16, 1), pltpu.roll(t, 16, 1))


def _rope128(t, cos, sin_signed):
    return t * cos + _partner128(t) * sin_signed


def _rope_wide(t, cos, sin_signed):
    return jnp.concatenate(
        [_rope128(t[:, c * LANES:(c + 1) * LANES], cos, sin_signed) for c in range(t.shape[1] // LANES)], axis=-1)


MOD_TK = 256
MOD_LHS_ROWS = 16


def _split_bf16(t):
    hi = t.astype(BF16)
    return hi, (t - hi.astype(F32)).astype(BF16)


def _mod_kernel(c_ref, w_ref, b_ref, o_ref):
    @pl.when(pl.program_id(1) == 0)
    def _():
        o_ref[...] = jnp.broadcast_to(b_ref[...], o_ref.shape)

    c = c_ref[...]
    x_hi, x_lo = _split_bf16(c / (1.0 + jnp.exp(-c)))
    w_hi, w_lo = _split_bf16(w_ref[...])
    acc = _dot(x_hi, w_hi) + _dot(x_lo, w_hi) + _dot(x_hi, w_lo)
    o_ref[...] += acc[:MOD_ROWS]


def _modulation(cvecs, w_mod, b_mod):
    nvec = cvecs.shape[0]
    c = jnp.zeros((MOD_LHS_ROWS, D_MODEL), F32).at[:nvec].set(cvecs)
    ncol = w_mod.shape[2]
    out = pl.pallas_call(
        _mod_kernel,
        grid=(DEPTH, D_MODEL // MOD_TK),
        in_specs=[
            pl.BlockSpec((MOD_LHS_ROWS, MOD_TK), lambda l, k: (0, k)),
            pl.BlockSpec((None, MOD_TK, ncol), lambda l, k: (l, k, 0)),
            pl.BlockSpec((None, 1, ncol), lambda l, k: (l, 0, 0)),
        ],
        out_specs=pl.BlockSpec((None, MOD_ROWS, ncol), lambda l, k: (l, 0, 0)),
        out_shape=jax.ShapeDtypeStruct((DEPTH, MOD_ROWS, ncol), F32),
        compiler_params=_cp(("arbitrary", "arbitrary")),
        name="modulation",
    )(c, w_mod, b_mod.reshape(DEPTH, 1, ncol))
    return out.reshape(DEPTH * MOD_ROWS, MOD_CHUNKS, D_MODEL)


def _mla_kv_heads(ckv_n, kr, wkv_ref, gk_ref, gkp_ref, rope):
    kv = _dot(ckv_n.astype(BF16), wkv_ref[...])
    ss_r = jnp.sum(kr * kr, axis=-1, keepdims=True)
    pk = _partner128(kr) if rope is not None else None
    ks = []
    for h in range(MLA_HEADS):
        kn = kv[:, h * MLA_NOPE:(h + 1) * MLA_NOPE]
        ms = (jnp.sum(kn * kn, axis=-1, keepdims=True) + ss_r) * (1.0 / MLA_QK)
        r = lax.rsqrt(ms + EPS)
        tail = kr * r * gk_ref[:, h * 256 + 128:(h + 1) * 256]
        if rope is not None:
            tail = tail * rope[0] + (pk * r * gkp_ref[:, h * LANES:(h + 1) * LANES]) * rope[1]
        ks.append(kn * r * gk_ref[:, h * 256:h * 256 + 128])
        ks.append(tail)
    return jnp.concatenate(ks, axis=-1), kv[:, MLA_HEADS * MLA_NOPE:]


def _proj_kernel(*refs, latent):
    (x_ref, mod_ref, gmix_ref, win_ref, wq_ref, wkv_ref, gqna_ref, gkna_ref, gql_ref, gkvl_ref,
     gqm_ref, gkm_ref, gqw_ref, gkw_ref, gqp_ref, gkp_ref, seg_ref) = refs[:17]
    pos = 17
    rope = None
    if latent:
        rope = (refs[17][...], refs[18][...])
        pos = 19
    (qna_o, kna_o, vna_o, qm_o, km_o, vm_o, qw_o, kw_o, vw_o, fv_o) = refs[pos:pos + 10]
    cache_o = refs[pos + 10:]

    x = x_ref[...]
    rinv = lax.rsqrt(jnp.mean(x * x, axis=-1, keepdims=True) + EPS)
    h = (x * rinv) * (gmix_ref[...] * (1.0 + mod_ref[0, 1:2, :])) + mod_ref[0, 0:1, :]
    hb = h.astype(BF16)

    projected = _dot(hb, win_ref[...])

    def proj(c0, c1):
        return projected[:, c0:c1]

    qna_o[...] = (_seg_rms64(proj(C_AQ, C_AK), seg_ref) * gqna_ref[...]).astype(BF16)
    k_na = _seg_rms64(proj(C_AK, C_AV), seg_ref) * gkna_ref[...]
    kna_o[...] = k_na.astype(BF16)
    v_na = proj(C_AV, C_MQ)
    vna_o[...] = v_na.astype(BF16)

    cq = (_rms(proj(C_MQ, C_CKV), MLA_Q_LORA) * gql_ref[...]).astype(BF16)
    qu = _dot(cq, wq_ref[...] if latent else wq_ref[:, :MLA_HEADS * 256])
    qs = []
    for hd in range(MLA_HEADS):
        blk = qu[:, hd * 256:(hd + 1) * 256]
        rinv = lax.rsqrt(jnp.sum(blk * blk, axis=-1, keepdims=True) * (1.0 / MLA_QK) + EPS)
        blk = blk * rinv * gqm_ref[:, hd * 256:(hd + 1) * 256]
        if latent:
            ps = slice(MLA_HEADS * 256 + hd * LANES, MLA_HEADS * 256 + (hd + 1) * LANES)
            partner = qu[:, ps] * rinv * gqp_ref[:, hd * LANES:(hd + 1) * LANES]
            blk = jnp.concatenate([blk[:, :LANES], blk[:, LANES:] * rope[0] + partner * rope[1]], axis=-1)
        qs.append(blk)
    qm_o[...] = jnp.concatenate(qs, axis=-1).astype(BF16)
    ckv_n = _rms(proj(C_CKV, C_KR), MLA_KV_LORA) * gkvl_ref[...]

    tail = proj(C_KR, IN_COLS_P)
    lo = lax.broadcasted_iota(jnp.int32, (x.shape[0], LANES), 1) < HEAD_DIM
    nslab = (IN_COLS_P - C_KR) // LANES
    slabs = [tail[:, c * LANES:(c + 1) * LANES] for c in range(nslab)]
    swapped = [pltpu.roll(t, HEAD_DIM, 1) for t in slabs]
    al = [jnp.where(lo, swapped[c], swapped[c + 1]) for c in range(nslab - 1)]
    kr = jnp.where(lo, slabs[0], 0.0)
    k_m, v_m = _mla_kv_heads(ckv_n, kr, wkv_ref, gkm_ref, gkp_ref, rope)
    km_o[...] = k_m.astype(BF16)
    vm_o[...] = v_m.astype(BF16)

    def dup(t):
        sw = pltpu.roll(t, HEAD_DIM, 1)
        return jnp.concatenate([jnp.where(lo, t, sw), jnp.where(lo, sw, t)], axis=-1)

    q_w = _seg_rms64(jnp.concatenate(al[0:4], axis=-1), seg_ref) * gqw_ref[...]
    k_w = _seg_rms64(al[4], seg_ref) * gkw_ref[...]
    v_w = al[5]
    if latent:
        q_w = _rope_wide(q_w, *rope)
        k_w = _rope128(k_w, *rope)
    qw_o[...] = q_w.astype(BF16)
    kw_o[...] = dup(k_w).astype(BF16)
    vw_o[...] = dup(v_w).astype(BF16)

    fv_o[...] = jnp.concatenate(al[6:10], axis=-1).astype(BF16)

    if not latent:
        kna32_o, vna32_o, ckv32_o, kr32_o, kw32_o, vw32_o = cache_o
        kna32_o[...] = k_na
        vna32_o[...] = v_na
        ckv32_o[...] = ckv_n
        kr32_o[...] = kr[:, :MLA_ROPE]
        kw32_o[...] = k_w
        vw32_o[...] = v_w


def _project(x, mod, layer, p, latent, rope_tabs):
    n = x.shape[0]
    tm = TM_PROJ
    row = lambda i: (i, 0)
    if latent:
        mod_map = lambda i: (layer * MOD_ROWS + 1 + (i * tm) // DEC_SEQ, 0, 0)
    else:
        mod_map = lambda i: (layer * MOD_ROWS, 0, 0)
    in_specs = [
        pl.BlockSpec((tm, D_MODEL), row),
        pl.BlockSpec((1, MOD_CHUNKS, D_MODEL), mod_map),
        _layer_spec((1, D_MODEL), layer),
        _layer_spec((D_MODEL, IN_COLS_P), layer, resident=True),
        _layer_spec((MLA_Q_LORA, MLA_HEADS * (256 + LANES)), layer, resident=True),
        _layer_spec((MLA_KV_LORA, 1024), layer, resident=True),
    ] + [_layer_spec((1, wd), layer) for wd in (512, 512, MLA_Q_LORA, MLA_KV_LORA, 1024, 1024, 512, 128, 512, 512)]
    in_specs.append(_const_spec((LANES, LANES)))
    args = [x, mod, p['g_mix'], p['w_in'], p['w_q_up'], p['w_kv_up'], p['gq_na'], p['gk_na'], p['g_q_lora'],
            p['g_kv_lora'], p['gq_mla'], p['gk_mla'], p['gq_win'], p['gk_win'], p['gqp_mla'], p['gkp_mla'],
            p['seg']]
    if latent:
        nt = DEC_SEQ // tm
        in_specs += [pl.BlockSpec((tm, LANES), lambda i: (i % nt, 0))] * 2
        args += list(rope_tabs)
    widths = [512, 512, 512, 1024, 1024, 512, 512, 256, 256, 512]
    out_specs = [pl.BlockSpec((tm, w), row) for w in widths]
    out_shape = [jax.ShapeDtypeStruct((n, w), BF16) for w in widths]
    if not latent:
        cache_w = [512, 512, MLA_KV_LORA, MLA_ROPE, 128, 128]
        out_specs += [pl.BlockSpec((tm, w), row) for w in cache_w]
        out_shape += [jax.ShapeDtypeStruct((n, w), F32) for w in cache_w]
    return pl.pallas_call(
        functools.partial(_proj_kernel, latent=latent),
        grid=(n // tm,),
        in_specs=in_specs,
        out_specs=out_specs,
        out_shape=out_shape,
        compiler_params=_cp(("arbitrary",)),
        name="proj_latent" if latent else "proj_context",
    )(*args)


def _mla_ctx_kernel(ckv_ref, kr_ref, wkv_ref, gk_ref, k_o, v_o):
    k_m, v_m = _mla_kv_heads(ckv_ref[...], kr_ref[...], wkv_ref, gk_ref, None, None)
    k_o[...] = k_m.astype(BF16)
    v_o[...] = v_m.astype(BF16)


def _mla_ctx(ckv, kr128, p, layer, nb):
    n = nb * PAST_LEN
    src = lambda b: (b * DEPTH + layer, 0)
    row = lambda b: (b, 0)
    return pl.pallas_call(
        _mla_ctx_kernel,
        grid=(nb,),
        in_specs=[pl.BlockSpec((PAST_LEN, MLA_KV_LORA), src), pl.BlockSpec((PAST_LEN, LANES), src),
                  _layer_spec((MLA_KV_LORA, 1024), layer), _layer_spec((1, 1024), layer)],
        out_specs=[pl.BlockSpec((PAST_LEN, 1024), row), pl.BlockSpec((PAST_LEN, 512), row)],
        out_shape=[jax.ShapeDtypeStruct((n, 1024), BF16), jax.ShapeDtypeStruct((n, 512), BF16)],
        compiler_params=_cp(("arbitrary",)),
        name="mla_cached_kv",
    )(ckv, kr128, p['w_kv_up'], p['gk_mla'])


def _softmax_pv(scores, values, sink=None):
    m = functools.reduce(jnp.maximum, [jnp.max(s, axis=-1, keepdims=True) for s in scores])
    if sink is not None:
        m = jnp.maximum(m, sink)
    ps = [jnp.exp2(s - m) for s in scores]
    l = functools.reduce(lambda a, b: a + b, [jnp.sum(p, axis=-1, keepdims=True) for p in ps])
    if sink is not None:
        l = l + jnp.exp2(sink - m)
    o = functools.reduce(lambda a, b: a + b, [_dot(p.astype(BF16), v) for p, v in zip(ps, values)])
    return o / l


def _half_masks():
    lane = lax.broadcasted_iota(jnp.int32, (1, LANES), 1)
    lo = lane < HEAD_DIM
    return jnp.where(lo, 1.0, 0.0).astype(BF16), jnp.where(lo, 0.0, 1.0).astype(BF16)


def _pair_attention(q2, keys, values, extra=None, sinks=None):
    rows = q2.shape[0]
    lo = lax.broadcasted_iota(jnp.int32, (rows, LANES), 1) < HEAD_DIM
    outs = []
    for e, msk in enumerate(_half_masks()):
        qm = q2 * msk
        scores = []
        for idx, k2 in enumerate(keys):
            s = _dot_nt(qm, k2)
            if extra is not None:
                s = extra(e, idx, s)
            scores.append(s)
        sink = None
        if sinks is not None:
            sink = jnp.full((rows, 1), sinks[e], F32)
        outs.append(_softmax_pv(scores, values, sink))
    return jnp.where(lo, outs[0], outs[1])


def _dft_real(x, cc_ref, sc_ref, cs, ss, scale):
    outs = []
    for g in range(FN_GROUPS):
        xg = x[:, g * FN_CH:(g + 1) * FN_CH]
        xc = _dot(xg, cc_ref[...]).astype(BF16)
        xs = _dot(xg, sc_ref[...]).astype(BF16)
        outs.append(_dot(cs, xc) - _dot(ss, xs))
    return jnp.concatenate(outs, axis=-1) * scale


def _ctx_attn_kernel(sink_ref, qna, kna, vna, qm, km, vm, qw, kw, vw, fv, cs_ref, ss_ref, cc_ref, sc_ref,
                     oa, ob, oc, od, *, layer):
    for hp in range(4):
        cs = slice(hp * LANES, (hp + 1) * LANES)
        oa[:, cs] = _pair_attention(qna[:, cs], [kna[:, cs]], [vna[:, cs]]).astype(BF16)
    for hd in range(MLA_HEADS):
        s = _dot_nt(qm[:, hd * 256:(hd + 1) * 256], km[:, hd * 256:(hd + 1) * 256])
        ob[:, hd * 128:(hd + 1) * 128] = _softmax_pv([s], [vm[:, hd * 128:(hd + 1) * 128]]).astype(BF16)
    for hp in range(4):
        cs = slice(hp * LANES, (hp + 1) * LANES)
        kcs = slice((hp // 2) * LANES, (hp // 2 + 1) * LANES)
        oc[:, cs] = _pair_attention(qw[:, cs], [kw[:, kcs]], [vw[:, kcs]],
                                    sinks=(sink_ref[layer, 2 * hp], sink_ref[layer, 2 * hp + 1])).astype(BF16)
    od[...] = _dft_real(fv[...], cc_ref, sc_ref, cs_ref[...], ss_ref[...],
                        1.0 / math.sqrt(SEQ * FN_CH)).astype(BF16)


def _ctx_attention(pr, sink, dft, layer):
    qna, kna, vna, qm, km, vm, qw, kw, vw, fv = pr[:10]
    n = qna.shape[0]
    row = lambda b: (b, 0)
    widths = [512, 512, 512, 1024, 1024, 512, 512, 256, 256, 512]
    in_specs = [pl.BlockSpec(memory_space=pltpu.SMEM)]
    in_specs += [pl.BlockSpec((SEQ, w), row) for w in widths]
    in_specs += [_const_spec((SEQ, SEQ)), _const_spec((SEQ, SEQ)), _const_spec((FN_CH, FN_CH)),
                 _const_spec((FN_CH, FN_CH))]
    return pl.pallas_call(
        functools.partial(_ctx_attn_kernel, layer=layer),
        grid=(n // SEQ,),
        in_specs=in_specs,
        out_specs=[pl.BlockSpec((SEQ, 512), row)] * 4,
        out_shape=[jax.ShapeDtypeStruct((n, 512), BF16)] * 4,
        compiler_params=_cp(("arbitrary",)),
        name="context_attention",
    )(sink, qna, kna, vna, qm, km, vm, qw, kw, vw, fv, dft['cs_ctx'], dft['ss_ctx'], dft['cc'], dft['sc'])


def _na_kernel(q_ref, k_ref, v_ref, kc_ref, vc_ref, ext_ref, o_ref, tl_ref, tr_ref):
    qt = pl.program_id(1)

    @pl.when((pl.program_id(0) == 0) & (qt == 0))
    def _():
        col = lax.broadcasted_iota(jnp.int32, (GRID_W, LANES), 0)
        lane = lax.broadcasted_iota(jnp.int32, (GRID_W, LANES), 1)
        kcol = lane % GRID_W
        first = jnp.clip(col - NA_KW // 2, 0, GRID_W - NA_KW)
        in_win = (kcol >= first) & (kcol < first + NA_KW)
        left = lane < GRID_W
        for h in range(tl_ref.shape[0]):
            for d in range(tl_ref.shape[1]):
                row = jnp.broadcast_to(ext_ref[h, d:d + 1, :], (GRID_W, LANES))
                blk_l = pltpu.roll(row, GRID_W + 1, 1, stride=1, stride_axis=0)
                blk_r = pltpu.roll(row, 1, 1, stride=1, stride_axis=0)
                tl_ref[h, d] = jnp.where(left, jnp.where(in_win, blk_l, NEG), 0.0)
                tr_ref[h, d] = jnp.where(left, 0.0, jnp.where(in_win, blk_r, NEG))

    rows = DEC_SEQ // GRID_W
    qrows = TQ // GRID_W
    kstart = jnp.clip(qt * qrows - NA_KH // 2, 0, rows - NA_KROWS)
    ks = pl.multiple_of(kstart * GRID_W, GRID_W)
    nk = NA_KROWS * GRID_W

    blk_idx = []
    for i in range(qrows):
        r = qt * qrows + i
        rs = jnp.clip(r - NA_KH // 2, 0, rows - NA_KH)
        blk_idx.append([jnp.where((kstart + j >= rs) & (kstart + j < rs + NA_KH), kstart + j - r + NA_KH - 1,
                                  NA_MASKED) for j in range(NA_KROWS)])

    def bias(h):
        return jnp.concatenate([
            jnp.concatenate([tl_ref[h, blk_idx[i][2 * jp]] + tr_ref[h, blk_idx[i][2 * jp + 1]]
                             for jp in range(NA_KROWS // 2)], axis=-1)
            for i in range(qrows)], axis=0)

    for hp in range(4):
        cs = slice(hp * LANES, (hp + 1) * LANES)

        def extra(e, idx, s, hp=hp):
            return s + bias(2 * hp + e) if idx == 0 else s

        o = _pair_attention(q_ref[:, cs], [k_ref[pl.ds(ks, nk), cs], kc_ref[:, cs]],
                            [v_ref[pl.ds(ks, nk), cs], vc_ref[:, cs]], extra=extra)
        o_ref[:, cs] = o.astype(BF16)


def _lat_specs(width_q, width_k, width_v, ctx_map, tq=TQ):
    nq = DEC_SEQ // tq
    ctx_nd = len(ctx_map(0, 0))
    lead = (None,) * (ctx_nd - 2)
    return [
        pl.BlockSpec((tq, width_q), lambda b, t: (b * nq + t, 0)),
        pl.BlockSpec((DEC_SEQ, width_k), lambda b, t: (b, 0)),
        pl.BlockSpec((DEC_SEQ, width_v), lambda b, t: (b, 0)),
        pl.BlockSpec(lead + (PAST_LEN, width_k), ctx_map),
        pl.BlockSpec(lead + (PAST_LEN, width_v), ctx_map),
    ]


def _lat_call(kernel, name, args, in_specs, nb, tq=TQ, scratch=()):
    nq = DEC_SEQ // tq
    return pl.pallas_call(
        kernel,
        grid=(nb, nq),
        in_specs=in_specs,
        out_specs=pl.BlockSpec((tq, 512), lambda b, t: (b * nq + t, 0)),
        out_shape=jax.ShapeDtypeStruct((nb * DEC_SEQ, 512), BF16),
        scratch_shapes=list(scratch),
        compiler_params=_cp(("arbitrary", "arbitrary")),
        name=name,
    )(*args)


def _na_attention(q, k, v, kc, vc, bias_ext, nb, layer):
    specs = _lat_specs(512, 512, 512, lambda b, t: (b, layer, 0, 0))
    specs.append(_layer_spec(bias_ext.shape[1:], layer, resident=True))
    tab = pltpu.VMEM(bias_ext.shape[1:3] + (GRID_W, LANES), F32)
    return _lat_call(_na_kernel, "neighbourhood_attention", (q, k, v, kc, vc, bias_ext), specs, nb,
                     scratch=(tab, tab))


def _mla_kernel(q_ref, k_ref, v_ref, kc_ref, vc_ref, o_ref):
    for hd in range(MLA_HEADS):
        ks = slice(hd * 256, (hd + 1) * 256)
        vs = slice(hd * 128, (hd + 1) * 128)
        q = q_ref[:, ks]
        scores = [_dot_nt(q, k_ref[:, ks]), _dot_nt(q, kc_ref[:, ks])]
        o_ref[:, vs] = _softmax_pv(scores, [v_ref[:, vs], vc_ref[:, vs]]).astype(BF16)


def _mla_attention(q, k, v, kc, vc, nb):
    specs = _lat_specs(1024, 1024, 512, lambda b, t: (b, 0), tq=TQ_MLA)
    return _lat_call(_mla_kernel, "latent_attention", (q, k, v, kc, vc), specs, nb, tq=TQ_MLA)


def _win_kernel(sink_ref, q_ref, k_ref, v_ref, kc_ref, vc_ref, o_ref, *, layer):
    qt = pl.program_id(1)
    nk = 4 * WINDOW
    ks = pl.multiple_of(jnp.clip(qt * TQ - WINDOW, 0, DEC_SEQ - nk), WINDOW)
    qpos = qt * TQ + lax.broadcasted_iota(jnp.int32, (TQ, nk), 0)
    kpos = ks + lax.broadcasted_iota(jnp.int32, (TQ, nk), 1)
    band = jnp.where(jnp.abs(kpos - qpos) <= WINDOW, 0.0, NEG)

    def extra(e, idx, s):
        return s + band if idx == 0 else s

    for hp in range(4):
        cs = slice(hp * LANES, (hp + 1) * LANES)
        kcs = slice((hp // 2) * LANES, (hp // 2 + 1) * LANES)
        o = _pair_attention(q_ref[:, cs], [k_ref[pl.ds(ks, nk), kcs], kc_ref[:, kcs]],
                            [v_ref[pl.ds(ks, nk), kcs], vc_ref[:, kcs]], extra=extra,
                            sinks=(sink_ref[layer, 2 * hp], sink_ref[layer, 2 * hp + 1]))
        o_ref[:, cs] = o.astype(BF16)


def _win_attention(sink, q, k, v, kc, vc, nb, layer):
    specs = [pl.BlockSpec(memory_space=pltpu.SMEM)]
    specs += _lat_specs(512, 256, 256, lambda b, t: (b, layer, 0, 0))
    return _lat_call(functools.partial(_win_kernel, layer=layer), "window_attention", (sink, q, k, v, kc, vc),
                     specs, nb)


def _fourier_kernel(x_ref, cc_ref, sc_ref, cs_ref, ss_ref, o_ref, xc_ref, xs_ref):
    @pl.when(pl.program_id(1) == 0)
    def _():
        for g in range(FN_GROUPS):
            gs = slice(g * FN_CH, (g + 1) * FN_CH)
            xg = x_ref[:, gs]
            xc_ref[:, gs] = _dot(xg, cc_ref[...]).astype(BF16)
            xs_ref[:, gs] = _dot(xg, sc_ref[...]).astype(BF16)

    y = _dot(cs_ref[...], xc_ref[...]) - _dot(ss_ref[...], xs_ref[...])
    o_ref[...] = (y * (1.0 / math.sqrt(DEC_SEQ * FN_CH))).astype(BF16)


def _fourier(x, dft, nb):
    nq = DEC_SEQ // TQ_FN
    return pl.pallas_call(
        _fourier_kernel,
        grid=(nb, nq),
        in_specs=[
            pl.BlockSpec((DEC_SEQ, 512), lambda b, t: (b, 0)),
            _const_spec((FN_CH, FN_CH)), _const_spec((FN_CH, FN_CH)),
            pl.BlockSpec((TQ_FN, DEC_SEQ), lambda b, t: (t, 0)),
            pl.BlockSpec((TQ_FN, DEC_SEQ), lambda b, t: (t, 0)),
        ],
        out_specs=pl.BlockSpec((TQ_FN, 512), lambda b, t: (b * nq + t, 0)),
        out_shape=jax.ShapeDtypeStruct((nb * DEC_SEQ, 512), BF16),
        scratch_shapes=[pltpu.VMEM((DEC_SEQ, 512), BF16), pltpu.VMEM((DEC_SEQ, 512), BF16)],
        compiler_params=_cp(("arbitrary", "arbitrary")),
        name="fourier_mix",
    )(x, dft['cc'], dft['sc'], dft['cs_lat'], dft['ss_lat'])


def _out_kernel(oa, ob, oc, od, x_ref, mod_ref, gffn_ref, w_ref, x1_o, h2_o):
    o = jnp.concatenate([oa[...], ob[...], oc[...], od[...]], axis=-1)
    acc = _dot(o, w_ref[...])
    x1 = x_ref[...] + mod_ref[0, 2:3, :] * acc
    x1_o[...] = x1
    rinv = lax.rsqrt(jnp.mean(x1 * x1, axis=-1, keepdims=True) + EPS)
    h2 = (x1 * rinv) * (gffn_ref[...] * (1.0 + mod_ref[0, 4:5, :])) + mod_ref[0, 3:4, :]
    h2_o[...] = h2.astype(BF16)


def _mod_map(layer, tm, latent):
    if latent:
        return lambda i, *_: (layer * MOD_ROWS + 1 + (i * tm) // DEC_SEQ, 0, 0)
    return lambda i, *_: (layer * MOD_ROWS, 0, 0)


def _out_project(o4, x, mod, layer, p, latent):
    n = x.shape[0]
    tm = TM_OUT
    row = lambda i: (i, 0)
    return pl.pallas_call(
        _out_kernel,
        grid=(n // tm,),
        in_specs=[pl.BlockSpec((tm, 512), row)] * 4 + [
            pl.BlockSpec((tm, D_MODEL), row),
            pl.BlockSpec((1, MOD_CHUNKS, D_MODEL), _mod_map(layer, tm, latent)),
            _layer_spec((1, D_MODEL), layer),
            _layer_spec((D_MODEL, D_MODEL), layer, resident=True),
        ],
        out_specs=[pl.BlockSpec((tm, D_MODEL), row), pl.BlockSpec((tm, D_MODEL), row)],
        out_shape=[jax.ShapeDtypeStruct((n, D_MODEL), F32), jax.ShapeDtypeStruct((n, D_MODEL), BF16)],
        compiler_params=_cp(("arbitrary",)),
        name="out_proj_latent" if latent else "out_proj_context",
    )(*o4, x, mod, p['g_ffn'], p['w_out'])


def _ffn_kernel(h_ref, hp_ref, hn_ref, x1_ref, mod_ref, wg_ref, wu_ref, wc_ref, wd_ref, o_ref, hext_ref, *, seq_len):
    i = pl.program_id(0)
    j = pl.program_id(1)
    tm = h_ref.shape[0]

    @pl.when(j == 0)
    def _():
        hext_ref[0:HALO, :] = hp_ref[...]
        hext_ref[HALO:HALO + tm, :] = h_ref[...]
        hext_ref[HALO + tm:, :] = hn_ref[...]
        o_ref[...] = jnp.zeros_like(o_ref)

    g_ext = _dot(hext_ref[...], wg_ref[...])
    rows_ext = g_ext.shape[0]
    pos = (i * tm + lax.broadcasted_iota(jnp.int32, (tm, 1), 0)) % seq_len
    g_mid = g_ext[HALO:HALO + tm]
    g_prev = pltpu.roll(g_ext, 1, 0)[HALO:HALO + tm]
    g_next = pltpu.roll(g_ext, rows_ext - 1, 0)[HALO:HALO + tm]
    g_prev = jnp.where(pos == 0, 0.0, g_prev)
    g_next = jnp.where(pos == seq_len - 1, 0.0, g_next)
    g = g_prev * wc_ref[0:1, :] + g_mid * wc_ref[1:2, :] + g_next * wc_ref[2:3, :]
    u = _dot(h_ref[...], wu_ref[...])
    act = (g / (1.0 + jnp.exp(-g))) * u
    o_ref[...] += _dot(act.astype(BF16), wd_ref[...])

    @pl.when(j == pl.num_programs(1) - 1)
    def _():
        o_ref[...] = x1_ref[...] + mod_ref[0, 5:6, :] * o_ref[...]


def _ffn(h2, x1, mod, layer, p, latent):
    n = x1.shape[0]
    tm, tf = TM_FFN, TF_FFN
    seq_len = DEC_SEQ if latent else SEQ
    assert n % tm == 0 and tm % HALO == 0 and D_FF % tf == 0
    hb = tm // HALO
    nhalo = n // HALO
    return pl.pallas_call(
        functools.partial(_ffn_kernel, seq_len=seq_len),
        grid=(n // tm, D_FF // tf),
        in_specs=[
            pl.BlockSpec((tm, D_MODEL), lambda i, j: (i, 0)),
            pl.BlockSpec((HALO, D_MODEL), lambda i, j: (jnp.maximum(i * hb - 1, 0), 0)),
            pl.BlockSpec((HALO, D_MODEL), lambda i, j: (jnp.minimum((i + 1) * hb, nhalo - 1), 0)),
            pl.BlockSpec((tm, D_MODEL), lambda i, j: (i, 0)),
            pl.BlockSpec((1, MOD_CHUNKS, D_MODEL), _mod_map(layer, tm, latent)),
            pl.BlockSpec((None, D_MODEL, tf), lambda i, j: (layer, 0, j)),
            pl.BlockSpec((None, D_MODEL, tf), lambda i, j: (layer, 0, j)),
            pl.BlockSpec((None, 8, tf), lambda i, j: (layer, 0, j)),
            pl.BlockSpec((None, tf, D_MODEL), lambda i, j: (layer, j, 0)),
        ],
        out_specs=pl.BlockSpec((tm, D_MODEL), lambda i, j: (i, 0)),
        out_shape=jax.ShapeDtypeStruct((n, D_MODEL), F32),
        scratch_shapes=[pltpu.VMEM((tm + 2 * HALO, D_MODEL), BF16)],
        compiler_params=_cp(("arbitrary", "arbitrary")),
        name="ffn_latent" if latent else "ffn_context",
    )(h2, h2, h2, x1, mod, p['w_gate'], p['w_up'], p['w_conv'], p['w_down'])


def _tile_row(g, reps, scale=1.0):
    return (jnp.tile(g, (1, reps)) * scale)[:, None, :]


def _prep(w):
    win_r = jnp.pad(w['w_in'], ((0, 0), (0, 0), (0, IN_COLS_P - IN_COLS))).astype(BF16)

    perm = np.array([j + 16 if j % 32 < 16 else j - 16 for j in range(MLA_ROPE)])
    half_pad = lambda t: jnp.concatenate([t, jnp.zeros(t.shape[:-1] + (LANES - MLA_ROPE,), F32)], axis=-1)
    wq = w['w_q_up'].reshape(DEPTH, MLA_Q_LORA, MLA_HEADS, MLA_QK)
    wq_r = jnp.concatenate([wq, jnp.zeros((DEPTH, MLA_Q_LORA, MLA_HEADS, 256 - MLA_QK), F32)], axis=-1)
    wq_p = half_pad(wq[..., MLA_NOPE:][..., perm])
    wq_r = jnp.concatenate([wq_r.reshape(DEPTH, MLA_Q_LORA, -1), wq_p.reshape(DEPTH, MLA_Q_LORA, -1)],
                           axis=-1).astype(BF16)
    wkv = w['w_kv_up'].reshape(DEPTH, MLA_KV_LORA, MLA_HEADS, 2 * MLA_NOPE)
    wkv_r = jnp.concatenate([wkv[..., :MLA_NOPE].reshape(DEPTH, MLA_KV_LORA, -1),
                             wkv[..., MLA_NOPE:].reshape(DEPTH, MLA_KV_LORA, -1)], axis=-1).astype(BF16)

    def pad_head(g, s):
        return _tile_row(jnp.concatenate([g, jnp.zeros((DEPTH, 256 - MLA_QK), F32)], axis=-1), MLA_HEADS, s)

    def partner_gain(g, s):
        return _tile_row(half_pad(g[:, MLA_NOPE:][:, perm]), MLA_HEADS, s)

    seg = np.kron(np.eye(LANES // HEAD_DIM), np.full((HEAD_DIM, HEAD_DIM), 1.0 / HEAD_DIM)).astype(np.float32)
    wc = jnp.concatenate([w['w_conv'], jnp.zeros((DEPTH, 5, D_FF), F32)], axis=1)
    return dict(
        g_mix=w['g_mix'][:, None, :], g_ffn=w['g_ffn'][:, None, :],
        w_in=win_r, w_q_up=wq_r, w_kv_up=wkv_r,
        gq_na=_tile_row(w['g_qn_na'], 8, HEAD_DIM ** -0.5 * LOG2E), gk_na=_tile_row(w['g_kn_na'], 8),
        g_q_lora=w['g_q_lora'][:, None, :], g_kv_lora=w['g_kv_lora'][:, None, :],
        gq_mla=pad_head(w['g_qn_mla'], MLA_QK ** -0.5 * LOG2E), gk_mla=pad_head(w['g_kn_mla'], 1.0),
        gqp_mla=partner_gain(w['g_qn_mla'], MLA_QK ** -0.5 * LOG2E), gkp_mla=partner_gain(w['g_kn_mla'], 1.0),
        seg=jnp.asarray(seg).astype(BF16),
        gq_win=_tile_row(w['g_qn_win'], 8, HEAD_DIM ** -0.5 * LOG2E), gk_win=_tile_row(w['g_kn_win'], 2),
        sink=w['sink_win'] * LOG2E,
        w_out=w['w_out'].astype(BF16), w_gate=w['w_gate'].astype(BF16), w_up=w['w_up'].astype(BF16),
        w_conv=wc, w_down=w['w_down'].astype(BF16),
    )


def _na_bias_rows(rpb):
    lead = rpb.shape[:-2]
    ext = jnp.concatenate([jnp.repeat(rpb[..., :1], 48, axis=-1), rpb, jnp.repeat(rpb[..., -1:], 49, axis=-1)], axis=-1)
    return jnp.concatenate([ext, jnp.full(lead + (1, LANES), NEG, F32)], axis=-2)


@functools.lru_cache(maxsize=None)
def _dft_tables():
    def cs(n):
        k = (np.arange(n)[:, None] * np.arange(n)[None, :]) % n
        ang = 2.0 * np.pi * k.astype(np.float64) / n
        return np.cos(ang), np.sin(ang)

    out = {}
    for name, n in (('ctx', SEQ), ('lat', DEC_SEQ)):
        c, s = cs(n)
        out['cs_' + name], out['ss_' + name] = c, s
    out['cc'], out['sc'] = cs(FN_CH)
    return out


@functools.lru_cache(maxsize=None)
def _rope_tables():
    t = np.arange(DEC_SEQ)
    quarter = MLA_ROPE // 4
    inv = ROPE_BASE ** (-np.arange(quarter, dtype=np.float64) / quarter)
    j = np.arange(MLA_ROPE)
    pos = np.where(j[None, :] < MLA_ROPE // 2, (t // GRID_W)[:, None], (t % GRID_W)[:, None]).astype(np.float64)
    ang = pos * inv[j % quarter][None, :]
    sign = np.where((j % 32) < 16, -1.0, 1.0)
    cos = np.cos(ang)
    sin = np.sin(ang) * sign[None, :]
    return np.tile(cos, (1, 2)).astype(np.float32), np.tile(sin, (1, 2)).astype(np.float32)


def kernel(x_prompt, x_sample, cache_na_k, cache_na_v, cache_mla_ckv, cache_mla_krope, cache_win_k, cache_win_v,
           c, c_ctx, w_mod, b_mod, g_mix, g_ffn, w_in, g_qn_na, g_kn_na, rpb_na, g_q_lora, w_q_up, g_kv_lora,
           w_kv_up, g_qn_mla, g_kn_mla, g_qn_win, g_kn_win, sink_win, w_out, w_gate, w_up, w_conv, w_down):
    w = dict(g_mix=g_mix, g_ffn=g_ffn, w_in=w_in, g_qn_na=g_qn_na, g_kn_na=g_kn_na, g_q_lora=g_q_lora,
             w_q_up=w_q_up, g_kv_lora=g_kv_lora, w_kv_up=w_kv_up, g_qn_mla=g_qn_mla, g_kn_mla=g_kn_mla,
             g_qn_win=g_qn_win, g_kn_win=g_kn_win, sink_win=sink_win, w_out=w_out, w_gate=w_gate, w_up=w_up,
             w_conv=w_conv, w_down=w_down)
    nb_ctx, nb_lat = x_prompt.shape[0], x_sample.shape[0]
    dft = {k: jnp.asarray(v, F32).astype(BF16) for k, v in _dft_tables().items()}
    rope_tabs = tuple(jnp.asarray(t) for t in _rope_tables())

    mod = _modulation(jnp.concatenate([c_ctx[None, :], c], axis=0), w_mod, b_mod)

    xp = x_prompt.reshape(nb_ctx * SEQ, D_MODEL)
    xs = x_sample.reshape(nb_lat * DEC_SEQ, D_MODEL)
    caches = []
    p = _prep(w)
    nctx = nb_lat * DEPTH * PAST_LEN
    dup = lambda t: jnp.concatenate([t[..., :64], t[..., :64], t[..., 64:], t[..., 64:]], axis=-1).astype(BF16)
    kna_c = cache_na_k.reshape(nb_lat, DEPTH, PAST_LEN, 512).astype(BF16)
    vna_c = cache_na_v.reshape(nb_lat, DEPTH, PAST_LEN, 512).astype(BF16)
    kw_c = dup(cache_win_k.reshape(nb_lat, DEPTH, PAST_LEN, 128))
    vw_c = dup(cache_win_v.reshape(nb_lat, DEPTH, PAST_LEN, 128))
    ckv_c = cache_mla_ckv.reshape(nctx, MLA_KV_LORA)
    kr_c = jnp.pad(cache_mla_krope.reshape(nctx, MLA_ROPE), ((0, 0), (0, LANES - MLA_ROPE)))
    bias_ext = _na_bias_rows(rpb_na * LOG2E)
    for l in range(DEPTH):

        pr = _project(xp, mod, l, p, False, None)
        caches.append(pr[10:])
        o4 = _ctx_attention(pr, p['sink'], dft, l)
        x1, h2 = _out_project(o4, xp, mod, l, p, False)
        xp = _ffn(h2, x1, mod, l, p, False)

        qna, kna, vna, qm, km, vm, qw, kw, vw, fv = _project(xs, mod, l, p, True, rope_tabs)
        km_c, vm_c = _mla_ctx(ckv_c, kr_c, p, l, nb_lat)
        o_a = _na_attention(qna, kna, vna, kna_c, vna_c, bias_ext, nb_lat, l)
        o_b = _mla_attention(qm, km, vm, km_c, vm_c, nb_lat)
        o_c = _win_attention(p['sink'], qw, kw, vw, kw_c, vw_c, nb_lat, l)
        o_d = _fourier(fv, dft, nb_lat)
        x1, h2 = _out_project((o_a, o_b, o_c, o_d), xs, mod, l, p, True)
        xs = _ffn(h2, x1, mod, l, p, True)

    def stack(idx, shape):
        return jnp.stack([caches[l][idx].reshape((nb_ctx, SEQ) + shape) for l in range(DEPTH)], axis=1)

    return (xp.reshape(nb_ctx, SEQ, D_MODEL), xs.reshape(nb_lat, DEC_SEQ, D_MODEL),
            stack(0, (8, HEAD_DIM)), stack(1, (8, HEAD_DIM)), stack(2, (MLA_KV_LORA,)), stack(3, (MLA_ROPE,)),
            stack(4, (2, HEAD_DIM)), stack(5, (2, HEAD_DIM)))
```

```python
import functools
import math

import numpy as np
import jax
import jax.numpy as jnp
from jax import lax
from jax.experimental import pallas as pl
from jax.experimental.pallas import tpu as pltpu

F32 = jnp.float32
BF16 = jnp.bfloat16

D_MODEL = 2048
DEPTH = 2
SEQ = 256
DEC_SEQ = 2048
PAST_LEN = 512
GRID_W = 64
HEAD_DIM = 64
GROUP_W = 512
MLA_HEADS = 4
MLA_NOPE = 128
MLA_ROPE = 64
MLA_QK = MLA_NOPE + MLA_ROPE
MLA_Q_LORA = 384
MLA_KV_LORA = 128
WINDOW = 128
FN_GROUPS = 4
FN_CH = 128
D_FF = 5632
MOD_CHUNKS = 6
ROPE_BASE = 10000.0
EPS = 1e-6
NEG = -1e30
LOG2E = math.log2(math.e)

LANES = 128
MOD_ROWS = 8
VMEM_LIMIT = 56 * 1024 * 1024

C_AQ, C_AK, C_AV = 0, 512, 1024
C_MQ = 1536
C_CKV = 1920
C_KR = 2048
IN_COLS = 3392
IN_COLS_P = 3456

TM_PROJ = 512
TM_OUT = 512
TM_FFN = 512
TF_FFN = 512
HALO = 16
TQ = 256
TQ_MLA = 512
TQ_FN = 1024
CTX_PER_STEP = 2
NA_KH = 8
NA_KW = 16
NA_KROWS = 12
NA_MASKED = 15


def _cp(sem):
    return pltpu.CompilerParams(dimension_semantics=sem, vmem_limit_bytes=VMEM_LIMIT)


def _const_spec(shape):
    n = len(shape)
    return pl.BlockSpec(shape, lambda *_: (0,) * n)


def _layer_spec(shape, layer, resident=False):
    n = len(shape)
    mode = dict(pipeline_mode=pl.Buffered(1)) if resident else {}
    return pl.BlockSpec((None,) + tuple(shape), lambda *_: (layer,) + (0,) * n, **mode)


def _dot(a, b):
    return jnp.dot(a, b, preferred_element_type=F32)


def _dot_nt(a, b):
    return lax.dot_general(a, b, (((1,), (1,)), ((), ())), preferred_element_type=F32)


def _rms(t, n):
    return t * lax.rsqrt(jnp.sum(t * t, axis=-1, keepdims=True) * (1.0 / n) + EPS)


def _seg_rms64(t, seg_ref):
    outs = []
    for c in range(t.shape[1] // LANES):
        tc = t[:, c * LANES:(c + 1) * LANES]
        ms = _dot((tc * tc).astype(BF16), seg_ref[...])
        outs.append(tc * lax.rsqrt(ms + EPS))
    return jnp.concatenate(outs, axis=-1)


def _partner128(t):
    first = (lax.broadcasted_iota(jnp.int32, t.shape, 1) % 32) < 16
    return jnp.where(first, pltpu.roll(t, LANES - 16, 1), pltpu.roll(t, 16, 1))


def _rope128(t, cos, sin_signed):
    return t * cos + _partner128(t) * sin_signed


def _rope_wide(t, cos, sin_signed):
    return jnp.concatenate(
        [_rope128(t[:, c * LANES:(c + 1) * LANES], cos, sin_signed) for c in range(t.shape[1] // LANES)], axis=-1)


MOD_TK = 256
MOD_LHS_ROWS = 16


def _split_bf16(t):
    hi = t.astype(BF16)
    return hi, (t - hi.astype(F32)).astype(BF16)


def _mod_kernel(c_ref, w_ref, b_ref, o_ref):
    @pl.when(pl.program_id(1) == 0)
    def _():
        o_ref[...] = jnp.broadcast_to(b_ref[...], o_ref.shape)

    c = c_ref[...]
    x_hi, x_lo = _split_bf16(c / (1.0 + jnp.exp(-c)))
    w_hi, w_lo = _split_bf16(w_ref[...])
    acc = _dot(x_hi, w_hi) + _dot(x_lo, w_hi) + _dot(x_hi, w_lo)
    o_ref[...] += acc[:MOD_ROWS]


def _modulation(cvecs, w_mod, b_mod):
    nvec = cvecs.shape[0]
    c = jnp.zeros((MOD_LHS_ROWS, D_MODEL), F32).at[:nvec].set(cvecs)
    ncol = w_mod.shape[2]
    out = pl.pallas_call(
        _mod_kernel,
        grid=(DEPTH, D_MODEL // MOD_TK),
        in_specs=[
            pl.BlockSpec((MOD_LHS_ROWS, MOD_TK), lambda l, k: (0, k)),
            pl.BlockSpec((None, MOD_TK, ncol), lambda l, k: (l, k, 0)),
            pl.BlockSpec((None, 1, ncol), lambda l, k: (l, 0, 0)),
        ],
        out_specs=pl.BlockSpec((None, MOD_ROWS, ncol), lambda l, k: (l, 0, 0)),
        out_shape=jax.ShapeDtypeStruct((DEPTH, MOD_ROWS, ncol), F32),
        compiler_params=_cp(("arbitrary", "arbitrary")),
        name="modulation",
    )(c, w_mod, b_mod.reshape(DEPTH, 1, ncol))
    return out.reshape(DEPTH * MOD_ROWS, MOD_CHUNKS, D_MODEL)


def _mla_kv_heads(ckv_n, kr, wkv_ref, gk_ref, gkp_ref, rope):
    kv = _dot(ckv_n.astype(BF16), wkv_ref[...])
    ss_r = jnp.sum(kr * kr, axis=-1, keepdims=True)
    pk = _partner128(kr) if rope is not None else None
    ks = []
    for h in range(MLA_HEADS):
        kn = kv[:, h * MLA_NOPE:(h + 1) * MLA_NOPE]
        ms = (jnp.sum(kn * kn, axis=-1, keepdims=True) + ss_r) * (1.0 / MLA_QK)
        r = lax.rsqrt(ms + EPS)
        tail = kr * r * gk_ref[:, h * 256 + 128:(h + 1) * 256]
        if rope is not None:
            tail = tail * rope[0] + (pk * r * gkp_ref[:, h * LANES:(h + 1) * LANES]) * rope[1]
        ks.append(kn * r * gk_ref[:, h * 256:h * 256 + 128])
        ks.append(tail)
    return jnp.concatenate(ks, axis=-1), kv[:, MLA_HEADS * MLA_NOPE:]


def _proj_kernel(*refs, latent):
    (x_ref, mod_ref, gmix_ref, win_ref, wq_ref, wkv_ref, gqna_ref, gkna_ref, gql_ref, gkvl_ref,
     gqm_ref, gkm_ref, gqw_ref, gkw_ref, gqp_ref, gkp_ref, seg_ref) = refs[:17]
    pos = 17
    rope = None
    if latent:
        rope = (refs[17][...], refs[18][...])
        pos = 19
    (qna_o, kna_o, vna_o, qm_o, km_o, vm_o, qw_o, kw_o, vw_o, fv_o) = refs[pos:pos + 10]
    cache_o = refs[pos + 10:]

    x = x_ref[...]
    rinv = lax.rsqrt(jnp.mean(x * x, axis=-1, keepdims=True) + EPS)
    h = (x * rinv) * (gmix_ref[...] * (1.0 + mod_ref[0, 1:2, :])) + mod_ref[0, 0:1, :]
    hb = h.astype(BF16)

    projected = _dot(hb, win_ref[...])

    def proj(c0, c1):
        return projected[:, c0:c1]

    qna_o[...] = (_seg_rms64(proj(C_AQ, C_AK), seg_ref) * gqna_ref[...]).astype(BF16)
    k_na = _seg_rms64(proj(C_AK, C_AV), seg_ref) * gkna_ref[...]
    kna_o[...] = k_na.astype(BF16)
    v_na = proj(C_AV, C_MQ)
    vna_o[...] = v_na.astype(BF16)

    cq = (_rms(proj(C_MQ, C_CKV), MLA_Q_LORA) * gql_ref[...]).astype(BF16)
    qu = _dot(cq, wq_ref[...] if latent else wq_ref[:, :MLA_HEADS * 256])
    qs = []
    for hd in range(MLA_HEADS):
        blk = qu[:, hd * 256:(hd + 1) * 256]
        rinv = lax.rsqrt(jnp.sum(blk * blk, axis=-1, keepdims=True) * (1.0 / MLA_QK) + EPS)
        blk = blk * rinv * gqm_ref[:, hd * 256:(hd + 1) * 256]
        if latent:
            ps = slice(MLA_HEADS * 256 + hd * LANES, MLA_HEADS * 256 + (hd + 1) * LANES)
            partner = qu[:, ps] * rinv * gqp_ref[:, hd * LANES:(hd + 1) * LANES]
            blk = jnp.concatenate([blk[:, :LANES], blk[:, LANES:] * rope[0] + partner * rope[1]], axis=-1)
        qs.append(blk)
    qm_o[...] = jnp.concatenate(qs, axis=-1).astype(BF16)
    ckv_n = _rms(proj(C_CKV, C_KR), MLA_KV_LORA) * gkvl_ref[...]

    tail = proj(C_KR, IN_COLS_P)
    lo = lax.broadcasted_iota(jnp.int32, (x.shape[0], LANES), 1) < HEAD_DIM
    nslab = (IN_COLS_P - C_KR) // LANES
    slabs = [tail[:, c * LANES:(c + 1) * LANES] for c in range(nslab)]
    swapped = [pltpu.roll(t, HEAD_DIM, 1) for t in slabs]
    al = [jnp.where(lo, swapped[c], swapped[c + 1]) for c in range(nslab - 1)]
    kr = jnp.where(lo, slabs[0], 0.0)
    k_m, v_m = _mla_kv_heads(ckv_n, kr, wkv_ref, gkm_ref, gkp_ref, rope)
    km_o[...] = k_m.astype(BF16)
    vm_o[...] = v_m.astype(BF16)

    def dup(t):
        sw = pltpu.roll(t, HEAD_DIM, 1)
        return jnp.concatenate([jnp.where(lo, t, sw), jnp.where(lo, sw, t)], axis=-1)

    q_w = _seg_rms64(jnp.concatenate(al[0:4], axis=-1), seg_ref) * gqw_ref[...]
    k_w = _seg_rms64(al[4], seg_ref) * gkw_ref[...]
    v_w = al[5]
    if latent:
        q_w = _rope_wide(q_w, *rope)
        k_w = _rope128(k_w, *rope)
    qw_o[...] = q_w.astype(BF16)
    kw_o[...] = dup(k_w).astype(BF16)
    vw_o[...] = dup(v_w).astype(BF16)

    fv_o[...] = jnp.concatenate(al[6:10], axis=-1).astype(BF16)

    if not latent:
        kna32_o, vna32_o, ckv32_o, kr32_o, kw32_o, vw32_o = cache_o
        kna32_o[...] = k_na
        vna32_o[...] = v_na
        ckv32_o[...] = ckv_n
        kr32_o[...] = kr[:, :MLA_ROPE]
        kw32_o[...] = k_w
        vw32_o[...] = v_w


def _project(x, mod, layer, p, latent, rope_tabs):
    n = x.shape[0]
    tm = TM_PROJ
    row = lambda i: (i, 0)
    if latent:
        mod_map = lambda i: (layer * MOD_ROWS + 1 + (i * tm) // DEC_SEQ, 0, 0)
    else:
        mod_map = lambda i: (layer * MOD_ROWS, 0, 0)
    in_specs = [
        pl.BlockSpec((tm, D_MODEL), row),
        pl.BlockSpec((1, MOD_CHUNKS, D_MODEL), mod_map),
        _layer_spec((1, D_MODEL), layer),
        _layer_spec((D_MODEL, IN_COLS_P), layer, resident=True),
        _layer_spec((MLA_Q_LORA, MLA_HEADS * (256 + LANES)), layer, resident=True),
        _layer_spec((MLA_KV_LORA, 1024), layer, resident=True),
    ] + [_layer_spec((1, wd), layer) for wd in (512, 512, MLA_Q_LORA, MLA_KV_LORA, 1024, 1024, 512, 128, 512, 512)]
    in_specs.append(_const_spec((LANES, LANES)))
    args = [x, mod, p['g_mix'], p['w_in'], p['w_q_up'], p['w_kv_up'], p['gq_na'], p['gk_na'], p['g_q_lora'],
            p['g_kv_lora'], p['gq_mla'], p['gk_mla'], p['gq_win'], p['gk_win'], p['gqp_mla'], p['gkp_mla'],
            p['seg']]
    if latent:
        nt = DEC_SEQ // tm
        in_specs += [pl.BlockSpec((tm, LANES), lambda i: (i % nt, 0))] * 2
        args += list(rope_tabs)
    widths = [512, 512, 512, 1024, 1024, 512, 512, 256, 256, 512]
    out_specs = [pl.BlockSpec((tm, w), row) for w in widths]
    out_shape = [jax.ShapeDtypeStruct((n, w), BF16) for w in widths]
    if not latent:
        cache_w = [512, 512, MLA_KV_LORA, MLA_ROPE, 128, 128]
        out_specs += [pl.BlockSpec((tm, w), row) for w in cache_w]
        out_shape += [jax.ShapeDtypeStruct((n, w), F32) for w in cache_w]
    return pl.pallas_call(
        functools.partial(_proj_kernel, latent=latent),
        grid=(n // tm,),
        in_specs=in_specs,
        out_specs=out_specs,
        out_shape=out_shape,
        compiler_params=_cp(("arbitrary",)),
        name="proj_latent" if latent else "proj_context",
    )(*args)


def _mla_ctx_kernel(ckv_ref, kr_ref, wkv_ref, gk_ref, k_o, v_o):
    k_m, v_m = _mla_kv_heads(ckv_ref[...], kr_ref[...], wkv_ref, gk_ref, None, None)
    k_o[...] = k_m.astype(BF16)
    v_o[...] = v_m.astype(BF16)


def _mla_ctx(ckv, kr128, p, layer, nb):
    n = nb * PAST_LEN
    src = lambda b: (b * DEPTH + layer, 0)
    row = lambda b: (b, 0)
    return pl.pallas_call(
        _mla_ctx_kernel,
        grid=(nb,),
        in_specs=[pl.BlockSpec((PAST_LEN, MLA_KV_LORA), src), pl.BlockSpec((PAST_LEN, LANES), src),
                  _layer_spec((MLA_KV_LORA, 1024), layer), _layer_spec((1, 1024), layer)],
        out_specs=[pl.BlockSpec((PAST_LEN, 1024), row), pl.BlockSpec((PAST_LEN, 512), row)],
        out_shape=[jax.ShapeDtypeStruct((n, 1024), BF16), jax.ShapeDtypeStruct((n, 512), BF16)],
        compiler_params=_cp(("arbitrary",)),
        name="mla_cached_kv",
    )(ckv, kr128, p['w_kv_up'], p['gk_mla'])


def _softmax_pv(scores, values, sink=None):
    m = functools.reduce(jnp.maximum, [jnp.max(s, axis=-1, keepdims=True) for s in scores])
    if sink is not None:
        m = jnp.maximum(m, sink)
    ps = [jnp.exp2(s - m) for s in scores]
    l = functools.reduce(lambda a, b: a + b, [jnp.sum(p, axis=-1, keepdims=True) for p in ps])
    if sink is not None:
        l = l + jnp.exp2(sink - m)
    o = functools.reduce(lambda a, b: a + b, [_dot(p.astype(BF16), v) for p, v in zip(ps, values)])
    return o / l


def _half_masks():
    lane = lax.broadcasted_iota(jnp.int32, (1, LANES), 1)
    lo = lane < HEAD_DIM
    return jnp.where(lo, 1.0, 0.0).astype(BF16), jnp.where(lo, 0.0, 1.0).astype(BF16)


def _pair_attention(q2, keys, values, extra=None, sinks=None):
    rows = q2.shape[0]
    lo = lax.broadcasted_iota(jnp.int32, (rows, LANES), 1) < HEAD_DIM
    outs = []
    for e, msk in enumerate(_half_masks()):
        qm = q2 * msk
        scores = []
        for idx, k2 in enumerate(keys):
            s = _dot_nt(qm, k2)
            if extra is not None:
                s = extra(e, idx, s)
            scores.append(s)
        sink = None
        if sinks is not None:
            sink = jnp.full((rows, 1), sinks[e], F32)
        outs.append(_softmax_pv(scores, values, sink))
    return jnp.where(lo, outs[0], outs[1])


def _dft_real(x, cc_ref, sc_ref, cs, ss, scale):
    outs = []
    for g in range(FN_GROUPS):
        xg = x[:, g * FN_CH:(g + 1) * FN_CH]
        xc = _dot(xg, cc_ref[...]).astype(BF16)
        xs = _dot(xg, sc_ref[...]).astype(BF16)
        outs.append(_dot(cs, xc) - _dot(ss, xs))
    return jnp.concatenate(outs, axis=-1) * scale


def _ctx_attn_kernel(sink_ref, qna, kna, vna, qm, km, vm, qw, kw, vw, fv, cs_ref, ss_ref, cc_ref, sc_ref,
                     oa, ob, oc, od, *, layer):
    for r in range(qna.shape[0] // SEQ):
        rs = slice(r * SEQ, (r + 1) * SEQ)
        for hp in range(4):
            cs = slice(hp * LANES, (hp + 1) * LANES)
            oa[rs, cs] = _pair_attention(qna[rs, cs], [kna[rs, cs]], [vna[rs, cs]]).astype(BF16)
        for hd in range(MLA_HEADS):
            s = _dot_nt(qm[rs, hd * 256:(hd + 1) * 256], km[rs, hd * 256:(hd + 1) * 256])
            ob[rs, hd * 128:(hd + 1) * 128] = _softmax_pv([s], [vm[rs, hd * 128:(hd + 1) * 128]]).astype(BF16)
        for hp in range(4):
            cs = slice(hp * LANES, (hp + 1) * LANES)
            kcs = slice((hp // 2) * LANES, (hp // 2 + 1) * LANES)
            oc[rs, cs] = _pair_attention(qw[rs, cs], [kw[rs, kcs]], [vw[rs, kcs]],
                                         sinks=(sink_ref[layer, 2 * hp], sink_ref[layer, 2 * hp + 1])).astype(BF16)
        od[rs, :] = _dft_real(fv[rs, :], cc_ref, sc_ref, cs_ref[...], ss_ref[...],
                              1.0 / math.sqrt(SEQ * FN_CH)).astype(BF16)


def _ctx_attention(pr, sink, dft, layer):
    qna, kna, vna, qm, km, vm, qw, kw, vw, fv = pr[:10]
    n = qna.shape[0]
    row = lambda b: (b, 0)
    widths = [512, 512, 512, 1024, 1024, 512, 512, 256, 256, 512]
    in_specs = [pl.BlockSpec(memory_space=pltpu.SMEM)]
    in_specs += [pl.BlockSpec((CTX_PER_STEP * SEQ, w), row) for w in widths]
    in_specs += [_const_spec((SEQ, SEQ)), _const_spec((SEQ, SEQ)), _const_spec((FN_CH, FN_CH)),
                 _const_spec((FN_CH, FN_CH))]
    return pl.pallas_call(
        functools.partial(_ctx_attn_kernel, layer=layer),
        grid=(n // (CTX_PER_STEP * SEQ),),
        in_specs=in_specs,
        out_specs=[pl.BlockSpec((CTX_PER_STEP * SEQ, 512), row)] * 4,
        out_shape=[jax.ShapeDtypeStruct((n, 512), BF16)] * 4,
        compiler_params=_cp(("arbitrary",)),
        name="context_attention",
    )(sink, qna, kna, vna, qm, km, vm, qw, kw, vw, fv, dft['cs_ctx'], dft['ss_ctx'], dft['cc'], dft['sc'])


def _na_kernel(q_ref, k_ref, v_ref, kc_ref, vc_ref, ext_ref, o_ref, tl_ref, tr_ref):
    qt = pl.program_id(1)

    @pl.when((pl.program_id(0) == 0) & (qt == 0))
    def _():
        col = lax.broadcasted_iota(jnp.int32, (GRID_W, LANES), 0)
        lane = lax.broadcasted_iota(jnp.int32, (GRID_W, LANES), 1)
        kcol = lane % GRID_W
        first = jnp.clip(col - NA_KW // 2, 0, GRID_W - NA_KW)
        in_win = (kcol >= first) & (kcol < first + NA_KW)
        left = lane < GRID_W
        for h in range(tl_ref.shape[0]):
            for d in range(tl_ref.shape[1]):
                row = jnp.broadcast_to(ext_ref[h, d:d + 1, :], (GRID_W, LANES))
                blk_l = pltpu.roll(row, GRID_W + 1, 1, stride=1, stride_axis=0)
                blk_r = pltpu.roll(row, 1, 1, stride=1, stride_axis=0)
                tl_ref[h, d] = jnp.where(left, jnp.where(in_win, blk_l, NEG), 0.0)
                tr_ref[h, d] = jnp.where(left, 0.0, jnp.where(in_win, blk_r, NEG))

    rows = DEC_SEQ // GRID_W
    qrows = TQ // GRID_W
    kstart = jnp.clip(qt * qrows - NA_KH // 2, 0, rows - NA_KROWS)
    ks = pl.multiple_of(kstart * GRID_W, GRID_W)
    nk = NA_KROWS * GRID_W

    blk_idx = []
    for i in range(qrows):
        r = qt * qrows + i
        rs = jnp.clip(r - NA_KH // 2, 0, rows - NA_KH)
        blk_idx.append([jnp.where((kstart + j >= rs) & (kstart + j < rs + NA_KH), kstart + j - r + NA_KH - 1,
                                  NA_MASKED) for j in range(NA_KROWS)])

    def bias(h):
        return jnp.concatenate([
            jnp.concatenate([tl_ref[h, blk_idx[i][2 * jp]] + tr_ref[h, blk_idx[i][2 * jp + 1]]
                             for jp in range(NA_KROWS // 2)], axis=-1)
            for i in range(qrows)], axis=0)

    for hp in range(4):
        cs = slice(hp * LANES, (hp + 1) * LANES)

        def extra(e, idx, s, hp=hp):
            return s + bias(2 * hp + e) if idx == 0 else s

        o = _pair_attention(q_ref[:, cs], [k_ref[pl.ds(ks, nk), cs], kc_ref[:, cs]],
                            [v_ref[pl.ds(ks, nk), cs], vc_ref[:, cs]], extra=extra)
        o_ref[:, cs] = o.astype(BF16)


def _lat_specs(width_q, width_k, width_v, ctx_map, tq=TQ):
    nq = DEC_SEQ // tq
    ctx_nd = len(ctx_map(0, 0))
    lead = (None,) * (ctx_nd - 2)
    return [
        pl.BlockSpec((tq, width_q), lambda b, t: (b * nq + t, 0)),
        pl.BlockSpec((DEC_SEQ, width_k), lambda b, t: (b, 0)),
        pl.BlockSpec((DEC_SEQ, width_v), lambda b, t: (b, 0)),
        pl.BlockSpec(lead + (PAST_LEN, width_k), ctx_map),
        pl.BlockSpec(lead + (PAST_LEN, width_v), ctx_map),
    ]


def _lat_call(kernel, name, args, in_specs, nb, tq=TQ, scratch=()):
    nq = DEC_SEQ // tq
    return pl.pallas_call(
        kernel,
        grid=(nb, nq),
        in_specs=in_specs,
        out_specs=pl.BlockSpec((tq, 512), lambda b, t: (b * nq + t, 0)),
        out_shape=jax.ShapeDtypeStruct((nb * DEC_SEQ, 512), BF16),
        scratch_shapes=list(scratch),
        compiler_params=_cp(("arbitrary", "arbitrary")),
        name=name,
    )(*args)


def _na_attention(q, k, v, kc, vc, bias_ext, nb, layer):
    specs = _lat_specs(512, 512, 512, lambda b, t: (b, layer, 0, 0))
    specs.append(_layer_spec(bias_ext.shape[1:], layer, resident=True))
    tab = pltpu.VMEM(bias_ext.shape[1:3] + (GRID_W, LANES), F32)
    return _lat_call(_na_kernel, "neighbourhood_attention", (q, k, v, kc, vc, bias_ext), specs, nb,
                     scratch=(tab, tab))


def _mla_kernel(q_ref, k_ref, v_ref, kc_ref, vc_ref, o_ref):
    for hd in range(MLA_HEADS):
        ks = slice(hd * 256, (hd + 1) * 256)
        vs = slice(hd * 128, (hd + 1) * 128)
        q = q_ref[:, ks]
        scores = [_dot_nt(q, k_ref[:, ks]), _dot_nt(q, kc_ref[:, ks])]
        o_ref[:, vs] = _softmax_pv(scores, [v_ref[:, vs], vc_ref[:, vs]]).astype(BF16)


def _mla_attention(q, k, v, kc, vc, nb):
    specs = _lat_specs(1024, 1024, 512, lambda b, t: (b, 0), tq=TQ_MLA)
    return _lat_call(_mla_kernel, "latent_attention", (q, k, v, kc, vc), specs, nb, tq=TQ_MLA)


def _win_kernel(sink_ref, q_ref, k_ref, v_ref, kc_ref, vc_ref, o_ref, *, layer):
    qt = pl.program_id(1)
    nk = 4 * WINDOW
    ks = pl.multiple_of(jnp.clip(qt * TQ - WINDOW, 0, DEC_SEQ - nk), WINDOW)
    qpos = qt * TQ + lax.broadcasted_iota(jnp.int32, (TQ, nk), 0)
    kpos = ks + lax.broadcasted_iota(jnp.int32, (TQ, nk), 1)
    band = jnp.where(jnp.abs(kpos - qpos) <= WINDOW, 0.0, NEG)

    def extra(e, idx, s):
        return s + band if idx == 0 else s

    for hp in range(4):
        cs = slice(hp * LANES, (hp + 1) * LANES)
        kcs = slice((hp // 2) * LANES, (hp // 2 + 1) * LANES)
        o = _pair_attention(q_ref[:, cs], [k_ref[pl.ds(ks, nk), kcs], kc_ref[:, kcs]],
                            [v_ref[pl.ds(ks, nk), kcs], vc_ref[:, kcs]], extra=extra,
                            sinks=(sink_ref[layer, 2 * hp], sink_ref[layer, 2 * hp + 1]))
        o_ref[:, cs] = o.astype(BF16)


def _win_attention(sink, q, k, v, kc, vc, nb, layer):
    specs = [pl.BlockSpec(memory_space=pltpu.SMEM)]
    specs += _lat_specs(512, 256, 256, lambda b, t: (b, layer, 0, 0))
    return _lat_call(functools.partial(_win_kernel, layer=layer), "window_attention", (sink, q, k, v, kc, vc),
                     specs, nb)


def _fourier_kernel(x_ref, cc_ref, sc_ref, cs_ref, ss_ref, o_ref, xc_ref, xs_ref):
    @pl.when(pl.program_id(1) == 0)
    def _():
        for g in range(FN_GROUPS):
            gs = slice(g * FN_CH, (g + 1) * FN_CH)
            xg = x_ref[:, gs]
            xc_ref[:, gs] = _dot(xg, cc_ref[...]).astype(BF16)
            xs_ref[:, gs] = _dot(xg, sc_ref[...]).astype(BF16)

    y = _dot(cs_ref[...], xc_ref[...]) - _dot(ss_ref[...], xs_ref[...])
    o_ref[...] = (y * (1.0 / math.sqrt(DEC_SEQ * FN_CH))).astype(BF16)


def _fourier(x, dft, nb):
    nq = DEC_SEQ // TQ_FN
    return pl.pallas_call(
        _fourier_kernel,
        grid=(nb, nq),
        in_specs=[
            pl.BlockSpec((DEC_SEQ, 512), lambda b, t: (b, 0)),
            _const_spec((FN_CH, FN_CH)), _const_spec((FN_CH, FN_CH)),
            pl.BlockSpec((TQ_FN, DEC_SEQ), lambda b, t: (t, 0)),
            pl.BlockSpec((TQ_FN, DEC_SEQ), lambda b, t: (t, 0)),
        ],
        out_specs=pl.BlockSpec((TQ_FN, 512), lambda b, t: (b * nq + t, 0)),
        out_shape=jax.ShapeDtypeStruct((nb * DEC_SEQ, 512), BF16),
        scratch_shapes=[pltpu.VMEM((DEC_SEQ, 512), BF16), pltpu.VMEM((DEC_SEQ, 512), BF16)],
        compiler_params=_cp(("arbitrary", "arbitrary")),
        name="fourier_mix",
    )(x, dft['cc'], dft['sc'], dft['cs_lat'], dft['ss_lat'])


def _out_kernel(oa, ob, oc, od, x_ref, mod_ref, gffn_ref, w_ref, x1_o, h2_o):
    o = jnp.concatenate([oa[...], ob[...], oc[...], od[...]], axis=-1)
    acc = _dot(o, w_ref[...])
    x1 = x_ref[...] + mod_ref[0, 2:3, :] * acc
    x1_o[...] = x1
    rinv = lax.rsqrt(jnp.mean(x1 * x1, axis=-1, keepdims=True) + EPS)
    h2 = (x1 * rinv) * (gffn_ref[...] * (1.0 + mod_ref[0, 4:5, :])) + mod_ref[0, 3:4, :]
    h2_o[...] = h2.astype(BF16)


def _mod_map(layer, tm, latent):
    if latent:
        return lambda i, *_: (layer * MOD_ROWS + 1 + (i * tm) // DEC_SEQ, 0, 0)
    return lambda i, *_: (layer * MOD_ROWS, 0, 0)


def _out_project(o4, x, mod, layer, p, latent):
    n = x.shape[0]
    tm = TM_OUT
    row = lambda i: (i, 0)
    return pl.pallas_call(
        _out_kernel,
        grid=(n // tm,),
        in_specs=[pl.BlockSpec((tm, 512), row)] * 4 + [
            pl.BlockSpec((tm, D_MODEL), row),
            pl.BlockSpec((1, MOD_CHUNKS, D_MODEL), _mod_map(layer, tm, latent)),
            _layer_spec((1, D_MODEL), layer),
            _layer_spec((D_MODEL, D_MODEL), layer, resident=True),
        ],
        out_specs=[pl.BlockSpec((tm, D_MODEL), row), pl.BlockSpec((tm, D_MODEL), row)],
        out_shape=[jax.ShapeDtypeStruct((n, D_MODEL), F32), jax.ShapeDtypeStruct((n, D_MODEL), BF16)],
        compiler_params=_cp(("arbitrary",)),
        name="out_proj_latent" if latent else "out_proj_context",
    )(*o4, x, mod, p['g_ffn'], p['w_out'])


def _ffn_kernel(h_ref, hp_ref, hn_ref, x1_ref, mod_ref, wg_ref, wu_ref, wc_ref, wd_ref, o_ref, hext_ref, *, seq_len):
    i = pl.program_id(0)
    j = pl.program_id(1)
    tm = h_ref.shape[0]

    @pl.when(j == 0)
    def _():
        hext_ref[0:HALO, :] = hp_ref[...]
        hext_ref[HALO:HALO + tm, :] = h_ref[...]
        hext_ref[HALO + tm:, :] = hn_ref[...]
        o_ref[...] = jnp.zeros_like(o_ref)

    g_ext = _dot(hext_ref[...], wg_ref[...])
    rows_ext = g_ext.shape[0]
    pos = (i * tm + lax.broadcasted_iota(jnp.int32, (tm, 1), 0)) % seq_len
    g_mid = g_ext[HALO:HALO + tm]
    g_prev = pltpu.roll(g_ext, 1, 0)[HALO:HALO + tm]
    g_next = pltpu.roll(g_ext, rows_ext - 1, 0)[HALO:HALO + tm]
    g_prev = jnp.where(pos == 0, 0.0, g_prev)
    g_next = jnp.where(pos == seq_len - 1, 0.0, g_next)
    g = g_prev * wc_ref[0:1, :] + g_mid * wc_ref[1:2, :] + g_next * wc_ref[2:3, :]
    half = tm // 2
    for r in range(2):
        rs = slice(r * half, (r + 1) * half)
        gr = g[rs]
        u = _dot(h_ref[rs, :], wu_ref[...])
        act = (gr / (1.0 + jnp.exp(-gr))) * u
        o_ref[rs, :] += _dot(act.astype(BF16), wd_ref[...])

    @pl.when(j == pl.num_programs(1) - 1)
    def _():
        o_ref[...] = x1_ref[...] + mod_ref[0, 5:6, :] * o_ref[...]


def _ffn(h2, x1, mod, layer, p, latent):
    n = x1.shape[0]
    tm, tf = TM_FFN, TF_FFN
    seq_len = DEC_SEQ if latent else SEQ
    assert n % tm == 0 and tm % HALO == 0 and D_FF % tf == 0
    hb = tm // HALO
    nhalo = n // HALO
    return pl.pallas_call(
        functools.partial(_ffn_kernel, seq_len=seq_len),
        grid=(n // tm, D_FF // tf),
        in_specs=[
            pl.BlockSpec((tm, D_MODEL), lambda i, j: (i, 0)),
            pl.BlockSpec((HALO, D_MODEL), lambda i, j: (jnp.maximum(i * hb - 1, 0), 0)),
            pl.BlockSpec((HALO, D_MODEL), lambda i, j: (jnp.minimum((i + 1) * hb, nhalo - 1), 0)),
            pl.BlockSpec((tm, D_MODEL), lambda i, j: (i, 0)),
            pl.BlockSpec((1, MOD_CHUNKS, D_MODEL), _mod_map(layer, tm, latent)),
            pl.BlockSpec((None, D_MODEL, tf), lambda i, j: (layer, 0, j)),
            pl.BlockSpec((None, D_MODEL, tf), lambda i, j: (layer, 0, j)),
            pl.BlockSpec((None, 8, tf), lambda i, j: (layer, 0, j)),
            pl.BlockSpec((None, tf, D_MODEL), lambda i, j: (layer, j, 0)),
        ],
        out_specs=pl.BlockSpec((tm, D_MODEL), lambda i, j: (i, 0)),
        out_shape=jax.ShapeDtypeStruct((n, D_MODEL), F32),
        scratch_shapes=[pltpu.VMEM((tm + 2 * HALO, D_MODEL), BF16)],
        compiler_params=_cp(("arbitrary", "arbitrary")),
        name="ffn_latent" if latent else "ffn_context",
    )(h2, h2, h2, x1, mod, p['w_gate'], p['w_up'], p['w_conv'], p['w_down'])


def _tile_row(g, reps, scale=1.0):
    return (jnp.tile(g, (1, reps)) * scale)[:, None, :]


def _prep(w):
    win_r = jnp.pad(w['w_in'], ((0, 0), (0, 0), (0, IN_COLS_P - IN_COLS))).astype(BF16)

    perm = np.array([j + 16 if j % 32 < 16 else j - 16 for j in range(MLA_ROPE)])
    half_pad = lambda t: jnp.concatenate([t, jnp.zeros(t.shape[:-1] + (LANES - MLA_ROPE,), F32)], axis=-1)
    wq = w['w_q_up'].reshape(DEPTH, MLA_Q_LORA, MLA_HEADS, MLA_QK)
    wq_r = jnp.concatenate([wq, jnp.zeros((DEPTH, MLA_Q_LORA, MLA_HEADS, 256 - MLA_QK), F32)], axis=-1)
    wq_p = half_pad(wq[..., MLA_NOPE:][..., perm])
    wq_r = jnp.concatenate([wq_r.reshape(DEPTH, MLA_Q_LORA, -1), wq_p.reshape(DEPTH, MLA_Q_LORA, -1)],
                           axis=-1).astype(BF16)
    wkv = w['w_kv_up'].reshape(DEPTH, MLA_KV_LORA, MLA_HEADS, 2 * MLA_NOPE)
    wkv_r = jnp.concatenate([wkv[..., :MLA_NOPE].reshape(DEPTH, MLA_KV_LORA, -1),
                             wkv[..., MLA_NOPE:].reshape(DEPTH, MLA_KV_LORA, -1)], axis=-1).astype(BF16)

    def pad_head(g, s):
        return _tile_row(jnp.concatenate([g, jnp.zeros((DEPTH, 256 - MLA_QK), F32)], axis=-1), MLA_HEADS, s)

    def partner_gain(g, s):
        return _tile_row(half_pad(g[:, MLA_NOPE:][:, perm]), MLA_HEADS, s)

    seg = np.kron(np.eye(LANES // HEAD_DIM), np.full((HEAD_DIM, HEAD_DIM), 1.0 / HEAD_DIM)).astype(np.float32)
    wc = jnp.concatenate([w['w_conv'], jnp.zeros((DEPTH, 5, D_FF), F32)], axis=1)
    return dict(
        g_mix=w['g_mix'][:, None, :], g_ffn=w['g_ffn'][:, None, :],
        w_in=win_r, w_q_up=wq_r, w_kv_up=wkv_r,
        gq_na=_tile_row(w['g_qn_na'], 8, HEAD_DIM ** -0.5 * LOG2E), gk_na=_tile_row(w['g_kn_na'], 8),
        g_q_lora=w['g_q_lora'][:, None, :], g_kv_lora=w['g_kv_lora'][:, None, :],
        gq_mla=pad_head(w['g_qn_mla'], MLA_QK ** -0.5 * LOG2E), gk_mla=pad_head(w['g_kn_mla'], 1.0),
        gqp_mla=partner_gain(w['g_qn_mla'], MLA_QK ** -0.5 * LOG2E), gkp_mla=partner_gain(w['g_kn_mla'], 1.0),
        seg=jnp.asarray(seg).astype(BF16),
        gq_win=_tile_row(w['g_qn_win'], 8, HEAD_DIM ** -0.5 * LOG2E), gk_win=_tile_row(w['g_kn_win'], 2),
        sink=w['sink_win'] * LOG2E,
        w_out=w['w_out'].astype(BF16), w_gate=w['w_gate'].astype(BF16), w_up=w['w_up'].astype(BF16),
        w_conv=wc, w_down=w['w_down'].astype(BF16),
    )


def _na_bias_rows(rpb):
    lead = rpb.shape[:-2]
    ext = jnp.concatenate([jnp.repeat(rpb[..., :1], 48, axis=-1), rpb, jnp.repeat(rpb[..., -1:], 49, axis=-1)], axis=-1)
    return jnp.concatenate([ext, jnp.full(lead + (1, LANES), NEG, F32)], axis=-2)


@functools.lru_cache(maxsize=None)
def _dft_tables():
    def cs(n):
        k = (np.arange(n)[:, None] * np.arange(n)[None, :]) % n
        ang = 2.0 * np.pi * k.astype(np.float64) / n
        return np.cos(ang), np.sin(ang)

    out = {}
    for name, n in (('ctx', SEQ), ('lat', DEC_SEQ)):
        c, s = cs(n)
        out['cs_' + name], out['ss_' + name] = c, s
    out['cc'], out['sc'] = cs(FN_CH)
    return out


@functools.lru_cache(maxsize=None)
def _rope_tables():
    t = np.arange(DEC_SEQ)
    quarter = MLA_ROPE // 4
    inv = ROPE_BASE ** (-np.arange(quarter, dtype=np.float64) / quarter)
    j = np.arange(MLA_ROPE)
    pos = np.where(j[None, :] < MLA_ROPE // 2, (t // GRID_W)[:, None], (t % GRID_W)[:, None]).astype(np.float64)
    ang = pos * inv[j % quarter][None, :]
    sign = np.where((j % 32) < 16, -1.0, 1.0)
    cos = np.cos(ang)
    sin = np.sin(ang) * sign[None, :]
    return np.tile(cos, (1, 2)).astype(np.float32), np.tile(sin, (1, 2)).astype(np.float32)


def kernel(x_prompt, x_sample, cache_na_k, cache_na_v, cache_mla_ckv, cache_mla_krope, cache_win_k, cache_win_v,
           c, c_ctx, w_mod, b_mod, g_mix, g_ffn, w_in, g_qn_na, g_kn_na, rpb_na, g_q_lora, w_q_up, g_kv_lora,
           w_kv_up, g_qn_mla, g_kn_mla, g_qn_win, g_kn_win, sink_win, w_out, w_gate, w_up, w_conv, w_down):
    w = dict(g_mix=g_mix, g_ffn=g_ffn, w_in=w_in, g_qn_na=g_qn_na, g_kn_na=g_kn_na, g_q_lora=g_q_lora,
             w_q_up=w_q_up, g_kv_lora=g_kv_lora, w_kv_up=w_kv_up, g_qn_mla=g_qn_mla, g_kn_mla=g_kn_mla,
             g_qn_win=g_qn_win, g_kn_win=g_kn_win, sink_win=sink_win, w_out=w_out, w_gate=w_gate, w_up=w_up,
             w_conv=w_conv, w_down=w_down)
    nb_ctx, nb_lat = x_prompt.shape[0], x_sample.shape[0]
    dft = {k: jnp.asarray(v, F32).astype(BF16) for k, v in _dft_tables().items()}
    rope_tabs = tuple(jnp.asarray(t) for t in _rope_tables())

    mod = _modulation(jnp.concatenate([c_ctx[None, :], c], axis=0), w_mod, b_mod)

    xp = x_prompt.reshape(nb_ctx * SEQ, D_MODEL)
    xs = x_sample.reshape(nb_lat * DEC_SEQ, D_MODEL)
    caches = []
    p = _prep(w)
    nctx = nb_lat * DEPTH * PAST_LEN
    dup = lambda t: jnp.concatenate([t[..., :64], t[..., :64], t[..., 64:], t[..., 64:]], axis=-1).astype(BF16)
    kna_c = cache_na_k.reshape(nb_lat, DEPTH, PAST_LEN, 512).astype(BF16)
    vna_c = cache_na_v.reshape(nb_lat, DEPTH, PAST_LEN, 512).astype(BF16)
    kw_c = dup(cache_win_k.reshape(nb_lat, DEPTH, PAST_LEN, 128))
    vw_c = dup(cache_win_v.reshape(nb_lat, DEPTH, PAST_LEN, 128))
    ckv_c = cache_mla_ckv.reshape(nctx, MLA_KV_LORA)
    kr_c = jnp.pad(cache_mla_krope.reshape(nctx, MLA_ROPE), ((0, 0), (0, LANES - MLA_ROPE)))
    bias_ext = _na_bias_rows(rpb_na * LOG2E)
    for l in range(DEPTH):

        pr = _project(xp, mod, l, p, False, None)
        caches.append(pr[10:])
        o4 = _ctx_attention(pr, p['sink'], dft, l)
        x1, h2 = _out_project(o4, xp, mod, l, p, False)
        xp = _ffn(h2, x1, mod, l, p, False)

        qna, kna, vna, qm, km, vm, qw, kw, vw, fv = _project(xs, mod, l, p, True, rope_tabs)
        km_c, vm_c = _mla_ctx(ckv_c, kr_c, p, l, nb_lat)
        o_a = _na_attention(qna, kna, vna, kna_c, vna_c, bias_ext, nb_lat, l)
        o_b = _mla_attention(qm, km, vm, km_c, vm_c, nb_lat)
        o_c = _win_attention(p['sink'], qw, kw, vw, kw_c, vw_c, nb_lat, l)
        o_d = _fourier(fv, dft, nb_lat)
        x1, h2 = _out_project((o_a, o_b, o_c, o_d), xs, mod, l, p, True)
        xs = _ffn(h2, x1, mod, l, p, True)

    def stack(idx, shape):
        return jnp.stack([caches[l][idx].reshape((nb_ctx, SEQ) + shape) for l in range(DEPTH)], axis=1)

    return (xp.reshape(nb_ctx, SEQ, D_MODEL), xs.reshape(nb_lat, DEC_SEQ, D_MODEL),
            stack(0, (8, HEAD_DIM)), stack(1, (8, HEAD_DIM)), stack(2, (MLA_KV_LORA,)), stack(3, (MLA_ROPE,)),
            stack(4, (2, HEAD_DIM)), stack(5, (2, HEAD_DIM)))
```
